```python
import math
import jax, jax.numpy as jnp
from jax import lax
import numpy as np

D_MODEL = 1024
BATCH = 8
SEQ = 4096
DEPTH = 1

GRID_W = 64
CTX_LEN = 256
DA_HEADS = 4
DA_QK_DIM = 64
DA_V_DIM = 2 * DA_QK_DIM
DA_WIDTH = DA_HEADS * DA_V_DIM
Q_BLOCK = 128
ROPE_THETA = 10000.0
ROPE_FREQS = DA_QK_DIM // 4
HY_WIDTH = D_MODEL // 2
HY_EMB_DIM = 33
HY_BANDS = (HY_EMB_DIM - 1) // 2
HY_FILTER_WIDTH = 64
HY_DECAY_TARGET = 1e-2
HY_FAST_DECAY = 0.3
HY_SLOW_DECAY = 1.5
N_EXPERTS = 32
TOP_K = 4
D_FF_EXPERT = D_MODEL
SWIGLU_LIMIT = 7.0
SWIGLU_ALPHA = 1.702
EXPERT_BLOCK = 256
EPS = 1e-6
Q_COLS = DA_HEADS * 2 * DA_QK_DIM
K_COLS = DA_HEADS * 2 * DA_QK_DIM
V_COLS = DA_WIDTH
HY_COLS = 3 * HY_WIDTH
GATE_COLS = 2 * D_MODEL
IN_COLS = Q_COLS + K_COLS + V_COLS + HY_COLS + GATE_COLS
IN_SPLITS = (Q_COLS, Q_COLS + K_COLS, Q_COLS + K_COLS + V_COLS, Q_COLS + K_COLS + V_COLS + HY_COLS)

kernel_name = "hybrid_diffattn_hyena_moe_dit_block"


def rms_norm(x, g):
    xf = x.astype(jnp.float32)
    y = xf * lax.rsqrt(jnp.mean(xf * xf, axis=-1, keepdims=True) + EPS)
    return (y * g.astype(jnp.float32)).astype(x.dtype)


def modulate(x, g, shift, scale):
    return rms_norm(x, g) * (1 + scale) + shift


def axial_rope_tables(rows):
    row = jnp.repeat(jnp.arange(rows), GRID_W)
    col = jnp.tile(jnp.arange(GRID_W), rows)
    pos = jnp.stack([row, col], axis=-1).astype(jnp.float32)
    freqs = ROPE_THETA ** (-jnp.arange(ROPE_FREQS, dtype=jnp.float32) / ROPE_FREQS)
    ang = pos[:, :, None] * freqs
    return jnp.cos(ang), jnp.sin(ang)


def rope_2d(x, cos, sin):
    xr = x.reshape(*x.shape[:-1], 2, 2, ROPE_FREQS)
    x1, x2 = xr[..., 0, :], xr[..., 1, :]
    c = cos[None, :, None, None]
    s = sin[None, :, None, None]
    out = jnp.stack([x1 * c - x2 * s, x2 * c + x1 * s], axis=-2)
    return out.reshape(x.shape).astype(x.dtype)


def split_proj(p):
    b, L = p.shape[:2]
    q, k, v, hy, gates = jnp.split(p, IN_SPLITS, axis=-1)
    return (q.reshape(b, L, DA_HEADS, 2, DA_QK_DIM),
            k.reshape(b, L, DA_HEADS, 2, DA_QK_DIM),
            v.reshape(b, L, DA_HEADS, DA_V_DIM),
            hy, gates)


def diff_attention(q, k, v, lam):
    b, sq, h, m, dh = q.shape
    nb = sq // Q_BLOCK
    qb = jnp.moveaxis(q.reshape(b, nb, Q_BLOCK, h, m, dh), 1, 0)
    scale = DA_QK_DIM ** -0.5

    def one_block(qblk):
        s = jnp.einsum("bqhmd,bkhmd->bhmqk", qblk, k).astype(jnp.float32) * scale
        p = jax.nn.softmax(s, axis=-1)
        p = p[:, :, 0] - lam * p[:, :, 1]
        return jnp.einsum("bhqk,bkhe->bqhe", p.astype(v.dtype), v)

    o = lax.map(one_block, qb)
    return jnp.moveaxis(o, 0, 1).reshape(b, sq, h, v.shape[-1])


def diff_out(o, subln_g, lambda_init):
    b, L = o.shape[:2]
    return (rms_norm(o, subln_g) * (1.0 - lambda_init)).reshape(b, L, DA_WIDTH)


def short_conv3(u, w, bias):
    up = jnp.pad(u, ((0, 0), (1, 1), (0, 0)))
    return up[:, :-2] * w[0] + up[:, 1:-1] * w[1] + up[:, 2:] * w[2] + bias


def implicit_filter(L, w1, b1, w2, b2, w3, b3, w4, freq):
    t = jnp.linspace(0.0, 1.0, L, dtype=jnp.float32)[:, None]
    w = 2.0 * math.pi * jnp.arange(L, dtype=jnp.float32)[:, None] / L
    f = jnp.linspace(1e-4, HY_BANDS - 1, HY_BANDS, dtype=jnp.float32)
    z = jnp.concatenate([t, jnp.cos(f * w), -jnp.sin(f * w)], axis=-1)
    freq = freq.astype(jnp.float32)
    hdn = jnp.sin(freq * (z @ w1 + b1))
    hdn = jnp.sin(freq * (hdn @ w2 + b2))
    hdn = jnp.sin(freq * (hdn @ w3 + b3))
    k = (hdn @ w4).astype(jnp.float32).reshape(L, 2, HY_WIDTH)
    deltas = jnp.abs(jnp.linspace(math.log(HY_DECAY_TARGET) / HY_SLOW_DECAY,
                                  math.log(HY_DECAY_TARGET) / HY_FAST_DECAY,
                                  HY_WIDTH, dtype=jnp.float32))
    decay = jnp.exp(-t * deltas)
    return k * decay[:, None, :]


def long_conv_bidir(u, k_fwd, k_bwd, bias):
    L, W = k_fwd.shape
    k_full = jnp.concatenate([k_fwd, jnp.zeros((1, W), jnp.float32), k_bwd[:0:-1]], axis=0)
    kf = jnp.fft.rfft(k_full, n=2 * L, axis=0)
    uf = jnp.fft.rfft(u.astype(jnp.float32), n=2 * L, axis=1)
    y = jnp.fft.irfft(uf * kf[None], n=2 * L, axis=1)[:, :L]
    return (y + u.astype(jnp.float32) * bias.astype(jnp.float32)).astype(u.dtype)


def hyena_branch(hy, conv_w, conv_b, fw1, fb1, fw2, fb2, fw3, fb3, fw4, ffreq, hbias):
    L = hy.shape[1]
    z = short_conv3(hy, conv_w, conv_b)
    x0, x1, v = jnp.split(z, 3, axis=-1)
    k = implicit_filter(L, fw1, fb1, fw2, fb2, fw3, fb3, fw4, ffreq)
    return x0 * long_conv_bidir(x1 * v, k[:, 0], k[:, 1], hbias)


def merge_branches(oa, ob, gates, w_up_a, w_up_b, w_out):
    ga, gb = jnp.split(jax.nn.sigmoid(gates), 2, axis=-1)
    return (ga * (oa @ w_up_a) + gb * (ob @ w_up_b)) @ w_out


def moe_ffn(h, router_w, router_b, w1, b1, w2, b2):
    b, L, d = h.shape
    xt = h.reshape(-1, d)
    T = xt.shape[0]
    logits = (xt @ router_w + router_b).astype(jnp.float32)
    top_v, top_i = lax.top_k(logits, TOP_K)
    wts = jax.nn.softmax(top_v, axis=-1)
    A = T * TOP_K
    e_flat = top_i.reshape(-1)
    tok_flat = jnp.arange(A, dtype=jnp.int32) // TOP_K
    w_flat = wts.reshape(-1)
    order = jnp.argsort(e_flat)
    e_s, tok_s, w_s = e_flat[order], tok_flat[order], w_flat[order]
    counts = jnp.bincount(e_flat, length=N_EXPERTS)
    padded = (counts + EXPERT_BLOCK - 1) // EXPERT_BLOCK * EXPERT_BLOCK
    pend = jnp.cumsum(padded)
    pstart = pend - padded
    start = jnp.cumsum(counts) - counts
    dest = pstart[e_s] + jnp.arange(A, dtype=jnp.int32) - start[e_s]
    n_rows = A + N_EXPERTS * EXPERT_BLOCK
    n_blocks = n_rows // EXPERT_BLOCK
    row_tok = jnp.full((n_rows,), T, jnp.int32).at[dest].set(tok_s)
    row_w = jnp.zeros((n_rows,), jnp.float32).at[dest].set(w_s)
    block_e = jnp.minimum(jnp.searchsorted(pend, jnp.arange(n_blocks) * EXPERT_BLOCK, side="right"),
                          N_EXPERTS - 1)
    x_pad = jnp.concatenate([xt, jnp.zeros((1, d), xt.dtype)], axis=0)
    xb = x_pad[row_tok].reshape(n_blocks, EXPERT_BLOCK, d)

    def expert_block(args):
        xblk, e = args
        gu = xblk @ w1[e] + b1[e]
        gate, up = jnp.split(gu, 2, axis=-1)
        gate = jnp.minimum(gate, SWIGLU_LIMIT)
        up = jnp.clip(up, -SWIGLU_LIMIT, SWIGLU_LIMIT)
        act = (up + 1) * gate * jax.nn.sigmoid(SWIGLU_ALPHA * gate)
        return act @ w2[e] + b2[e]

    yb = lax.map(expert_block, (xb, block_e)).reshape(n_rows, d)
    y = jnp.zeros((T + 1, d), jnp.float32).at[row_tok].add(yb.astype(jnp.float32) * row_w[:, None])
    return y[:T].reshape(b, L, d).astype(h.dtype)


def setup_inputs(seed: int = 0) -> dict:
    key = jax.random.key(seed)
    ks = iter(jax.random.split(key, 40))

    def nrm(shape, std):
        return std * jax.random.normal(next(ks), shape, jnp.float32)

    L, D, F = DEPTH, D_MODEL, D_FF_EXPERT
    return {
        "x": nrm((BATCH, SEQ, D), 1.0),
        "c": nrm((BATCH, D), 1.0),
        "ctx": nrm((BATCH, CTX_LEN, D), 1.0),
        "c_ctx": nrm((D,), 1.0),
        "ada_w": nrm((L, D, 6 * D), 0.5 * D ** -0.5),
        "ada_b": nrm((L, 6 * D), 0.02),
        "norm1_g": 1.0 + nrm((L, D), 0.05),
        "norm2_g": 1.0 + nrm((L, D), 0.05),
        "w_in": nrm((L, D, IN_COLS), D ** -0.5),
        "b_in": nrm((L, IN_COLS), 0.02),
        "q_norm_g": 1.0 + nrm((L, DA_QK_DIM), 0.05),
        "k_norm_g": 1.0 + nrm((L, DA_QK_DIM), 0.05),
        "lambda_q1": nrm((L, DA_QK_DIM), 0.1),
        "lambda_k1": nrm((L, DA_QK_DIM), 0.1),
        "lambda_q2": nrm((L, DA_QK_DIM), 0.1),
        "lambda_k2": nrm((L, DA_QK_DIM), 0.1),
        "subln_g": 1.0 + nrm((L, DA_V_DIM), 0.05),
        "conv_w": nrm((L, 3, HY_COLS), 3 ** -0.5),
        "conv_b": nrm((L, HY_COLS), 0.02),
        "filt_w1": nrm((L, HY_EMB_DIM, HY_FILTER_WIDTH), HY_EMB_DIM ** -0.5),
        "filt_b1": nrm((L, HY_FILTER_WIDTH), 0.1),
        "filt_w2": nrm((L, HY_FILTER_WIDTH, HY_FILTER_WIDTH), HY_FILTER_WIDTH ** -0.5),
        "filt_b2": nrm((L, HY_FILTER_WIDTH), 0.1),
        "filt_w3": nrm((L, HY_FILTER_WIDTH, HY_FILTER_WIDTH), HY_FILTER_WIDTH ** -0.5),
        "filt_b3": nrm((L, HY_FILTER_WIDTH), 0.1),
        "filt_w4": nrm((L, HY_FILTER_WIDTH, 2 * HY_WIDTH), 0.1 * HY_FILTER_WIDTH ** -0.5),
        "filt_freq": 1.0 + nrm((L, HY_FILTER_WIDTH), 0.05),
        "hyena_bias": nrm((L, HY_WIDTH), 0.1),
        "w_up_a": nrm((L, DA_WIDTH, D), DA_WIDTH ** -0.5),
        "w_up_b": nrm((L, HY_WIDTH, D), HY_WIDTH ** -0.5),
        "w_out": nrm((L, D, D), D ** -0.5),
        "router_w": nrm((L, D, N_EXPERTS), D ** -0.5),
        "router_b": nrm((L, N_EXPERTS), 0.01),
        "exp_w1": nrm((L, N_EXPERTS, D, 2 * F), D ** -0.5),
        "exp_b1": nrm((L, N_EXPERTS, 2 * F), 0.02),
        "exp_w2": nrm((L, N_EXPERTS, F, D), F ** -0.5),
        "exp_b2": nrm((L, N_EXPERTS, D), 0.02),
    }


def reference(x, c, ctx, c_ctx, ada_w, ada_b, norm1_g, norm2_g, w_in, b_in, q_norm_g, k_norm_g,
              lambda_q1, lambda_k1, lambda_q2, lambda_k2, subln_g, conv_w, conv_b,
              filt_w1, filt_b1, filt_w2, filt_b2, filt_w3, filt_b3, filt_w4, filt_freq, hyena_bias,
              w_up_a, w_up_b, w_out, router_w, router_b, exp_w1, exp_b1, exp_w2, exp_b2):
    ROWS = x.shape[1] // GRID_W
    cos, sin = axial_rope_tables(ROWS)
    xc = ctx
    for l in range(DEPTH):
        lambda_init = 0.8 - 0.6 * math.exp(-0.3 * l)
        lam = (jnp.exp(jnp.sum(lambda_q1[l].astype(jnp.float32) * lambda_k1[l].astype(jnp.float32)))
               - jnp.exp(jnp.sum(lambda_q2[l].astype(jnp.float32) * lambda_k2[l].astype(jnp.float32)))
               + lambda_init)
        sh1, sc1, g1, sh2, sc2, g2 = [m[:, None, :] for m in
                                      jnp.split(jax.nn.silu(c) @ ada_w[l] + ada_b[l], 6, axis=-1)]
        sh1c, sc1c, g1c, sh2c, sc2c, g2c = jnp.split(jax.nn.silu(c_ctx) @ ada_w[l] + ada_b[l], 6, axis=-1)
        hy_params = (conv_w[l], conv_b[l], filt_w1[l], filt_b1[l], filt_w2[l], filt_b2[l],
                     filt_w3[l], filt_b3[l], filt_w4[l], filt_freq[l], hyena_bias[l])
        merge_params = (w_up_a[l], w_up_b[l], w_out[l])
        moe_params = (router_w[l], router_b[l], exp_w1[l], exp_b1[l], exp_w2[l], exp_b2[l])

        px = modulate(x, norm1_g[l], sh1, sc1) @ w_in[l] + b_in[l]
        pc = modulate(xc, norm1_g[l], sh1c, sc1c) @ w_in[l] + b_in[l]
        qx, kx, vx, hyx, gx = split_proj(px)
        qc, kc, vc, hyc, gc = split_proj(pc)
        qx = rope_2d(rms_norm(qx, q_norm_g[l]), cos, sin)
        kx = rope_2d(rms_norm(kx, k_norm_g[l]), cos, sin)
        qc = rms_norm(qc, q_norm_g[l])
        kc = rms_norm(kc, k_norm_g[l])
        k_all = jnp.concatenate([kc, kx], axis=1)
        v_all = jnp.concatenate([vc, vx], axis=1)
        oa = diff_out(diff_attention(qx, k_all, v_all, lam), subln_g[l], lambda_init)
        ob = hyena_branch(hyx, *hy_params)
        x_new = x + g1 * merge_branches(oa, ob, gx, *merge_params)
        x_new = x_new + g2 * moe_ffn(modulate(x_new, norm2_g[l], sh2, sc2), *moe_params)

        if l < DEPTH - 1:
            oac = diff_out(diff_attention(qc, kc, vc, lam), subln_g[l], lambda_init)
            obc = hyena_branch(hyc, *hy_params)
            xc = xc + g1c * merge_branches(oac, obc, gc, *merge_params)
            xc = xc + g2c * moe_ffn(modulate(xc, norm2_g[l], sh2c, sc2c), *moe_params)
        x = x_new
    return x
```

```python
import functools
import math

import jax
import jax.numpy as jnp
import numpy as np
from jax import lax
from jax.experimental import pallas as pl
from jax.experimental.pallas import tpu as pltpu

F32 = jnp.float32
BF16 = jnp.bfloat16
HIGHEST = lax.Precision.HIGHEST

GRID_W = 64
DA_HEADS = 4
DA_QK_DIM = 64
DA_V_DIM = 2 * DA_QK_DIM
ROPE_THETA = 10000.0
ROPE_FREQS = DA_QK_DIM // 4
HY_DECAY_TARGET = 1e-2
HY_FAST_DECAY = 0.3
HY_SLOW_DECAY = 1.5
TOP_K = 4
SWIGLU_LIMIT = 7.0
SWIGLU_ALPHA = 1.702
EPS = 1e-6

LANES = 128
V7X_VMEM_LIMIT_BYTES = 48 * 1024 * 1024
FFT_N2 = 128
EXPERT_BLOCK = 256


def _cparams(*sem):
    return pltpu.CompilerParams(dimension_semantics=sem, vmem_limit_bytes=V7X_VMEM_LIMIT_BYTES)


def _sigmoid(x):
    return 1.0 / (1.0 + jnp.exp(-x))


def _rms(x, width):
    return x * lax.rsqrt(jnp.sum(x * x, axis=-1, keepdims=True) * (1.0 / width) + EPS)


def _ada_kernel(c_ref, w_ref, b_ref, o_ref):
    c = c_ref[...]
    s = c * _sigmoid(c)
    o_ref[...] = jnp.dot(s, w_ref[...], preferred_element_type=F32, precision=HIGHEST) + b_ref[...]


def _ada(cc, w, b):
    R, D = cc.shape
    N = w.shape[1]
    tn = D
    return pl.pallas_call(
        _ada_kernel,
        out_shape=jax.ShapeDtypeStruct((R, N), F32),
        grid=(N // tn,),
        in_specs=[pl.BlockSpec((R, D), lambda j: (0, 0)),
                  pl.BlockSpec((D, tn), lambda j: (0, j)),
                  pl.BlockSpec((1, tn), lambda j: (0, j))],
        out_specs=pl.BlockSpec((R, tn), lambda j: (0, j)),
        compiler_params=_cparams("arbitrary"),
        name="ada",
    )(cc, w, b)


def _in_kernel(spec, use_rope, qscale, x_ref, sc_ref, sh_ref, g_ref, w_ref, b_ref, qg_ref, kg_ref,
               bd_ref, *rest):
    if use_rope:
        cos_ref, sin_ref, *outs = rest
    else:
        outs = rest
    x = x_ref[0]
    h = _rms(x, x.shape[-1]) * g_ref[...]
    h = (h * (1.0 + sc_ref[0]) + sh_ref[0]).astype(BF16)
    for (kind, c0, cw), o_ref in zip(spec, outs):
        y = jnp.dot(h, w_ref[:, c0:c0 + cw], preferred_element_type=F32) + b_ref[:, c0:c0 + cw]
        if kind in ("q", "k", "kc"):
            y2 = y * y
            hi = y2.astype(BF16)
            lo = (y2 - hi.astype(F32)).astype(BF16)
            ssq = (jnp.dot(hi, bd_ref[...], preferred_element_type=F32)
                   + jnp.dot(lo, bd_ref[...], preferred_element_type=F32))
            gn = qg_ref if kind == "q" else kg_ref
            y = y * lax.rsqrt(ssq * (1.0 / DA_QK_DIM) + EPS) * gn[...]
            if kind != "kc":
                lane = lax.broadcasted_iota(jnp.int32, y.shape, 1)
                partner = jnp.where((lane % (2 * ROPE_FREQS)) < ROPE_FREQS,
                                    pltpu.roll(y, cw - ROPE_FREQS, 1), pltpu.roll(y, ROPE_FREQS, 1))
                y = y * cos_ref[...] + partner * sin_ref[...]
            if kind == "q":
                y = y * qscale
        elif kind == "sigmoid":
            y = _sigmoid(y)
        o_ref[0] = y.astype(o_ref.dtype)


def _in_proj(x, sc, sh, g, w, b, qg, kg, bd, rope, spec, tm, qscale):
    B, S, D = x.shape
    N = w.shape[1]
    use_rope = rope is not None
    const2 = lambda i, bb: (0, 0)
    in_specs = [pl.BlockSpec((1, tm, D), lambda i, bb: (bb, i, 0)),
                pl.BlockSpec((1, 1, D), lambda i, bb: (bb, 0, 0)),
                pl.BlockSpec((1, 1, D), lambda i, bb: (bb, 0, 0)),
                pl.BlockSpec((1, D), const2),
                pl.BlockSpec((D, N), const2),
                pl.BlockSpec((1, N), const2),
                pl.BlockSpec(qg.shape, const2),
                pl.BlockSpec(kg.shape, const2),
                pl.BlockSpec(bd.shape, const2)]
    args = [x, sc, sh, g, w, b, qg, kg, bd]
    if use_rope:
        cw = rope[0].shape[1]
        in_specs += [pl.BlockSpec((tm, cw), lambda i, bb: (i, 0))] * 2
        args += list(rope)
    out_shape = [jax.ShapeDtypeStruct((B, S, cw_), BF16) for (_, _, cw_) in spec]
    out_specs = [pl.BlockSpec((1, tm, cw_), lambda i, bb: (bb, i, 0)) for (_, _, cw_) in spec]
    return pl.pallas_call(
        functools.partial(_in_kernel, spec, use_rope, qscale),
        out_shape=out_shape,
        grid=(S // tm, B),
        in_specs=in_specs,
        out_specs=out_specs,
        compiler_params=_cparams("arbitrary", "arbitrary"),
        name="in_proj_rope" if use_rope else "in_proj_ctx",
    )(*args)


def _rope_tables(S, width):
    t = np.arange(S)
    pos = np.stack([t // GRID_W, t % GRID_W], axis=-1).astype(np.float32)
    freqs = (np.float32(ROPE_THETA) ** (-np.arange(ROPE_FREQS, dtype=np.float32) / ROPE_FREQS)).astype(np.float32)
    lane = np.arange(width)
    d = lane % DA_QK_DIM
    axis = d // (2 * ROPE_FREQS)
    half = (d % (2 * ROPE_FREQS)) // ROPE_FREQS
    f = d % ROPE_FREQS
    ang = (pos[:, axis] * freqs[f][None, :]).astype(np.float32)
    cos = np.cos(ang.astype(np.float64)).astype(np.float32)
    sin = np.sin(ang.astype(np.float64)).astype(np.float32)
    sin = np.where(half[None, :] == 0, -sin, sin)
    return jnp.asarray(cos), jnp.asarray(sin)


def _attn_kernel(tq, lam_ref, q_ref, k_ref, v_ref, g_ref, o_ref):
    q = q_ref[0]
    k = k_ref[0]
    v = v_ref[0]
    lane = lax.broadcasted_iota(jnp.int32, q.shape, 1)
    zero = jnp.zeros_like(q)
    qs = jnp.concatenate([jnp.where(lane < DA_QK_DIM, q, zero),
                          jnp.where(lane >= DA_QK_DIM, q, zero)], axis=0)
    s = lax.dot_general(qs, k, (((1,), (1,)), ((), ())), preferred_element_type=F32)
    m = jnp.max(s, axis=-1, keepdims=True)
    e = jnp.exp2(s - m)
    r = 1.0 / jnp.sum(e, axis=-1, keepdims=True)
    lam = lam_ref[0]
    p = e[:tq] * r[:tq] - e[tq:] * (lam * r[tq:])
    o = jnp.dot(p.astype(BF16), v, preferred_element_type=F32)
    o = _rms(o, o.shape[-1]) * g_ref[...]
    o_ref[0] = o.astype(o_ref.dtype)


def _attention(lam, q, k, v, g, tq):
    B, S, W = q.shape
    Sk = k.shape[1]
    H = W // LANES
    return pl.pallas_call(
        functools.partial(_attn_kernel, tq),
        out_shape=jax.ShapeDtypeStruct((B, S, W), BF16),
        grid=(B, H, S // tq),
        in_specs=[pl.BlockSpec(memory_space=pltpu.SMEM),
                  pl.BlockSpec((1, tq, LANES), lambda b, h, i: (b, i, h)),
                  pl.BlockSpec((1, Sk, LANES), lambda b, h, i: (b, 0, h)),
                  pl.BlockSpec((1, Sk, LANES), lambda b, h, i: (b, 0, h)),
                  pl.BlockSpec((1, LANES), lambda b, h, i: (0, 0))],
        out_specs=pl.BlockSpec((1, tq, LANES), lambda b, h, i: (b, i, h)),
        compiler_params=_cparams("arbitrary", "arbitrary", "arbitrary"),
        name="diff_attn",
    )(lam, q, k, v, g)


def _filter_kernel(z_ref, w1, b1, w2, b2, w3, b3, w4, fr, dec_ref, o_ref):
    dot = functools.partial(jnp.dot, preferred_element_type=F32, precision=HIGHEST)
    f = fr[...]
    h = jnp.sin(f * (dot(z_ref[...], w1[...]) + b1[...]))
    h = jnp.sin(f * (dot(h, w2[...]) + b2[...]))
    h = jnp.sin(f * (dot(h, w3[...]) + b3[...]))
    o_ref[...] = dot(h, w4[...]) * dec_ref[...]


def _implicit_filter(z, w1, b1, w2, b2, w3, b3, w4, fr, dec, tl):
    L = z.shape[0]
    N = w4.shape[1]
    full = lambda a: pl.BlockSpec(a.shape, lambda i: (0, 0))
    return pl.pallas_call(
        _filter_kernel,
        out_shape=jax.ShapeDtypeStruct((L, N), F32),
        grid=(L // tl,),
        in_specs=[pl.BlockSpec((tl, z.shape[1]), lambda i: (i, 0)),
                  full(w1), full(b1), full(w2), full(b2), full(w3), full(b3), full(w4), full(fr),
                  pl.BlockSpec((tl, N), lambda i: (i, 0))],
        out_specs=pl.BlockSpec((tl, N), lambda i: (i, 0)),
        compiler_params=_cparams("arbitrary"),
        name="hyena_filter",
    )(z, w1, b1, w2, b2, w3, b3, w4, fr, dec)


def _conv3(a, w_ref, b_ref):
    L = a.shape[0]
    row = lax.broadcasted_iota(jnp.int32, a.shape, 0)
    prev = jnp.where(row == 0, 0.0, pltpu.roll(a, 1, 0))
    nxt = jnp.where(row == L - 1, 0.0, pltpu.roll(a, L - 1, 0))
    return prev * w_ref[0:1, :] + a * w_ref[1:2, :] + nxt * w_ref[2:3, :] + b_ref[...]


def _gate_kernel(x1_ref, v_ref, w1_ref, wv_ref, b1_ref, bv_ref, u_ref):
    u = _conv3(x1_ref[0].astype(F32), w1_ref, b1_ref) * _conv3(v_ref[0].astype(F32), wv_ref, bv_ref)
    u_ref[0] = u.astype(u_ref.dtype)


def _hyena_gate(hy, conv_w, conv_b):
    B, L, W3 = hy.shape
    W = W3 // 3
    nw = W // LANES
    act = lambda off: pl.BlockSpec((1, L, LANES), lambda b, j: (b, 0, j + off))
    cw = lambda off: pl.BlockSpec((3, LANES), lambda b, j: (0, j + off))
    cb = lambda off: pl.BlockSpec((1, LANES), lambda b, j: (0, j + off))
    return pl.pallas_call(
        _gate_kernel,
        out_shape=jax.ShapeDtypeStruct((B, L, W), BF16),
        grid=(B, nw),
        in_specs=[act(nw), act(2 * nw), cw(nw), cw(2 * nw), cb(nw), cb(2 * nw)],
        out_specs=pl.BlockSpec((1, L, LANES), lambda b, j: (b, 0, j)),
        compiler_params=_cparams("arbitrary", "arbitrary"),
        name="hyena_gate",
    )(hy, hy, conv_w, conv_w, conv_b, conv_b)


def _outer_dft_kernel(precision, f_ref, x_ref, o_ref):
    o_ref[0] = jnp.dot(f_ref[...], x_ref[0], preferred_element_type=F32,
                       precision=precision).astype(o_ref.dtype)


def _outer_dft(f, x, out_dtype, tn, name):
    B, K, N = x.shape
    M = f.shape[0]
    precision = HIGHEST if x.dtype == F32 else None
    return pl.pallas_call(
        functools.partial(_outer_dft_kernel, precision),
        out_shape=jax.ShapeDtypeStruct((B, M, N), out_dtype),
        grid=(B, N // tn),
        in_specs=[pl.BlockSpec((M, K), lambda b, j: (0, 0)),
                  pl.BlockSpec((1, K, tn), lambda b, j: (b, 0, j))],
        out_specs=pl.BlockSpec((1, M, tn), lambda b, j: (b, 0, j)),
        compiler_params=_cparams("arbitrary", "arbitrary"),
        name=name,
    )(f, x)


def _twiddle(re, im, tr, ti, conj):
    if conj:
        return re * tr + im * ti, im * tr - re * ti
    return re * tr - im * ti, re * ti + im * tr


def _inner_spectrum_kernel(kb, a_ref, tr_ref, ti_ref, f_ref, o_ref):
    reps = a_ref.shape[-1] // LANES
    for j in range(kb):
        tr = jnp.concatenate([tr_ref[j]] * reps, axis=-1)
        ti = jnp.concatenate([ti_ref[j]] * reps, axis=-1)
        re, im = _twiddle(a_ref[0, 0, j], a_ref[0, 1, j], tr, ti, False)
        n2 = re.shape[0]
        x = jnp.dot(f_ref[...], jnp.concatenate([re, im], axis=0), preferred_element_type=F32,
                    precision=HIGHEST)
        o_ref[0, 0, j] = x[:n2]
        o_ref[0, 1, j] = x[n2:]


def _inner_conv_kernel(kb, a_ref, kf_ref, tr_ref, ti_ref, f_ref, fi_ref, o_ref):
    reps = a_ref.shape[-1] // LANES
    for j in range(kb):
        tr = jnp.concatenate([tr_ref[j]] * reps, axis=-1)
        ti = jnp.concatenate([ti_ref[j]] * reps, axis=-1)
        re, im = _twiddle(a_ref[0, 0, j].astype(F32), a_ref[0, 1, j].astype(F32), tr, ti, False)
        n2 = re.shape[0]
        x = jnp.dot(f_ref[...], jnp.concatenate([re, im], axis=0).astype(BF16), preferred_element_type=F32)
        xr, xi = x[:n2], x[n2:]
        kr, ki = kf_ref[0, 0, j], kf_ref[0, 1, j]
        yr = xr * kr - xi * ki
        yi = xr * ki + xi * kr
        y = jnp.dot(fi_ref[...], jnp.concatenate([yr, yi], axis=0).astype(BF16), preferred_element_type=F32)
        re, im = _twiddle(y[:n2], y[n2:], tr, ti, True)
        o_ref[0, 0, j] = re.astype(o_ref.dtype)
        o_ref[0, 1, j] = im.astype(o_ref.dtype)


def _inner_stage(a5, kf5, tr, ti, f2, f2i, kb):
    B, _, N1, N2, W = a5.shape
    blk = lambda: pl.BlockSpec((1, 2, kb, N2, W), lambda g, b: (b, 0, g, 0, 0))
    tw = pl.BlockSpec((kb, N2, LANES), lambda g, b: (g, 0, 0))
    mat = pl.BlockSpec((2 * N2, 2 * N2), lambda g, b: (0, 0))
    if kf5 is None:
        kern = functools.partial(_inner_spectrum_kernel, kb)
        in_specs, args, out_dtype, name = [blk(), tw, tw, mat], (a5, tr, ti, f2), F32, "hyena_filter_spectrum"
    else:
        kern = functools.partial(_inner_conv_kernel, kb)
        kf_spec = pl.BlockSpec((1, 2, kb, N2, W), lambda g, b: (0, 0, g, 0, 0))
        in_specs, args, out_dtype, name = ([blk(), kf_spec, tw, tw, mat, mat], (a5, kf5, tr, ti, f2, f2i),
                                           BF16, "hyena_inner_conv")
    return pl.pallas_call(
        kern,
        out_shape=jax.ShapeDtypeStruct(a5.shape, out_dtype),
        grid=(N1 // kb, B),
        in_specs=in_specs,
        out_specs=blk(),
        compiler_params=_cparams("arbitrary", "arbitrary"),
        name=name,
    )(*args)


def _hyena_out_kernel(x0_ref, w0_ref, b0_ref, y_ref, u_ref, hb_ref, o_ref):
    x0 = _conv3(x0_ref[0].astype(F32), w0_ref, b0_ref)
    o_ref[0] = (x0 * (y_ref[0] + u_ref[0].astype(F32) * hb_ref[...])).astype(o_ref.dtype)


def _hyena_out(hy, conv_w, conv_b, y, u, hbias):
    B, L, W = u.shape
    nw = W // LANES
    blk = pl.BlockSpec((1, L, LANES), lambda b, j: (b, 0, j))
    vec = lambda r: pl.BlockSpec((r, LANES), lambda b, j: (0, j))
    return pl.pallas_call(
        _hyena_out_kernel,
        out_shape=jax.ShapeDtypeStruct((B, L, W), BF16),
        grid=(B, nw),
        in_specs=[blk, vec(3), vec(1), blk, blk, vec(1)],
        out_specs=blk,
        compiler_params=_cparams("arbitrary", "arbitrary"),
        name="hyena_out",
    )(hy, conv_w, conv_b, y, u, hbias)


def _dft_constants(L):
    N = 2 * L
    N2 = FFT_N2
    N1 = N // N2
    k1 = np.arange(N1, dtype=np.float64)
    th1 = 2.0 * np.pi * np.outer(k1, k1) / N1
    fwd = np.concatenate([np.cos(th1), -np.sin(th1)], axis=0)
    inv = np.concatenate([np.cos(th1), -np.sin(th1)], axis=1)[: N1 // 2] / N
    n2 = np.arange(N2, dtype=np.float64)
    tw = 2.0 * np.pi * np.outer(k1, n2) / N
    tr = np.repeat(np.cos(tw)[:, :, None], LANES, axis=2)
    ti = np.repeat(-np.sin(tw)[:, :, None], LANES, axis=2)
    th2 = 2.0 * np.pi * np.outer(n2, n2) / N2
    c2, s2 = np.cos(th2), np.sin(th2)
    f2 = np.block([[c2, s2], [-s2, c2]])
    f2i = np.block([[c2, -s2], [s2, c2]])
    f = lambda a: jnp.asarray(a, F32)
    return dict(N1=N1, N2=N2, fwd=f(fwd), inv=f(inv), tr=f(tr), ti=f(ti), f2=f(f2), f2i=f(f2i))


def _filter_features(L, emb_dim):
    bands = (emb_dim - 1) // 2
    t = np.linspace(0.0, 1.0, L, dtype=np.float32)[:, None]
    w = (np.float32(2.0 * math.pi) * np.arange(L, dtype=np.float32)[:, None] / np.float32(L)).astype(np.float32)
    f = np.linspace(1e-4, bands - 1, bands, dtype=np.float32)
    fw = (f * w).astype(np.float32)
    z = np.concatenate([t, np.cos(fw.astype(np.float64)).astype(np.float32),
                        -np.sin(fw.astype(np.float64)).astype(np.float32)], axis=-1)
    return jnp.asarray(z), t


def _decay_window(t, W):
    deltas = np.abs(np.linspace(math.log(HY_DECAY_TARGET) / HY_SLOW_DECAY,
                                math.log(HY_DECAY_TARGET) / HY_FAST_DECAY, W, dtype=np.float32))
    dec = np.exp((-t * deltas).astype(np.float32).astype(np.float64)).astype(np.float32)
    return jnp.asarray(np.concatenate([dec, dec], axis=1))


def _hyena(hy, conv_w, conv_b, fw1, fb1, fw2, fb2, fw3, fb3, fw4, ffreq, hbias):
    B, L, W3 = hy.shape
    W = W3 // 3
    C = _dft_constants(L)
    N1, N2 = C["N1"], C["N2"]
    row = lambda a: a.reshape(1, -1)
    z, t = _filter_features(L, fw1.shape[0])
    kpad = LANES - fw1.shape[0]
    z = jnp.pad(z, ((0, 0), (0, kpad)))
    taps = _implicit_filter(z, jnp.pad(fw1, ((0, kpad), (0, 0))), row(fb1), fw2, row(fb2), fw3, row(fb3), fw4,
                            row(ffreq),
                            _decay_window(t, W), tl=min(L, 512))
    k_fwd, k_bwd = taps[:, :W], taps[:, W:]
    k_full = jnp.concatenate([k_fwd, jnp.zeros((1, W), F32), k_bwd[:0:-1]], axis=0)
    lanes = N2 * W
    tn = min(lanes, 8192)
    kf = _outer_dft(C["fwd"], k_full.reshape(1, N1, lanes), F32, tn, "hyena_filter_outer")
    kf5 = _inner_stage(kf.reshape(1, 2, N1, N2, W), None, C["tr"], C["ti"], C["f2"], None, kb=min(N1, 8))
    u = _hyena_gate(hy, conv_w, row(conv_b))
    a = _outer_dft(C["fwd"][:, : N1 // 2].astype(BF16), u.reshape(B, N1 // 2, lanes), BF16, tn,
                   "hyena_outer_fwd")
    bb = _inner_stage(a.reshape(B, 2, N1, N2, W), kf5, C["tr"], C["ti"], C["f2"].astype(BF16),
                      C["f2i"].astype(BF16), kb=min(N1, 8))
    y = _outer_dft(C["inv"].astype(BF16), bb.reshape(B, 2 * N1, lanes), F32, tn, "hyena_outer_inv")
    return _hyena_out(hy, conv_w, row(conv_b), y.reshape(B, L, W), u, row(hbias))


def _merge_kernel(x_ref, oa_ref, ob_ref, gt_ref, g1_ref, sc_ref, sh_ref, ng_ref, wa_ref, wb_ref, wo_ref,
                  rw_ref, rb_ref, xo_ref, h_ref, lg_ref):
    D = x_ref.shape[-1]
    gt = gt_ref[0]
    m = (gt[:, :D].astype(F32) * jnp.dot(oa_ref[0], wa_ref[...], preferred_element_type=F32)
         + gt[:, D:].astype(F32) * jnp.dot(ob_ref[0], wb_ref[...], preferred_element_type=F32))
    y = jnp.dot(m.astype(BF16), wo_ref[...], preferred_element_type=F32)
    xn = x_ref[0] + g1_ref[0] * y
    xo_ref[0] = xn
    h = _rms(xn, D) * ng_ref[...]
    h = h * (1.0 + sc_ref[0]) + sh_ref[0]
    h_ref[0] = h.astype(h_ref.dtype)
    lg_ref[0] = jnp.dot(h, rw_ref[...], preferred_element_type=F32, precision=HIGHEST) + rb_ref[...]


def _merge(x, oa, ob, gates, g1, sc2, sh2, ng, wa, wb, wo, rw, rb, tm):
    B, S, D = x.shape
    E = rw.shape[1]
    act = lambda w: pl.BlockSpec((1, tm, w), lambda b, i: (b, i, 0))
    mod = pl.BlockSpec((1, 1, D), lambda b, i: (b, 0, 0))
    full = lambda a: pl.BlockSpec(a.shape, lambda b, i: (0, 0))
    return pl.pallas_call(
        _merge_kernel,
        out_shape=[jax.ShapeDtypeStruct((B, S, D), F32), jax.ShapeDtypeStruct((B, S, D), BF16),
                   jax.ShapeDtypeStruct((B, S, E), F32)],
        grid=(B, S // tm),
        in_specs=[act(D), act(oa.shape[-1]), act(ob.shape[-1]), act(2 * D), mod, mod, mod, full(ng),
                  full(wa), full(wb), full(wo), full(rw), full(rb)],
        out_specs=[act(D), act(D), act(E)],
        compiler_params=_cparams("arbitrary", "arbitrary"),
        name="merge_norm_router",
    )(x, oa, ob, gates, g1, sc2, sh2, ng, wa, wb, wo, rw, rb)


def _moe_kernel(be_ref, nu_ref, x_ref, w1_ref, b1_ref, w2_ref, b2_ref, o_ref):
    del be_ref

    @pl.when(pl.program_id(0) < nu_ref[0])
    def _():
        F = w2_ref.shape[1]
        gu = jnp.dot(x_ref[...], w1_ref[0], preferred_element_type=F32) + b1_ref[0]
        gate = jnp.minimum(gu[:, :F], SWIGLU_LIMIT)
        up = jnp.clip(gu[:, F:], -SWIGLU_LIMIT, SWIGLU_LIMIT)
        act = (up + 1.0) * gate * _sigmoid(SWIGLU_ALPHA * gate)
        y = jnp.dot(act.astype(BF16), w2_ref[0], preferred_element_type=F32) + b2_ref[0]
        o_ref[...] = y.astype(o_ref.dtype)


def _moe_experts(block_e, n_used, xg, w1, b1, w2, b2, bm):
    R, D = xg.shape
    E, _, F2 = w1.shape
    F = w2.shape[1]
    grid_spec = pltpu.PrefetchScalarGridSpec(
        num_scalar_prefetch=2,
        grid=(R // bm,),
        in_specs=[pl.BlockSpec((bm, D), lambda i, be, nu: (i, 0)),
                  pl.BlockSpec((1, D, F2), lambda i, be, nu: (be[i], 0, 0)),
                  pl.BlockSpec((1, 1, F2), lambda i, be, nu: (be[i], 0, 0)),
                  pl.BlockSpec((1, F, D), lambda i, be, nu: (be[i], 0, 0)),
                  pl.BlockSpec((1, 1, D), lambda i, be, nu: (be[i], 0, 0))],
        out_specs=pl.BlockSpec((bm, D), lambda i, be, nu: (i, 0)),
    )
    return pl.pallas_call(
        _moe_kernel,
        out_shape=jax.ShapeDtypeStruct((R, D), BF16),
        grid_spec=grid_spec,
        compiler_params=_cparams("arbitrary"),
        name="moe_experts",
    )(block_e, n_used, xg, w1, b1.reshape(E, 1, F2), w2, b2.reshape(E, 1, D))


def _combine_kernel(x_ref, y_ref, w_ref, g_ref, o_ref):
    w = w_ref[0]
    acc = w[:, 0:1] * y_ref[0, 0].astype(F32)
    for k in range(1, y_ref.shape[0]):
        acc = acc + w[:, k:k + 1] * y_ref[k, 0].astype(F32)
    o_ref[0] = x_ref[0] + g_ref[0] * acc


def _combine(x, yg, wts, g2, tm):
    B, S, D = x.shape
    K = yg.shape[0]
    return pl.pallas_call(
        _combine_kernel,
        out_shape=jax.ShapeDtypeStruct((B, S, D), F32),
        grid=(B, S // tm),
        in_specs=[pl.BlockSpec((1, tm, D), lambda b, i: (b, i, 0)),
                  pl.BlockSpec((K, 1, tm, D), lambda b, i: (0, b, i, 0)),
                  pl.BlockSpec((1, tm, K), lambda b, i: (b, i, 0)),
                  pl.BlockSpec((1, 1, D), lambda b, i: (b, 0, 0))],
        out_specs=pl.BlockSpec((1, tm, D), lambda b, i: (b, i, 0)),
        compiler_params=_cparams("arbitrary", "arbitrary"),
        name="moe_combine",
    )(x, yg, wts, g2)


def _route(logits, bm):
    T, E = logits.shape
    top_v, top_i = lax.top_k(logits, TOP_K)
    wts = jax.nn.softmax(top_v, axis=-1)
    A = T * TOP_K
    e_flat = top_i.reshape(-1).astype(jnp.int32)
    order = jnp.argsort(e_flat, stable=True).astype(jnp.int32)
    e_s = e_flat[order]
    counts = jnp.sum((e_flat[:, None] == jnp.arange(E, dtype=jnp.int32)[None, :]).astype(jnp.int32), axis=0)
    padded = (counts + bm - 1) // bm * bm
    pend = jnp.cumsum(padded)
    pstart = pend - padded
    start = jnp.cumsum(counts) - counts
    dest = (pstart[e_s] + jnp.arange(A, dtype=jnp.int32) - start[e_s]).astype(jnp.int32)
    n_rows = A + E * bm
    n_blocks = n_rows // bm
    row_tok = jnp.zeros((n_rows,), jnp.int32).at[dest].set(order // TOP_K)
    pos = jnp.zeros((A,), jnp.int32).at[order].set(dest)
    block_e = jnp.minimum(jnp.searchsorted(pend, jnp.arange(n_blocks, dtype=jnp.int32) * bm, side="right"),
                          E - 1).astype(jnp.int32)
    n_used = (pend[-1:] // bm).astype(jnp.int32)
    return wts, row_tok, pos.reshape(T, TOP_K), block_e, n_used


def kernel(x, c, ctx, c_ctx, ada_w, ada_b, norm1_g, norm2_g, w_in, b_in, q_norm_g, k_norm_g, lambda_q1, lambda_k1, lambda_q2, lambda_k2, subln_g, conv_w, conv_b, filt_w1, filt_b1, filt_w2, filt_b2, filt_w3, filt_b3, filt_w4, filt_freq, hyena_bias, w_up_a, w_up_b, w_out, router_w, router_b, exp_w1, exp_b1, exp_w2, exp_b2):
    B, S, D = x.shape
    depth = ada_w.shape[0]
    assert depth == 1, "context-stream update between layers is not implemented"
    l = 0
    QC = DA_HEADS * 2 * DA_QK_DIM
    W = D // 2
    row = lambda a: a.reshape(1, -1)

    lambda_init = 0.8 - 0.6 * math.exp(-0.3 * l)
    lam = (jnp.exp(jnp.sum(lambda_q1[l] * lambda_k1[l])) - jnp.exp(jnp.sum(lambda_q2[l] * lambda_k2[l]))
           + lambda_init).reshape(1).astype(F32)

    R = -(-(B + 1) // 8) * 8
    cc = jnp.concatenate([c, c_ctx[None, :], jnp.zeros((R - B - 1, D), F32)], axis=0)
    mods = _ada(cc, ada_w[l], row(ada_b[l]))
    sh1, sc1, g1, sh2, sc2, g2 = [mods[:B, i * D:(i + 1) * D].reshape(B, 1, D) for i in range(6)]
    sh1c, sc1c = [jnp.broadcast_to(mods[B, i * D:(i + 1) * D].reshape(1, 1, D), (B, 1, D)) for i in range(2)]

    w_in_b = w_in[l].astype(BF16)
    qg = row(jnp.tile(q_norm_g[l], QC // DA_QK_DIM))
    kg = row(jnp.tile(k_norm_g[l], QC // DA_QK_DIM))
    grp = np.arange(QC) // DA_QK_DIM
    bd = jnp.asarray(grp[:, None] == grp[None, :], BF16)
    qscale = DA_QK_DIM ** -0.5 * math.log2(math.e)
    spec = (("q", 0, QC), ("k", QC, QC), ("plain", 2 * QC, QC), ("plain", 3 * QC, 3 * W),
            ("sigmoid", 3 * QC + 3 * W, 2 * D))
    qx, kx, vx, hy, gates = _in_proj(x, sc1, sh1, row(norm1_g[l]), w_in_b, row(b_in[l]), qg, kg, bd,
                                     _rope_tables(S, QC), spec, tm=min(S, 256), qscale=qscale)
    spec_c = (("kc", 0, QC), ("plain", QC, QC))
    kc, vc = _in_proj(ctx, sc1c, sh1c, row(norm1_g[l]), w_in_b[:, QC:3 * QC], row(b_in[l][QC:3 * QC]), qg, kg,
                      bd, None, spec_c, tm=ctx.shape[1], qscale=qscale)

    k_all = jnp.concatenate([kc, kx], axis=1)
    v_all = jnp.concatenate([vc, vx], axis=1)
    oa = _attention(lam, qx, k_all, v_all, row(subln_g[l] * (1.0 - lambda_init)), tq=128)
    ob = _hyena(hy, conv_w[l], conv_b[l], filt_w1[l], filt_b1[l], filt_w2[l], filt_b2[l], filt_w3[l],
                filt_b3[l], filt_w4[l], filt_freq[l], hyena_bias[l])
    x_new, h2, logits = _merge(x, oa, ob, gates, g1, sc2, sh2, row(norm2_g[l]), w_up_a[l].astype(BF16),
                               w_up_b[l].astype(BF16), w_out[l].astype(BF16), router_w[l], row(router_b[l]),
                               tm=min(S, 512))

    T = B * S
    bm = EXPERT_BLOCK
    wts, row_tok, pos, block_e, n_used = _route(logits.reshape(T, -1), bm)
    xg = jnp.take(h2.reshape(T, D), row_tok, axis=0)
    yb = _moe_experts(block_e, n_used, xg, exp_w1[l].astype(BF16), exp_b1[l], exp_w2[l].astype(BF16),
                      exp_b2[l], bm)
    yg = jnp.take(yb, pos.T, axis=0).reshape(TOP_K, B, S, D)
    return _combine(x_new, yg, wts.reshape(B, S, TOP_K), g2, tm=min(S, 512))
```

```python
import functools
import math

import jax
import jax.numpy as jnp
import numpy as np
from jax import lax
from jax.experimental import pallas as pl
from jax.experimental.pallas import tpu as pltpu

F32 = jnp.float32
BF16 = jnp.bfloat16
HIGHEST = lax.Precision.HIGHEST

GRID_W = 64
DA_HEADS = 4
DA_QK_DIM = 64
DA_V_DIM = 2 * DA_QK_DIM
ROPE_THETA = 10000.0
ROPE_FREQS = DA_QK_DIM // 4
HY_DECAY_TARGET = 1e-2
HY_FAST_DECAY = 0.3
HY_SLOW_DECAY = 1.5
TOP_K = 4
SWIGLU_LIMIT = 7.0
SWIGLU_ALPHA = 1.702
EPS = 1e-6

LANES = 128
V7X_VMEM_LIMIT_BYTES = 48 * 1024 * 1024
V7X_VMEM_LIMIT_MOE_BYTES = 56 * 1024 * 1024
FFT_N2 = 128
EXPERT_BLOCK = 256


def _cparams(*sem):
    return pltpu.CompilerParams(dimension_semantics=sem, vmem_limit_bytes=V7X_VMEM_LIMIT_BYTES)


def _sigmoid(x):
    return 1.0 / (1.0 + jnp.exp(-x))


def _rms(x, width):
    return x * lax.rsqrt(jnp.sum(x * x, axis=-1, keepdims=True) * (1.0 / width) + EPS)


def _ada_kernel(c_ref, w_ref, b_ref, o_ref):
    c = c_ref[...]
    s = c * _sigmoid(c)
    o_ref[...] = jnp.dot(s, w_ref[...], preferred_element_type=F32, precision=HIGHEST) + b_ref[...]


def _ada(cc, w, b):
    R, D = cc.shape
    N = w.shape[1]
    tn = D
    return pl.pallas_call(
        _ada_kernel,
        out_shape=jax.ShapeDtypeStruct((R, N), F32),
        grid=(N // tn,),
        in_specs=[pl.BlockSpec((R, D), lambda j: (0, 0)),
                  pl.BlockSpec((D, tn), lambda j: (0, j)),
                  pl.BlockSpec((1, tn), lambda j: (0, j))],
        out_specs=pl.BlockSpec((R, tn), lambda j: (0, j)),
        compiler_params=_cparams("arbitrary"),
        name="ada",
    )(cc, w, b)


def _in_kernel(spec, use_rope, qscale, x_ref, sc_ref, sh_ref, g_ref, w_ref, b_ref, qg_ref, kg_ref,
               bd_ref, *rest):
    if use_rope:
        cos_ref, sin_ref, *outs = rest
    else:
        outs = rest
    x = x_ref[0]
    h = _rms(x, x.shape[-1]) * g_ref[...]
    h = (h * (1.0 + sc_ref[0]) + sh_ref[0]).astype(BF16)
    for (kind, c0, cw), o_ref in zip(spec, outs):
        y = jnp.dot(h, w_ref[:, c0:c0 + cw], preferred_element_type=F32) + b_ref[:, c0:c0 + cw]
        if kind in ("q", "k", "kc"):
            y2 = y * y
            hi = y2.astype(BF16)
            lo = (y2 - hi.astype(F32)).astype(BF16)
            ssq = (jnp.dot(hi, bd_ref[...], preferred_element_type=F32)
                   + jnp.dot(lo, bd_ref[...], preferred_element_type=F32))
            gn = qg_ref if kind == "q" else kg_ref
            y = y * lax.rsqrt(ssq * (1.0 / DA_QK_DIM) + EPS) * gn[...]
            if kind != "kc":
                lane = lax.broadcasted_iota(jnp.int32, y.shape, 1)
                partner = jnp.where((lane % (2 * ROPE_FREQS)) < ROPE_FREQS,
                                    pltpu.roll(y, cw - ROPE_FREQS, 1), pltpu.roll(y, ROPE_FREQS, 1))
                y = y * cos_ref[...] + partner * sin_ref[...]
            if kind == "q":
                y = y * qscale
        elif kind == "sigmoid":
            y = _sigmoid(y)
        elif kind == "transposed":
            y = y.T
        o_ref[0] = y.astype(o_ref.dtype)


def _in_proj(x, sc, sh, g, w, b, qg, kg, bd, rope, spec, tm, qscale):
    B, S, D = x.shape
    N = w.shape[1]
    use_rope = rope is not None
    const2 = lambda i, bb: (0, 0)
    in_specs = [pl.BlockSpec((1, tm, D), lambda i, bb: (bb, i, 0)),
                pl.BlockSpec((1, 1, D), lambda i, bb: (bb, 0, 0)),
                pl.BlockSpec((1, 1, D), lambda i, bb: (bb, 0, 0)),
                pl.BlockSpec((1, D), const2),
                pl.BlockSpec((D, N), const2),
                pl.BlockSpec((1, N), const2),
                pl.BlockSpec(qg.shape, const2),
                pl.BlockSpec(kg.shape, const2),
                pl.BlockSpec(bd.shape, const2)]
    args = [x, sc, sh, g, w, b, qg, kg, bd]
    if use_rope:
        cw = rope[0].shape[1]
        in_specs += [pl.BlockSpec((tm, cw), lambda i, bb: (i, 0))] * 2
        args += list(rope)
    out_shape = [jax.ShapeDtypeStruct((B, cw_, S) if kind == "transposed" else (B, S, cw_), BF16)
                 for (kind, _, cw_) in spec]
    out_specs = [pl.BlockSpec((1, cw_, tm), lambda i, bb: (bb, 0, i)) if kind == "transposed"
                 else pl.BlockSpec((1, tm, cw_), lambda i, bb: (bb, i, 0)) for (kind, _, cw_) in spec]
    return pl.pallas_call(
        functools.partial(_in_kernel, spec, use_rope, qscale),
        out_shape=out_shape,
        grid=(S // tm, B),
        in_specs=in_specs,
        out_specs=out_specs,
        compiler_params=_cparams("arbitrary", "arbitrary"),
        name="in_proj_rope" if use_rope else "in_proj_ctx",
    )(*args)


def _rope_tables(S, width):
    t = np.arange(S)
    pos = np.stack([t // GRID_W, t % GRID_W], axis=-1).astype(np.float32)
    freqs = (np.float32(ROPE_THETA) ** (-np.arange(ROPE_FREQS, dtype=np.float32) / ROPE_FREQS)).astype(np.float32)
    lane = np.arange(width)
    d = lane % DA_QK_DIM
    axis = d // (2 * ROPE_FREQS)
    half = (d % (2 * ROPE_FREQS)) // ROPE_FREQS
    f = d % ROPE_FREQS
    ang = (pos[:, axis] * freqs[f][None, :]).astype(np.float32)
    cos = np.cos(ang.astype(np.float64)).astype(np.float32)
    sin = np.sin(ang.astype(np.float64)).astype(np.float32)
    sin = np.where(half[None, :] == 0, -sin, sin)
    return jnp.asarray(cos), jnp.asarray(sin)


def _attn_kernel(tq, nq, lam_ref, q_ref, k_ref, vt_ref, g_ref, o_ref):
    k = k_ref[0]
    vt = vt_ref[0, 0]
    dv = vt.shape[0] - 16
    lam = lam_ref[0]
    for c in range(nq):
        q = q_ref[0, c * tq:(c + 1) * tq, :]
        lane = lax.broadcasted_iota(jnp.int32, q.shape, 1)
        zero = jnp.zeros_like(q)
        qs = jnp.concatenate([jnp.where(lane < DA_QK_DIM, q, zero),
                              jnp.where(lane >= DA_QK_DIM, q, zero)], axis=0)
        st = lax.dot_general(k, qs, (((1,), (1,)), ((), ())), preferred_element_type=F32)
        m = jnp.max(st, axis=0, keepdims=True)
        e = jnp.exp2(st - m).astype(BF16)
        ot = jnp.dot(vt, e, preferred_element_type=F32)
        ot = ot[:dv] * (1.0 / ot[dv:dv + 1])
        o = (ot[:, :tq] - lam * ot[:, tq:]).T
        o = _rms(o, o.shape[-1]) * g_ref[...]
        o_ref[0, c * tq:(c + 1) * tq, :] = o.astype(o_ref.dtype)


def _attention(lam, q, k, vt, g, tq, nq):
    B, S, W = q.shape
    Sk = k.shape[1]
    H = W // LANES
    tb = tq * nq
    return pl.pallas_call(
        functools.partial(_attn_kernel, tq, nq),
        out_shape=jax.ShapeDtypeStruct((B, S, W), BF16),
        grid=(B, H, S // tb),
        in_specs=[pl.BlockSpec(memory_space=pltpu.SMEM),
                  pl.BlockSpec((1, tb, LANES), lambda b, h, i: (b, i, h)),
                  pl.BlockSpec((1, Sk, LANES), lambda b, h, i: (b, 0, h)),
                  pl.BlockSpec((1, 1, vt.shape[2], Sk), lambda b, h, i: (b, h, 0, 0)),
                  pl.BlockSpec((1, LANES), lambda b, h, i: (0, 0))],
        out_specs=pl.BlockSpec((1, tb, LANES), lambda b, h, i: (b, i, h)),
        compiler_params=_cparams("arbitrary", "arbitrary", "arbitrary"),
        name="diff_attn",
    )(lam, q, k, vt, g)


def _filter_kernel(z_ref, w1, b1, w2, b2, w3, b3, w4, fr, dec_ref, o_ref):
    dot = functools.partial(jnp.dot, preferred_element_type=F32, precision=HIGHEST)
    f = fr[...]
    h = jnp.sin(f * (dot(z_ref[...], w1[...]) + b1[...]))
    h = jnp.sin(f * (dot(h, w2[...]) + b2[...]))
    h = jnp.sin(f * (dot(h, w3[...]) + b3[...]))
    o_ref[...] = dot(h, w4[...]) * dec_ref[...]


def _implicit_filter(z, w1, b1, w2, b2, w3, b3, w4, fr, dec, tl):
    L = z.shape[0]
    N = w4.shape[1]
    full = lambda a: pl.BlockSpec(a.shape, lambda i: (0, 0))
    return pl.pallas_call(
        _filter_kernel,
        out_shape=jax.ShapeDtypeStruct((L, N), F32),
        grid=(L // tl,),
        in_specs=[pl.BlockSpec((tl, z.shape[1]), lambda i: (i, 0)),
                  full(w1), full(b1), full(w2), full(b2), full(w3), full(b3), full(w4), full(fr),
                  pl.BlockSpec((tl, N), lambda i: (i, 0))],
        out_specs=pl.BlockSpec((tl, N), lambda i: (i, 0)),
        compiler_params=_cparams("arbitrary"),
        name="hyena_filter",
    )(z, w1, b1, w2, b2, w3, b3, w4, fr, dec)


def _conv3(a, w_ref, b_ref):
    L = a.shape[0]
    row = lax.broadcasted_iota(jnp.int32, a.shape, 0)
    prev = jnp.where(row == 0, 0.0, pltpu.roll(a, 1, 0))
    nxt = jnp.where(row == L - 1, 0.0, pltpu.roll(a, L - 1, 0))
    return prev * w_ref[0:1, :] + a * w_ref[1:2, :] + nxt * w_ref[2:3, :] + b_ref[...]


def _gate_kernel(x1_ref, v_ref, w1_ref, wv_ref, b1_ref, bv_ref, u_ref):
    u = _conv3(x1_ref[0].astype(F32), w1_ref, b1_ref) * _conv3(v_ref[0].astype(F32), wv_ref, bv_ref)
    u_ref[0] = u.astype(u_ref.dtype)


def _hyena_gate(hy, conv_w, conv_b):
    B, L, W3 = hy.shape
    W = W3 // 3
    nw = W // LANES
    act = lambda off: pl.BlockSpec((1, L, LANES), lambda b, j: (b, 0, j + off))
    cw = lambda off: pl.BlockSpec((3, LANES), lambda b, j: (0, j + off))
    cb = lambda off: pl.BlockSpec((1, LANES), lambda b, j: (0, j + off))
    return pl.pallas_call(
        _gate_kernel,
        out_shape=jax.ShapeDtypeStruct((B, L, W), BF16),
        grid=(B, nw),
        in_specs=[act(nw), act(2 * nw), cw(nw), cw(2 * nw), cb(nw), cb(2 * nw)],
        out_specs=pl.BlockSpec((1, L, LANES), lambda b, j: (b, 0, j)),
        compiler_params=_cparams("arbitrary", "arbitrary"),
        name="hyena_gate",
    )(hy, hy, conv_w, conv_w, conv_b, conv_b)


def _outer_dft_kernel(precision, f_ref, x_ref, o_ref):
    o_ref[0] = jnp.dot(f_ref[...], x_ref[0], preferred_element_type=F32,
                       precision=precision).astype(o_ref.dtype)


def _outer_dft(f, x, out_dtype, tn, name):
    B, K, N = x.shape
    M = f.shape[0]
    precision = HIGHEST if x.dtype == F32 else None
    return pl.pallas_call(
        functools.partial(_outer_dft_kernel, precision),
        out_shape=jax.ShapeDtypeStruct((B, M, N), out_dtype),
        grid=(B, N // tn),
        in_specs=[pl.BlockSpec((M, K), lambda b, j: (0, 0)),
                  pl.BlockSpec((1, K, tn), lambda b, j: (b, 0, j))],
        out_specs=pl.BlockSpec((1, M, tn), lambda b, j: (b, 0, j)),
        compiler_params=_cparams("arbitrary", "arbitrary"),
        name=name,
    )(f, x)


def _twiddle(re, im, tr, ti, conj):
    if conj:
        return re * tr + im * ti, im * tr - re * ti
    return re * tr - im * ti, re * ti + im * tr


def _inner_spectrum_kernel(kb, a_ref, tr_ref, ti_ref, f_ref, o_ref):
    reps = a_ref.shape[-1] // LANES
    for j in range(kb):
        tr = jnp.concatenate([tr_ref[j]] * reps, axis=-1)
        ti = jnp.concatenate([ti_ref[j]] * reps, axis=-1)
        re, im = _twiddle(a_ref[0, 0, j], a_ref[0, 1, j], tr, ti, False)
        n2 = re.shape[0]
        x = jnp.dot(f_ref[...], jnp.concatenate([re, im], axis=0), preferred_element_type=F32,
                    precision=HIGHEST)
        o_ref[0, 0, j] = x[:n2]
        o_ref[0, 1, j] = x[n2:]


def _inner_conv_kernel(kb, a_ref, kf_ref, tr_ref, ti_ref, f_ref, fi_ref, o_ref):
    reps = a_ref.shape[-1] // LANES
    for j in range(kb):
        tr = jnp.concatenate([tr_ref[j]] * reps, axis=-1)
        ti = jnp.concatenate([ti_ref[j]] * reps, axis=-1)
        re, im = _twiddle(a_ref[0, 0, j].astype(F32), a_ref[0, 1, j].astype(F32), tr, ti, False)
        n2 = re.shape[0]
        x = jnp.dot(f_ref[...], jnp.concatenate([re, im], axis=0).astype(BF16), preferred_element_type=F32)
        xr, xi = x[:n2], x[n2:]
        kr, ki = kf_ref[0, 0, j], kf_ref[0, 1, j]
        yr = xr * kr - xi * ki
        yi = xr * ki + xi * kr
        y = jnp.dot(fi_ref[...], jnp.concatenate([yr, yi], axis=0).astype(BF16), preferred_element_type=F32)
        re, im = _twiddle(y[:n2], y[n2:], tr, ti, True)
        o_ref[0, 0, j] = re.astype(o_ref.dtype)
        o_ref[0, 1, j] = im.astype(o_ref.dtype)


def _inner_stage(a5, kf5, tr, ti, f2, f2i, kb):
    B, _, N1, N2, W = a5.shape
    blk = lambda: pl.BlockSpec((1, 2, kb, N2, W), lambda g, b: (b, 0, g, 0, 0))
    tw = pl.BlockSpec((kb, N2, LANES), lambda g, b: (g, 0, 0))
    mat = pl.BlockSpec((2 * N2, 2 * N2), lambda g, b: (0, 0))
    if kf5 is None:
        kern = functools.partial(_inner_spectrum_kernel, kb)
        in_specs, args, out_dtype, name = [blk(), tw, tw, mat], (a5, tr, ti, f2), F32, "hyena_filter_spectrum"
    else:
        kern = functools.partial(_inner_conv_kernel, kb)
        kf_spec = pl.BlockSpec((1, 2, kb, N2, W), lambda g, b: (0, 0, g, 0, 0))
        in_specs, args, out_dtype, name = ([blk(), kf_spec, tw, tw, mat, mat], (a5, kf5, tr, ti, f2, f2i),
                                           BF16, "hyena_inner_conv")
    return pl.pallas_call(
        kern,
        out_shape=jax.ShapeDtypeStruct(a5.shape, out_dtype),
        grid=(N1 // kb, B),
        in_specs=in_specs,
        out_specs=blk(),
        compiler_params=_cparams("arbitrary", "arbitrary"),
        name=name,
    )(*args)


def _hyena_out_kernel(x0_ref, w0_ref, b0_ref, y_ref, u_ref, hb_ref, o_ref):
    x0 = _conv3(x0_ref[0].astype(F32), w0_ref, b0_ref)
    o_ref[0] = (x0 * (y_ref[0] + u_ref[0].astype(F32) * hb_ref[...])).astype(o_ref.dtype)


def _hyena_out(hy, conv_w, conv_b, y, u, hbias):
    B, L, W = u.shape
    nw = W // LANES
    blk = pl.BlockSpec((1, L, LANES), lambda b, j: (b, 0, j))
    vec = lambda r: pl.BlockSpec((r, LANES), lambda b, j: (0, j))
    return pl.pallas_call(
        _hyena_out_kernel,
        out_shape=jax.ShapeDtypeStruct((B, L, W), BF16),
        grid=(B, nw),
        in_specs=[blk, vec(3), vec(1), blk, blk, vec(1)],
        out_specs=blk,
        compiler_params=_cparams("arbitrary", "arbitrary"),
        name="hyena_out",
    )(hy, conv_w, conv_b, y, u, hbias)


def _dft_constants(L):
    N = 2 * L
    N2 = FFT_N2
    N1 = N // N2
    k1 = np.arange(N1, dtype=np.float64)
    th1 = 2.0 * np.pi * np.outer(k1, k1) / N1
    fwd = np.concatenate([np.cos(th1), -np.sin(th1)], axis=0)
    inv = np.concatenate([np.cos(th1), -np.sin(th1)], axis=1)[: N1 // 2] / N
    n2 = np.arange(N2, dtype=np.float64)
    tw = 2.0 * np.pi * np.outer(k1, n2) / N
    tr = np.repeat(np.cos(tw)[:, :, None], LANES, axis=2)
    ti = np.repeat(-np.sin(tw)[:, :, None], LANES, axis=2)
    th2 = 2.0 * np.pi * np.outer(n2, n2) / N2
    c2, s2 = np.cos(th2), np.sin(th2)
    f2 = np.block([[c2, s2], [-s2, c2]])
    f2i = np.block([[c2, -s2], [s2, c2]])
    f = lambda a: jnp.asarray(a, F32)
    return dict(N1=N1, N2=N2, fwd=f(fwd), inv=f(inv), tr=f(tr), ti=f(ti), f2=f(f2), f2i=f(f2i))


def _filter_features(L, emb_dim):
    bands = (emb_dim - 1) // 2
    t = np.linspace(0.0, 1.0, L, dtype=np.float32)[:, None]
    w = (np.float32(2.0 * math.pi) * np.arange(L, dtype=np.float32)[:, None] / np.float32(L)).astype(np.float32)
    f = np.linspace(1e-4, bands - 1, bands, dtype=np.float32)
    fw = (f * w).astype(np.float32)
    z = np.concatenate([t, np.cos(fw.astype(np.float64)).astype(np.float32),
                        -np.sin(fw.astype(np.float64)).astype(np.float32)], axis=-1)
    return jnp.asarray(z), t


def _decay_window(t, W):
    deltas = np.abs(np.linspace(math.log(HY_DECAY_TARGET) / HY_SLOW_DECAY,
                                math.log(HY_DECAY_TARGET) / HY_FAST_DECAY, W, dtype=np.float32))
    dec = np.exp((-t * deltas).astype(np.float32).astype(np.float64)).astype(np.float32)
    return jnp.asarray(np.concatenate([dec, dec], axis=1))


def _hyena(hy, conv_w, conv_b, fw1, fb1, fw2, fb2, fw3, fb3, fw4, ffreq, hbias):
    B, L, W3 = hy.shape
    W = W3 // 3
    C = _dft_constants(L)
    N1, N2 = C["N1"], C["N2"]
    row = lambda a: a.reshape(1, -1)
    z, t = _filter_features(L, fw1.shape[0])
    kpad = LANES - fw1.shape[0]
    z = jnp.pad(z, ((0, 0), (0, kpad)))
    taps = _implicit_filter(z, jnp.pad(fw1, ((0, kpad), (0, 0))), row(fb1), fw2, row(fb2), fw3, row(fb3), fw4,
                            row(ffreq),
                            _decay_window(t, W), tl=min(L, 512))
    k_fwd, k_bwd = taps[:, :W], taps[:, W:]
    k_full = jnp.concatenate([k_fwd, jnp.zeros((1, W), F32), k_bwd[:0:-1]], axis=0)
    lanes = N2 * W
    tn = min(lanes, 8192)
    kf = _outer_dft(C["fwd"], k_full.reshape(1, N1, lanes), F32, tn, "hyena_filter_outer")
    kf5 = _inner_stage(kf.reshape(1, 2, N1, N2, W), None, C["tr"], C["ti"], C["f2"], None, kb=min(N1, 8))
    u = _hyena_gate(hy, conv_w, row(conv_b))
    a = _outer_dft(C["fwd"][:, : N1 // 2].astype(BF16), u.reshape(B, N1 // 2, lanes), BF16, tn,
                   "hyena_outer_fwd")
    bb = _inner_stage(a.reshape(B, 2, N1, N2, W), kf5, C["tr"], C["ti"], C["f2"].astype(BF16),
                      C["f2i"].astype(BF16), kb=min(N1, 8))
    y = _outer_dft(C["inv"].astype(BF16), bb.reshape(B, 2 * N1, lanes), F32, tn, "hyena_outer_inv")
    return _hyena_out(hy, conv_w, row(conv_b), y.reshape(B, L, W), u, row(hbias))


ROUTE_ROWS = 16


def _route_tile(h, rw_ref, rb_ref, tri_ref, carry_ref):
    tm = h.shape[0]
    E = rw_ref.shape[0] // 2
    nt = (((1,), (1,)), ((), ()))
    h_hi = h.astype(BF16)
    h_lo = (h - h_hi.astype(F32)).astype(BF16)
    a = lax.dot_general(rw_ref[...], h_hi, nt, preferred_element_type=F32)
    b = lax.dot_general(rw_ref[0:E, :], h_lo, nt, preferred_element_type=F32)
    reps = tm // LANES
    logits = a[:E] + a[E:] + b + jnp.concatenate([rb_ref[...]] * reps, axis=1)
    row = lax.broadcasted_iota(jnp.int32, (E, tm), 0)
    v = logits
    ids, vals = [], []
    for _ in range(TOP_K):
        m = jnp.max(v, axis=0, keepdims=True)
        idx = jnp.min(jnp.where(v == m, row, E), axis=0, keepdims=True)
        ids.append(idx)
        vals.append(m)
        v = jnp.where(row == idx, -jnp.inf, v)
    ex = [jnp.exp(vk - vals[0]) for vk in vals]
    den = ex[0]
    for t in ex[1:]:
        den = den + t
    hot = [row == idx for idx in ids]
    member = jnp.zeros((E, tm), F32)
    for hk in hot:
        member = jnp.where(hk, 1.0, member)
    member = member.astype(BF16)
    carry = carry_ref[...]
    before = (jnp.dot(member, tri_ref[...], preferred_element_type=F32)
              + jnp.concatenate([carry] * reps, axis=1))
    ranks = [jnp.sum(jnp.where(hk, before, 0.0), axis=0, keepdims=True) for hk in hot]
    carry_ref[...] = carry + jnp.dot(member, jnp.ones((tm, LANES), BF16), preferred_element_type=F32)
    out_row = lax.broadcasted_iota(jnp.int32, (ROUTE_ROWS, tm), 0)
    rec = jnp.zeros((ROUTE_ROWS, tm), F32)
    for k in range(TOP_K):
        rec = jnp.where(out_row == k, ids[k].astype(F32), rec)
        rec = jnp.where(out_row == TOP_K + k, ex[k] / den, rec)
        rec = jnp.where(out_row == 2 * TOP_K + k, ranks[k], rec)
    return rec


def _merge_kernel(x_ref, oa_ref, ob_ref, gt_ref, g1_ref, sc_ref, sh_ref, ng_ref, wa_ref, wb_ref, wo_ref,
                  rw_ref, rb_ref, tri_ref, xo_ref, h_ref, rec_ref, cnt_ref, carry_ref):
    @pl.when((pl.program_id(0) == 0) & (pl.program_id(1) == 0))
    def _():
        carry_ref[...] = jnp.zeros_like(carry_ref)

    D = x_ref.shape[-1]
    gt = gt_ref[0]
    m = (gt[:, :D].astype(F32) * jnp.dot(oa_ref[0], wa_ref[...], preferred_element_type=F32)
         + gt[:, D:].astype(F32) * jnp.dot(ob_ref[0], wb_ref[...], preferred_element_type=F32))
    y = jnp.dot(m.astype(BF16), wo_ref[...], preferred_element_type=F32)
    xn = x_ref[0] + g1_ref[0] * y
    xo_ref[0] = xn
    h = _rms(xn, D) * ng_ref[...]
    h = h * (1.0 + sc_ref[0]) + sh_ref[0]
    h_ref[0] = h.astype(h_ref.dtype)
    rec_ref[0] = _route_tile(h, rw_ref, rb_ref, tri_ref, carry_ref)
    cnt_ref[...] = carry_ref[...]


def _merge(x, oa, ob, gates, g1, sc2, sh2, ng, wa, wb, wo, rw, rb, tm):
    B, S, D = x.shape
    E = rw.shape[1]
    rw_hi = rw.T.astype(BF16)
    rw2 = jnp.concatenate([rw_hi, (rw.T - rw_hi.astype(F32)).astype(BF16)], axis=0)
    rb2 = jnp.broadcast_to(rb.reshape(E, 1), (E, LANES))
    tri = jnp.asarray(np.triu(np.ones((tm, tm), np.float32), 1), BF16)
    act = lambda w: pl.BlockSpec((1, tm, w), lambda b, i: (b, i, 0))
    mod = pl.BlockSpec((1, 1, D), lambda b, i: (b, 0, 0))
    full = lambda a: pl.BlockSpec(a.shape, lambda b, i: (0, 0))
    return pl.pallas_call(
        _merge_kernel,
        out_shape=[jax.ShapeDtypeStruct((B, S, D), F32), jax.ShapeDtypeStruct((B, S, D), BF16),
                   jax.ShapeDtypeStruct((B, ROUTE_ROWS, S), F32), jax.ShapeDtypeStruct((E, LANES), F32)],
        grid=(B, S // tm),
        in_specs=[act(D), act(oa.shape[-1]), act(ob.shape[-1]), act(2 * D), mod, mod, mod, full(ng),
                  full(wa), full(wb), full(wo), full(rw2), full(rb2), full(tri)],
        out_specs=[act(D), act(D), pl.BlockSpec((1, ROUTE_ROWS, tm), lambda b, i: (b, 0, i)),
                   pl.BlockSpec((E, LANES), lambda b, i: (0, 0))],
        scratch_shapes=[pltpu.VMEM((E, LANES), F32)],
        compiler_params=_cparams("arbitrary", "arbitrary"),
        name="merge_norm_router",
    )(x, oa, ob, gates, g1, sc2, sh2, ng, wa, wb, wo, rw2, rb2, tri)


def _moe_kernel(be_ref, nu_ref, x_ref, w1_ref, b1_ref, w2_ref, b2_ref, o_ref, w1b_ref, w2b_ref):
    i = pl.program_id(0)

    @pl.when(i < nu_ref[0])
    def _():
        @pl.when((i == 0) | (be_ref[i] != be_ref[jnp.maximum(i - 1, 0)]))
        def _():
            w1b_ref[...] = w1_ref[0].astype(BF16)
            w2b_ref[...] = w2_ref[0].astype(BF16)

        F = w2_ref.shape[1]
        gu = jnp.dot(x_ref[...], w1b_ref[...], preferred_element_type=F32) + b1_ref[0]
        gate = jnp.minimum(gu[:, :F], SWIGLU_LIMIT)
        up = jnp.clip(gu[:, F:], -SWIGLU_LIMIT, SWIGLU_LIMIT)
        act = (up + 1.0) * gate * _sigmoid(SWIGLU_ALPHA * gate)
        y = jnp.dot(act.astype(BF16), w2b_ref[...], preferred_element_type=F32) + b2_ref[0]
        o_ref[...] = y.astype(o_ref.dtype)


def _moe_experts(block_e, n_used, xg, w1, b1, w2, b2, bm):
    R, D = xg.shape
    E, _, F2 = w1.shape
    F = w2.shape[1]
    grid_spec = pltpu.PrefetchScalarGridSpec(
        num_scalar_prefetch=2,
        grid=(R // bm,),
        in_specs=[pl.BlockSpec((bm, D), lambda i, be, nu: (i, 0)),
                  pl.BlockSpec((1, D, F2), lambda i, be, nu: (be[i], 0, 0)),
                  pl.BlockSpec((1, 1, F2), lambda i, be, nu: (be[i], 0, 0)),
                  pl.BlockSpec((1, F, D), lambda i, be, nu: (be[i], 0, 0)),
                  pl.BlockSpec((1, 1, D), lambda i, be, nu: (be[i], 0, 0))],
        out_specs=pl.BlockSpec((bm, D), lambda i, be, nu: (i, 0)),
        scratch_shapes=[pltpu.VMEM((D, F2), BF16), pltpu.VMEM((F, D), BF16)],
    )
    return pl.pallas_call(
        _moe_kernel,
        out_shape=jax.ShapeDtypeStruct((R, D), BF16),
        grid_spec=grid_spec,
        compiler_params=pltpu.CompilerParams(dimension_semantics=("arbitrary",),
                                             vmem_limit_bytes=V7X_VMEM_LIMIT_MOE_BYTES),
        name="moe_experts",
    )(block_e, n_used, xg, w1, b1.reshape(E, 1, F2), w2, b2.reshape(E, 1, D))


def _combine_kernel(x_ref, y_ref, w_ref, g_ref, o_ref):
    w = w_ref[0]
    acc = w[:, 0:1] * y_ref[0, 0].astype(F32)
    for k in range(1, y_ref.shape[0]):
        acc = acc + w[:, k:k + 1] * y_ref[k, 0].astype(F32)
    o_ref[0] = x_ref[0] + g_ref[0] * acc


def _combine(x, yg, wts, g2, tm):
    B, S, D = x.shape
    K = yg.shape[0]
    return pl.pallas_call(
        _combine_kernel,
        out_shape=jax.ShapeDtypeStruct((B, S, D), F32),
        grid=(B, S // tm),
        in_specs=[pl.BlockSpec((1, tm, D), lambda b, i: (b, i, 0)),
                  pl.BlockSpec((K, 1, tm, D), lambda b, i: (0, b, i, 0)),
                  pl.BlockSpec((1, tm, K), lambda b, i: (b, i, 0)),
                  pl.BlockSpec((1, 1, D), lambda b, i: (b, 0, 0))],
        out_specs=pl.BlockSpec((1, tm, D), lambda b, i: (b, i, 0)),
        compiler_params=_cparams("arbitrary", "arbitrary"),
        name="moe_combine",
    )(x, yg, wts, g2)


def _route(rec, counts, bm):
    B, _, S = rec.shape
    T = B * S
    E = counts.shape[0]
    I32 = jnp.int32
    rec = jnp.swapaxes(rec, 1, 2).reshape(T, ROUTE_ROWS)
    top_i = rec[:, 0:TOP_K].astype(I32)
    wts = rec[:, TOP_K:2 * TOP_K]
    rank = rec[:, 2 * TOP_K:3 * TOP_K].astype(I32)
    counts = counts.astype(I32)
    A = T * TOP_K
    padded = (counts + bm - 1) // bm * bm
    pend = jnp.cumsum(padded)
    pstart = pend - padded
    start = jnp.cumsum(counts) - counts
    experts = jnp.arange(E, dtype=I32)
    pick = lambda table, idx: jnp.sum(jnp.where(idx[..., None] == experts, table, 0), axis=-1)
    pos = pick(pstart, top_i) + rank
    keys = top_i.reshape(-1) * A + jnp.arange(A, dtype=I32)
    order = jnp.sort(keys) % A
    n_rows = A + E * bm
    n_blocks = n_rows // bm
    blk_start = jnp.arange(n_blocks, dtype=I32) * bm
    block_e = jnp.minimum(jnp.sum((pend[None, :] <= blk_start[:, None]).astype(I32), axis=1), E - 1)
    off = blk_start[:, None] + jnp.arange(bm, dtype=I32)[None, :] - pick(pstart, block_e)[:, None]
    valid = (off >= 0) & (off < pick(counts, block_e)[:, None])
    src = jnp.clip(pick(start, block_e)[:, None] + off, 0, A - 1)
    row_tok = jnp.where(valid, jnp.take(order, src.reshape(-1)).reshape(n_blocks, bm) // TOP_K, 0)
    n_used = (pend[-1:] // bm).astype(I32)
    return wts, row_tok.reshape(-1), pos, block_e.astype(I32), n_used


def kernel(x, c, ctx, c_ctx, ada_w, ada_b, norm1_g, norm2_g, w_in, b_in, q_norm_g, k_norm_g, lambda_q1, lambda_k1, lambda_q2, lambda_k2, subln_g, conv_w, conv_b, filt_w1, filt_b1, filt_w2, filt_b2, filt_w3, filt_b3, filt_w4, filt_freq, hyena_bias, w_up_a, w_up_b, w_out, router_w, router_b, exp_w1, exp_b1, exp_w2, exp_b2):
    B, S, D = x.shape
    depth = ada_w.shape[0]
    assert depth == 1, "context-stream update between layers is not implemented"
    l = 0
    QC = DA_HEADS * 2 * DA_QK_DIM
    W = D // 2
    row = lambda a: a.reshape(1, -1)

    lambda_init = 0.8 - 0.6 * math.exp(-0.3 * l)
    lam = (jnp.exp(jnp.sum(lambda_q1[l] * lambda_k1[l])) - jnp.exp(jnp.sum(lambda_q2[l] * lambda_k2[l]))
           + lambda_init).reshape(1).astype(F32)

    R = -(-(B + 1) // 8) * 8
    cc = jnp.concatenate([c, c_ctx[None, :], jnp.zeros((R - B - 1, D), F32)], axis=0)
    mods = _ada(cc, ada_w[l], row(ada_b[l]))
    sh1, sc1, g1, sh2, sc2, g2 = [mods[:B, i * D:(i + 1) * D].reshape(B, 1, D) for i in range(6)]
    sh1c, sc1c = [jnp.broadcast_to(mods[B, i * D:(i + 1) * D].reshape(1, 1, D), (B, 1, D)) for i in range(2)]

    w_in_b = w_in[l].astype(BF16)
    qg = row(jnp.tile(q_norm_g[l], QC // DA_QK_DIM))
    kg = row(jnp.tile(k_norm_g[l], QC // DA_QK_DIM))
    grp = np.arange(QC) // DA_QK_DIM
    bd = jnp.asarray(grp[:, None] == grp[None, :], BF16)
    qscale = DA_QK_DIM ** -0.5 * math.log2(math.e)
    spec = (("q", 0, QC), ("k", QC, QC), ("transposed", 2 * QC, QC), ("plain", 3 * QC, 3 * W),
            ("sigmoid", 3 * QC + 3 * W, 2 * D))
    qx, kx, vtx, hy, gates = _in_proj(x, sc1, sh1, row(norm1_g[l]), w_in_b, row(b_in[l]), qg, kg, bd,
                                      _rope_tables(S, QC), spec, tm=min(S, 256), qscale=qscale)
    spec_c = (("kc", 0, QC), ("transposed", QC, QC))
    kc, vtc = _in_proj(ctx, sc1c, sh1c, row(norm1_g[l]), w_in_b[:, QC:3 * QC], row(b_in[l][QC:3 * QC]), qg,
                       kg, bd, None, spec_c, tm=ctx.shape[1], qscale=qscale)

    k_all = jnp.concatenate([kc, kx], axis=1)
    Sk = k_all.shape[1]
    vt_all = jnp.concatenate([vtc, vtx], axis=2).reshape(B, DA_HEADS, DA_V_DIM, Sk)
    vt_all = jnp.concatenate([vt_all, jnp.ones((B, DA_HEADS, 16, Sk), BF16)], axis=2)
    oa = _attention(lam, qx, k_all, vt_all, row(subln_g[l] * (1.0 - lambda_init)), tq=min(S, 256),
                    nq=2 if S >= 512 else 1)
    ob = _hyena(hy, conv_w[l], conv_b[l], filt_w1[l], filt_b1[l], filt_w2[l], filt_b2[l], filt_w3[l],
                filt_b3[l], filt_w4[l], filt_freq[l], hyena_bias[l])
    x_new, h2, rec, counts = _merge(x, oa, ob, gates, g1, sc2, sh2, row(norm2_g[l]), w_up_a[l].astype(BF16),
                                    w_up_b[l].astype(BF16), w_out[l].astype(BF16), router_w[l],
                                    row(router_b[l]), tm=min(S, 512))

    T = B * S
    bm = EXPERT_BLOCK
    wts, row_tok, pos, block_e, n_used = _route(rec, counts[:, 0], bm)
    xg = jnp.take(h2.reshape(T, D), row_tok, axis=0)
    yb = _moe_experts(block_e, n_used, xg, exp_w1[l], exp_b1[l], exp_w2[l], exp_b2[l], bm)
    yg = jnp.take(yb, pos.T, axis=0).reshape(TOP_K, B, S, D)
    return _combine(x_new, yg, wts.reshape(B, S, TOP_K), g2, tm=min(S, 512))
```

```python
import functools
import math

import jax
import jax.numpy as jnp
import numpy as np
from jax import lax
from jax.experimental import pallas as pl
from jax.experimental.pallas import tpu as pltpu

F32 = jnp.float32
BF16 = jnp.bfloat16
HIGHEST = lax.Precision.HIGHEST

GRID_W = 64
DA_HEADS = 4
DA_QK_DIM = 64
DA_V_DIM = 2 * DA_QK_DIM
ROPE_THETA = 10000.0
ROPE_FREQS = DA_QK_DIM // 4
HY_DECAY_TARGET = 1e-2
HY_FAST_DECAY = 0.3
HY_SLOW_DECAY = 1.5
TOP_K = 4
SWIGLU_LIMIT = 7.0
SWIGLU_ALPHA = 1.702
EPS = 1e-6

LANES = 128
V7X_VMEM_LIMIT_BYTES = 48 * 1024 * 1024
V7X_VMEM_LIMIT_MOE_BYTES = 56 * 1024 * 1024
FFT_N2 = 128
EXPERT_BLOCK = 512


def _cparams(*sem):
    return pltpu.CompilerParams(dimension_semantics=sem, vmem_limit_bytes=V7X_VMEM_LIMIT_BYTES)


def _sigmoid(x):
    return 1.0 / (1.0 + jnp.exp(-x))


def _rms(x, width):
    return x * lax.rsqrt(jnp.sum(x * x, axis=-1, keepdims=True) * (1.0 / width) + EPS)


def _ada_kernel(c_ref, w_ref, b_ref, o_ref):
    c = c_ref[...]
    s = c * _sigmoid(c)
    o_ref[...] = jnp.dot(s, w_ref[...], preferred_element_type=F32, precision=HIGHEST) + b_ref[...]


def _ada(cc, w, b):
    R, D = cc.shape
    N = w.shape[1]
    tn = D
    return pl.pallas_call(
        _ada_kernel,
        out_shape=jax.ShapeDtypeStruct((R, N), F32),
        grid=(N // tn,),
        in_specs=[pl.BlockSpec((R, D), lambda j: (0, 0)),
                  pl.BlockSpec((D, tn), lambda j: (0, j)),
                  pl.BlockSpec((1, tn), lambda j: (0, j))],
        out_specs=pl.BlockSpec((R, tn), lambda j: (0, j)),
        compiler_params=_cparams("arbitrary"),
        name="ada",
    )(cc, w, b)


_TRANSPOSED_KINDS = ("q", "transposed")


def _in_kernel(spec, use_rope, qscale, x_ref, sc_ref, sh_ref, g_ref, w_ref, b_ref, qg_ref, kg_ref,
               bd_ref, *rest):
    if use_rope:
        cos_ref, sin_ref, *outs = rest
    else:
        outs = rest
    x = x_ref[0]
    h = _rms(x, x.shape[-1]) * g_ref[...]
    h = (h * (1.0 + sc_ref[0]) + sh_ref[0]).astype(BF16)
    for (kind, c0, cw), o_ref in zip(spec, outs):
        y = jnp.dot(h, w_ref[:, c0:c0 + cw], preferred_element_type=F32) + b_ref[:, c0:c0 + cw]
        if kind in ("q", "k", "kc"):
            y2 = y * y
            hi = y2.astype(BF16)
            lo = (y2 - hi.astype(F32)).astype(BF16)
            ssq = (jnp.dot(hi, bd_ref[...], preferred_element_type=F32)
                   + jnp.dot(lo, bd_ref[...], preferred_element_type=F32))
            gn = qg_ref if kind == "q" else kg_ref
            y = y * lax.rsqrt(ssq * (1.0 / DA_QK_DIM) + EPS) * gn[...]
            if kind != "kc":
                lane = lax.broadcasted_iota(jnp.int32, y.shape, 1)
                partner = jnp.where((lane % (2 * ROPE_FREQS)) < ROPE_FREQS,
                                    pltpu.roll(y, cw - ROPE_FREQS, 1), pltpu.roll(y, ROPE_FREQS, 1))
                y = y * cos_ref[...] + partner * sin_ref[...]
            if kind == "q":
                y = y * qscale
        elif kind == "sigmoid":
            y = _sigmoid(y)
        if kind in _TRANSPOSED_KINDS:
            y = y.T
        o_ref[0] = y.astype(o_ref.dtype)


def _in_proj(x, sc, sh, g, w, b, qg, kg, bd, rope, spec, tm, qscale):
    B, S, D = x.shape
    N = w.shape[1]
    use_rope = rope is not None
    const2 = lambda i, bb: (0, 0)
    in_specs = [pl.BlockSpec((1, tm, D), lambda i, bb: (bb, i, 0)),
                pl.BlockSpec((1, 1, D), lambda i, bb: (bb, 0, 0)),
                pl.BlockSpec((1, 1, D), lambda i, bb: (bb, 0, 0)),
                pl.BlockSpec((1, D), const2),
                pl.BlockSpec((D, N), const2, pipeline_mode=pl.Buffered(1)),
                pl.BlockSpec((1, N), const2),
                pl.BlockSpec(qg.shape, const2),
                pl.BlockSpec(kg.shape, const2),
                pl.BlockSpec(bd.shape, const2)]
    args = [x, sc, sh, g, w, b, qg, kg, bd]
    if use_rope:
        cw = rope[0].shape[1]
        in_specs += [pl.BlockSpec((tm, cw), lambda i, bb: (i, 0))] * 2
        args += list(rope)
    out_shape = [jax.ShapeDtypeStruct((B, cw_, S) if kind in _TRANSPOSED_KINDS else (B, S, cw_), BF16)
                 for (kind, _, cw_) in spec]
    out_specs = [pl.BlockSpec((1, cw_, tm), lambda i, bb: (bb, 0, i)) if kind in _TRANSPOSED_KINDS
                 else pl.BlockSpec((1, tm, cw_), lambda i, bb: (bb, i, 0)) for (kind, _, cw_) in spec]
    return pl.pallas_call(
        functools.partial(_in_kernel, spec, use_rope, qscale),
        out_shape=out_shape,
        grid=(S // tm, B),
        in_specs=in_specs,
        out_specs=out_specs,
        compiler_params=_cparams("arbitrary", "arbitrary"),
        name="in_proj_rope" if use_rope else "in_proj_ctx",
    )(*args)


def _rope_tables(S, width):
    t = np.arange(S)
    pos = np.stack([t // GRID_W, t % GRID_W], axis=-1).astype(np.float32)
    freqs = (np.float32(ROPE_THETA) ** (-np.arange(ROPE_FREQS, dtype=np.float32) / ROPE_FREQS)).astype(np.float32)
    lane = np.arange(width)
    d = lane % DA_QK_DIM
    axis = d // (2 * ROPE_FREQS)
    half = (d % (2 * ROPE_FREQS)) // ROPE_FREQS
    f = d % ROPE_FREQS
    ang = (pos[:, axis] * freqs[f][None, :]).astype(np.float32)
    cos = np.cos(ang.astype(np.float64)).astype(np.float32)
    sin = np.sin(ang.astype(np.float64)).astype(np.float32)
    sin = np.where(half[None, :] == 0, -sin, sin)
    return jnp.asarray(cos), jnp.asarray(sin)


def _attn_kernel(tq, n_score, kc, lam_ref, q_ref, qn_ref, k_ref, vt_ref, g_ref, o_ref, st0, st1, mx0, mx1):
    Sk = k_ref.shape[1]
    dv = vt_ref.shape[2] - 16
    rows_a = Sk // n_score
    n_out = Sk // kc
    lam = lam_ref[0]

    def stacked(qt):
        row = lax.broadcasted_iota(jnp.int32, qt.shape, 0)
        zero = jnp.zeros_like(qt)
        return jnp.concatenate([jnp.where(row < DA_QK_DIM, qt, zero), jnp.where(row >= DA_QK_DIM, qt, zero)],
                               axis=1)

    def score_piece(qs, st, mx, j):
        r0 = j * rows_a
        st[r0:r0 + rows_a, :] = jnp.dot(k_ref[0, r0:r0 + rows_a, :], qs, preferred_element_type=F32)
        part = jnp.max(st[r0:r0 + rows_a, :].reshape(rows_a // 8, 8, st.shape[-1]), axis=0)
        mx[...] = part if j == 0 else jnp.maximum(mx[...], part)

    def out_piece(st, m, acc, j):
        e = jnp.exp2(st[j * kc:(j + 1) * kc, :] - m).astype(BF16)
        d = jnp.dot(vt_ref[0, 0, :, j * kc:(j + 1) * kc], e, preferred_element_type=F32)
        return d if acc is None else acc + d

    def finish(acc, c):
        ot = acc[:dv] * (1.0 / acc[dv:dv + 1])
        o = (ot[:, :tq] - lam * ot[:, tq:]).T
        o = _rms(o, o.shape[-1]) * g_ref[...]
        o_ref[0, c * tq:(c + 1) * tq, :] = o.astype(o_ref.dtype)

    def overlapped(st_cur, mx_cur, q_next, st_next, mx_next, c):
        m = jnp.max(mx_cur[...], axis=0, keepdims=True)
        qs = stacked(q_next)
        acc = None
        p = 0
        for j in range(n_out):
            while p < n_score and p * n_out <= j * n_score:
                score_piece(qs, st_next, mx_next, p)
                p += 1
            acc = out_piece(st_cur, m, acc, j)
        for p in range(p, n_score):
            score_piece(qs, st_next, mx_next, p)
        finish(acc, c)

    @pl.when(pl.program_id(2) == 0)
    def _():
        qs = stacked(q_ref[0, :, 0:tq])
        for j in range(n_score):
            score_piece(qs, st0, mx0, j)

    overlapped(st0, mx0, q_ref[0, :, tq:2 * tq], st1, mx1, 0)
    overlapped(st1, mx1, qn_ref[0], st0, mx0, 1)


def _attention(lam, qt, k, vt, g, tq):
    B, W, S = qt.shape
    Sk = k.shape[1]
    H = W // LANES
    tb = 2 * tq
    n_steps = S // tb
    kc = 256 if Sk % 256 == 0 else LANES
    n_score = Sk // kc
    return pl.pallas_call(
        functools.partial(_attn_kernel, tq, n_score, kc),
        out_shape=jax.ShapeDtypeStruct((B, S, W), BF16),
        grid=(B, H, n_steps),
        in_specs=[pl.BlockSpec(memory_space=pltpu.SMEM),
                  pl.BlockSpec((1, LANES, tb), lambda b, h, i: (b, h, i)),
                  pl.BlockSpec((1, LANES, tq), lambda b, h, i: (b, h, jnp.minimum(i + 1, n_steps - 1) * 2)),
                  pl.BlockSpec((1, Sk, LANES), lambda b, h, i: (b, 0, h)),
                  pl.BlockSpec((1, 1, vt.shape[2], Sk), lambda b, h, i: (b, h, 0, 0)),
                  pl.BlockSpec((1, LANES), lambda b, h, i: (0, 0))],
        out_specs=pl.BlockSpec((1, tb, LANES), lambda b, h, i: (b, i, h)),
        scratch_shapes=[pltpu.VMEM((Sk, 2 * tq), F32), pltpu.VMEM((Sk, 2 * tq), F32),
                        pltpu.VMEM((8, 2 * tq), F32), pltpu.VMEM((8, 2 * tq), F32)],
        compiler_params=_cparams("arbitrary", "arbitrary", "arbitrary"),
        name="diff_attn",
    )(lam, qt, qt, k, vt, g)


def _filter_kernel(z_ref, w1, b1, w2, b2, w3, b3, w4, fr, dec_ref, o_ref):
    dot = functools.partial(jnp.dot, preferred_element_type=F32, precision=HIGHEST)
    f = fr[...]
    h = jnp.sin(f * (dot(z_ref[...], w1[...]) + b1[...]))
    h = jnp.sin(f * (dot(h, w2[...]) + b2[...]))
    h = jnp.sin(f * (dot(h, w3[...]) + b3[...]))
    o_ref[...] = dot(h, w4[...]) * dec_ref[...]


def _implicit_filter(z, w1, b1, w2, b2, w3, b3, w4, fr, dec, tl):
    L = z.shape[0]
    N = w4.shape[1]
    full = lambda a: pl.BlockSpec(a.shape, lambda i: (0, 0))
    return pl.pallas_call(
        _filter_kernel,
        out_shape=jax.ShapeDtypeStruct((L, N), F32),
        grid=(L // tl,),
        in_specs=[pl.BlockSpec((tl, z.shape[1]), lambda i: (i, 0)),
                  full(w1), full(b1), full(w2), full(b2), full(w3), full(b3), full(w4), full(fr),
                  pl.BlockSpec((tl, N), lambda i: (i, 0))],
        out_specs=pl.BlockSpec((tl, N), lambda i: (i, 0)),
        compiler_params=_cparams("arbitrary"),
        name="hyena_filter",
    )(z, w1, b1, w2, b2, w3, b3, w4, fr, dec)


def _conv3(a, w_ref, b_ref):
    L = a.shape[0]
    row = lax.broadcasted_iota(jnp.int32, a.shape, 0)
    prev = jnp.where(row == 0, 0.0, pltpu.roll(a, 1, 0))
    nxt = jnp.where(row == L - 1, 0.0, pltpu.roll(a, L - 1, 0))
    return prev * w_ref[0:1, :] + a * w_ref[1:2, :] + nxt * w_ref[2:3, :] + b_ref[...]


def _gate_kernel(x1_ref, v_ref, w1_ref, wv_ref, b1_ref, bv_ref, u_ref):
    u = _conv3(x1_ref[0].astype(F32), w1_ref, b1_ref) * _conv3(v_ref[0].astype(F32), wv_ref, bv_ref)
    u_ref[0] = u.astype(u_ref.dtype)


def _hyena_gate(hy, conv_w, conv_b):
    B, L, W3 = hy.shape
    W = W3 // 3
    nw = W // LANES
    act = lambda off: pl.BlockSpec((1, L, LANES), lambda b, j: (b, 0, j + off))
    cw = lambda off: pl.BlockSpec((3, LANES), lambda b, j: (0, j + off))
    cb = lambda off: pl.BlockSpec((1, LANES), lambda b, j: (0, j + off))
    return pl.pallas_call(
        _gate_kernel,
        out_shape=jax.ShapeDtypeStruct((B, L, W), BF16),
        grid=(B, nw),
        in_specs=[act(nw), act(2 * nw), cw(nw), cw(2 * nw), cb(nw), cb(2 * nw)],
        out_specs=pl.BlockSpec((1, L, LANES), lambda b, j: (b, 0, j)),
        compiler_params=_cparams("arbitrary", "arbitrary"),
        name="hyena_gate",
    )(hy, hy, conv_w, conv_w, conv_b, conv_b)


def _outer_dft_kernel(precision, f_ref, x_ref, o_ref):
    o_ref[0] = jnp.dot(f_ref[...], x_ref[0], preferred_element_type=F32,
                       precision=precision).astype(o_ref.dtype)


def _outer_dft(f, x, out_dtype, tn, name):
    B, K, N = x.shape
    M = f.shape[0]
    precision = HIGHEST if x.dtype == F32 else None
    return pl.pallas_call(
        functools.partial(_outer_dft_kernel, precision),
        out_shape=jax.ShapeDtypeStruct((B, M, N), out_dtype),
        grid=(B, N // tn),
        in_specs=[pl.BlockSpec((M, K), lambda b, j: (0, 0)),
                  pl.BlockSpec((1, K, tn), lambda b, j: (b, 0, j))],
        out_specs=pl.BlockSpec((1, M, tn), lambda b, j: (b, 0, j)),
        compiler_params=_cparams("arbitrary", "arbitrary"),
        name=name,
    )(f, x)


def _twiddle(re, im, tr, ti, conj):
    if conj:
        return re * tr + im * ti, im * tr - re * ti
    return re * tr - im * ti, re * ti + im * tr


def _inner_spectrum_kernel(kb, a_ref, tr_ref, ti_ref, f_ref, o_ref):
    reps = a_ref.shape[-1] // LANES
    for j in range(kb):
        tr = jnp.concatenate([tr_ref[j]] * reps, axis=-1)
        ti = jnp.concatenate([ti_ref[j]] * reps, axis=-1)
        re, im = _twiddle(a_ref[0, 0, j], a_ref[0, 1, j], tr, ti, False)
        n2 = re.shape[0]
        x = jnp.dot(f_ref[...], jnp.concatenate([re, im], axis=0), preferred_element_type=F32,
                    precision=HIGHEST)
        o_ref[0, 0, j] = x[:n2]
        o_ref[0, 1, j] = x[n2:]


def _inner_conv_kernel(kb, a_ref, kf_ref, tr_ref, ti_ref, f_ref, fi_ref, o_ref):
    reps = a_ref.shape[-1] // LANES
    for j in range(kb):
        tr = jnp.concatenate([tr_ref[j]] * reps, axis=-1)
        ti = jnp.concatenate([ti_ref[j]] * reps, axis=-1)
        re, im = _twiddle(a_ref[0, 0, j].astype(F32), a_ref[0, 1, j].astype(F32), tr, ti, False)
        n2 = re.shape[0]
        x = jnp.dot(f_ref[...], jnp.concatenate([re, im], axis=0).astype(BF16), preferred_element_type=F32)
        xr, xi = x[:n2], x[n2:]
        kr, ki = kf_ref[0, 0, j], kf_ref[0, 1, j]
        yr = xr * kr - xi * ki
        yi = xr * ki + xi * kr
        y = jnp.dot(fi_ref[...], jnp.concatenate([yr, yi], axis=0).astype(BF16), preferred_element_type=F32)
        re, im = _twiddle(y[:n2], y[n2:], tr, ti, True)
        o_ref[0, 0, j] = re.astype(o_ref.dtype)
        o_ref[0, 1, j] = im.astype(o_ref.dtype)


def _inner_stage(a5, kf5, tr, ti, f2, f2i, kb):
    B, _, N1, N2, W = a5.shape
    blk = lambda: pl.BlockSpec((1, 2, kb, N2, W), lambda g, b: (b, 0, g, 0, 0))
    tw = pl.BlockSpec((kb, N2, LANES), lambda g, b: (g, 0, 0))
    mat = pl.BlockSpec((2 * N2, 2 * N2), lambda g, b: (0, 0))
    if kf5 is None:
        kern = functools.partial(_inner_spectrum_kernel, kb)
        in_specs, args, out_dtype, name = [blk(), tw, tw, mat], (a5, tr, ti, f2), F32, "hyena_filter_spectrum"
    else:
        kern = functools.partial(_inner_conv_kernel, kb)
        kf_spec = pl.BlockSpec((1, 2, kb, N2, W), lambda g, b: (0, 0, g, 0, 0))
        in_specs, args, out_dtype, name = ([blk(), kf_spec, tw, tw, mat, mat], (a5, kf5, tr, ti, f2, f2i),
                                           BF16, "hyena_inner_conv")
    return pl.pallas_call(
        kern,
        out_shape=jax.ShapeDtypeStruct(a5.shape, out_dtype),
        grid=(N1 // kb, B),
        in_specs=in_specs,
        out_specs=blk(),
        compiler_params=_cparams("arbitrary", "arbitrary"),
        name=name,
    )(*args)


def _hyena_out_kernel(x0_ref, w0_ref, b0_ref, y_ref, u_ref, hb_ref, o_ref):
    x0 = _conv3(x0_ref[0].astype(F32), w0_ref, b0_ref)
    o_ref[0] = (x0 * (y_ref[0] + u_ref[0].astype(F32) * hb_ref[...])).astype(o_ref.dtype)


def _hyena_out(hy, conv_w, conv_b, y, u, hbias):
    B, L, W = u.shape
    nw = W // LANES
    blk = pl.BlockSpec((1, L, LANES), lambda b, j: (b, 0, j))
    vec = lambda r: pl.BlockSpec((r, LANES), lambda b, j: (0, j))
    return pl.pallas_call(
        _hyena_out_kernel,
        out_shape=jax.ShapeDtypeStruct((B, L, W), BF16),
        grid=(B, nw),
        in_specs=[blk, vec(3), vec(1), blk, blk, vec(1)],
        out_specs=blk,
        compiler_params=_cparams("arbitrary", "arbitrary"),
        name="hyena_out",
    )(hy, conv_w, conv_b, y, u, hbias)


def _dft_constants(L):
    N = 2 * L
    N2 = FFT_N2
    N1 = N // N2
    k1 = np.arange(N1, dtype=np.float64)
    th1 = 2.0 * np.pi * np.outer(k1, k1) / N1
    fwd = np.concatenate([np.cos(th1), -np.sin(th1)], axis=0)
    inv = np.concatenate([np.cos(th1), -np.sin(th1)], axis=1)[: N1 // 2] / N
    n2 = np.arange(N2, dtype=np.float64)
    tw = 2.0 * np.pi * np.outer(k1, n2) / N
    tr = np.repeat(np.cos(tw)[:, :, None], LANES, axis=2)
    ti = np.repeat(-np.sin(tw)[:, :, None], LANES, axis=2)
    th2 = 2.0 * np.pi * np.outer(n2, n2) / N2
    c2, s2 = np.cos(th2), np.sin(th2)
    f2 = np.block([[c2, s2], [-s2, c2]])
    f2i = np.block([[c2, -s2], [s2, c2]])
    f = lambda a: jnp.asarray(a, F32)
    return dict(N1=N1, N2=N2, fwd=f(fwd), inv=f(inv), tr=f(tr), ti=f(ti), f2=f(f2), f2i=f(f2i))


def _filter_features(L, emb_dim):
    bands = (emb_dim - 1) // 2
    t = np.linspace(0.0, 1.0, L, dtype=np.float32)[:, None]
    w = (np.float32(2.0 * math.pi) * np.arange(L, dtype=np.float32)[:, None] / np.float32(L)).astype(np.float32)
    f = np.linspace(1e-4, bands - 1, bands, dtype=np.float32)
    fw = (f * w).astype(np.float32)
    z = np.concatenate([t, np.cos(fw.astype(np.float64)).astype(np.float32),
                        -np.sin(fw.astype(np.float64)).astype(np.float32)], axis=-1)
    return jnp.asarray(z), t


def _decay_window(t, W):
    deltas = np.abs(np.linspace(math.log(HY_DECAY_TARGET) / HY_SLOW_DECAY,
                                math.log(HY_DECAY_TARGET) / HY_FAST_DECAY, W, dtype=np.float32))
    dec = np.exp((-t * deltas).astype(np.float32).astype(np.float64)).astype(np.float32)
    return jnp.asarray(np.concatenate([dec, dec], axis=1))


def _hyena(hy, conv_w, conv_b, fw1, fb1, fw2, fb2, fw3, fb3, fw4, ffreq, hbias):
    B, L, W3 = hy.shape
    W = W3 // 3
    C = _dft_constants(L)
    N1, N2 = C["N1"], C["N2"]
    row = lambda a: a.reshape(1, -1)
    z, t = _filter_features(L, fw1.shape[0])
    kpad = LANES - fw1.shape[0]
    z = jnp.pad(z, ((0, 0), (0, kpad)))
    taps = _implicit_filter(z, jnp.pad(fw1, ((0, kpad), (0, 0))), row(fb1), fw2, row(fb2), fw3, row(fb3), fw4,
                            row(ffreq),
                            _decay_window(t, W), tl=min(L, 512))
    k_fwd, k_bwd = taps[:, :W], taps[:, W:]
    k_full = jnp.concatenate([k_fwd, jnp.zeros((1, W), F32), k_bwd[:0:-1]], axis=0)
    lanes = N2 * W
    tn = min(lanes, 8192)
    kf = _outer_dft(C["fwd"], k_full.reshape(1, N1, lanes), F32, tn, "hyena_filter_outer")
    kf5 = _inner_stage(kf.reshape(1, 2, N1, N2, W), None, C["tr"], C["ti"], C["f2"], None, kb=min(N1, 8))
    u = _hyena_gate(hy, conv_w, row(conv_b))
    a = _outer_dft(C["fwd"][:, : N1 // 2].astype(BF16), u.reshape(B, N1 // 2, lanes), BF16, tn,
                   "hyena_outer_fwd")
    bb = _inner_stage(a.reshape(B, 2, N1, N2, W), kf5, C["tr"], C["ti"], C["f2"].astype(BF16),
                      C["f2i"].astype(BF16), kb=min(N1, 8))
    y = _outer_dft(C["inv"].astype(BF16), bb.reshape(B, 2 * N1, lanes), F32, tn, "hyena_outer_inv")
    return _hyena_out(hy, conv_w, row(conv_b), y.reshape(B, L, W), u, row(hbias))


ROUTE_ROWS = 16


def _route_tile(h, rw_ref, rb_ref, tri_ref, carry_ref):
    tm = h.shape[0]
    E = rw_ref.shape[0] // 2
    nt = (((1,), (1,)), ((), ()))
    h_hi = h.astype(BF16)
    h_lo = (h - h_hi.astype(F32)).astype(BF16)
    a = lax.dot_general(rw_ref[...], h_hi, nt, preferred_element_type=F32)
    b = lax.dot_general(rw_ref[0:E, :], h_lo, nt, preferred_element_type=F32)
    reps = tm // LANES
    logits = a[:E] + a[E:] + b + jnp.concatenate([rb_ref[...]] * reps, axis=1)
    row = lax.broadcasted_iota(jnp.int32, (E, tm), 0)
    v = logits
    ids, vals = [], []
    for _ in range(TOP_K):
        m = jnp.max(v, axis=0, keepdims=True)
        idx = jnp.min(jnp.where(v == m, row, E), axis=0, keepdims=True)
        ids.append(idx)
        vals.append(m)
        v = jnp.where(row == idx, -jnp.inf, v)
    ex = [jnp.exp(vk - vals[0]) for vk in vals]
    den = ex[0]
    for t in ex[1:]:
        den = den + t
    hot = [row == idx for idx in ids]
    member = jnp.zeros((E, tm), F32)
    for hk in hot:
        member = jnp.where(hk, 1.0, member)
    member = member.astype(BF16)
    carry = carry_ref[...]
    before = (jnp.dot(member, tri_ref[...], preferred_element_type=F32)
              + jnp.concatenate([carry] * reps, axis=1))
    ranks = [jnp.sum(jnp.where(hk, before, 0.0), axis=0, keepdims=True) for hk in hot]
    carry_ref[...] = carry + jnp.dot(member, jnp.ones((tm, LANES), BF16), preferred_element_type=F32)
    out_row = lax.broadcasted_iota(jnp.int32, (ROUTE_ROWS, tm), 0)
    rec = jnp.zeros((ROUTE_ROWS, tm), F32)
    for k in range(TOP_K):
        rec = jnp.where(out_row == k, ids[k].astype(F32), rec)
        rec = jnp.where(out_row == TOP_K + k, ex[k] / den, rec)
        rec = jnp.where(out_row == 2 * TOP_K + k, ranks[k], rec)
    return rec


def _merge_kernel(x_ref, oa_ref, ob_ref, gt_ref, g1_ref, sc_ref, sh_ref, ng_ref, wa_ref, wb_ref, wo_ref,
                  rw_ref, rb_ref, tri_ref, xo_ref, h_ref, rec_ref, cnt_ref, carry_ref):
    @pl.when((pl.program_id(0) == 0) & (pl.program_id(1) == 0))
    def _():
        carry_ref[...] = jnp.zeros_like(carry_ref)

    D = x_ref.shape[-1]
    gt = gt_ref[0]
    m = (gt[:, :D].astype(F32) * jnp.dot(oa_ref[0], wa_ref[...], preferred_element_type=F32)
         + gt[:, D:].astype(F32) * jnp.dot(ob_ref[0], wb_ref[...], preferred_element_type=F32))
    y = jnp.dot(m.astype(BF16), wo_ref[...], preferred_element_type=F32)
    xn = x_ref[0] + g1_ref[0] * y
    xo_ref[0] = xn
    h = _rms(xn, D) * ng_ref[...]
    h = h * (1.0 + sc_ref[0]) + sh_ref[0]
    h_ref[0] = h.astype(h_ref.dtype)
    rec_ref[0] = _route_tile(h, rw_ref, rb_ref, tri_ref, carry_ref)
    cnt_ref[...] = carry_ref[...]


def _merge(x, oa, ob, gates, g1, sc2, sh2, ng, wa, wb, wo, rw, rb, tm):
    B, S, D = x.shape
    E = rw.shape[1]
    rw_hi = rw.T.astype(BF16)
    rw2 = jnp.concatenate([rw_hi, (rw.T - rw_hi.astype(F32)).astype(BF16)], axis=0)
    rb2 = jnp.broadcast_to(rb.reshape(E, 1), (E, LANES))
    tri = jnp.asarray(np.triu(np.ones((tm, tm), np.float32), 1), BF16)
    act = lambda w: pl.BlockSpec((1, tm, w), lambda b, i: (b, i, 0))
    mod = pl.BlockSpec((1, 1, D), lambda b, i: (b, 0, 0))
    full = lambda a: pl.BlockSpec(a.shape, lambda b, i: (0, 0))
    return pl.pallas_call(
        _merge_kernel,
        out_shape=[jax.ShapeDtypeStruct((B, S, D), F32), jax.ShapeDtypeStruct((B, S, D), BF16),
                   jax.ShapeDtypeStruct((B, ROUTE_ROWS, S), F32), jax.ShapeDtypeStruct((E, LANES), F32)],
        grid=(B, S // tm),
        in_specs=[act(D), act(oa.shape[-1]), act(ob.shape[-1]), act(2 * D), mod, mod, mod, full(ng),
                  full(wa), full(wb), full(wo), full(rw2), full(rb2), full(tri)],
        out_specs=[act(D), act(D), pl.BlockSpec((1, ROUTE_ROWS, tm), lambda b, i: (b, 0, i)),
                   pl.BlockSpec((E, LANES), lambda b, i: (0, 0))],
        scratch_shapes=[pltpu.VMEM((E, LANES), F32)],
        compiler_params=_cparams("arbitrary", "arbitrary"),
        name="merge_norm_router",
    )(x, oa, ob, gates, g1, sc2, sh2, ng, wa, wb, wo, rw2, rb2, tri)


def _moe_kernel(be_ref, nu_ref, x_ref, w1_ref, b1_ref, w2_ref, b2_ref, o_ref, w1b_ref, w2b_ref):
    i = pl.program_id(0)

    @pl.when(i < nu_ref[0])
    def _():
        @pl.when((i == 0) | (be_ref[i] != be_ref[jnp.maximum(i - 1, 0)]))
        def _():
            w1b_ref[...] = w1_ref[0].astype(BF16)
            w2b_ref[...] = w2_ref[0].astype(BF16)

        F = w2_ref.shape[1]
        gu = jnp.dot(x_ref[...], w1b_ref[...], preferred_element_type=F32) + b1_ref[0]
        gate = jnp.minimum(gu[:, :F], SWIGLU_LIMIT)
        up = jnp.clip(gu[:, F:], -SWIGLU_LIMIT, SWIGLU_LIMIT)
        act = (up + 1.0) * gate * _sigmoid(SWIGLU_ALPHA * gate)
        y = jnp.dot(act.astype(BF16), w2b_ref[...], preferred_element_type=F32) + b2_ref[0]
        o_ref[...] = y.astype(o_ref.dtype)


def _moe_experts(block_e, n_used, xg, w1, b1, w2, b2, bm):
    R, D = xg.shape
    E, _, F2 = w1.shape
    F = w2.shape[1]
    grid_spec = pltpu.PrefetchScalarGridSpec(
        num_scalar_prefetch=2,
        grid=(R // bm,),
        in_specs=[pl.BlockSpec((bm, D), lambda i, be, nu: (i, 0)),
                  pl.BlockSpec((1, D, F2), lambda i, be, nu: (be[i], 0, 0)),
                  pl.BlockSpec((1, 1, F2), lambda i, be, nu: (be[i], 0, 0)),
                  pl.BlockSpec((1, F, D), lambda i, be, nu: (be[i], 0, 0)),
                  pl.BlockSpec((1, 1, D), lambda i, be, nu: (be[i], 0, 0))],
        out_specs=pl.BlockSpec((bm, D), lambda i, be, nu: (i, 0)),
        scratch_shapes=[pltpu.VMEM((D, F2), BF16), pltpu.VMEM((F, D), BF16)],
    )
    return pl.pallas_call(
        _moe_kernel,
        out_shape=jax.ShapeDtypeStruct((R, D), BF16),
        grid_spec=grid_spec,
        compiler_params=pltpu.CompilerParams(dimension_semantics=("arbitrary",),
                                             vmem_limit_bytes=V7X_VMEM_LIMIT_MOE_BYTES),
        name="moe_experts",
    )(block_e, n_used, xg, w1, b1.reshape(E, 1, F2), w2, b2.reshape(E, 1, D))


def _combine_kernel(x_ref, y_ref, w_ref, g_ref, o_ref):
    w = w_ref[0]
    acc = w[:, 0:1] * y_ref[0, 0].astype(F32)
    for k in range(1, y_ref.shape[0]):
        acc = acc + w[:, k:k + 1] * y_ref[k, 0].astype(F32)
    o_ref[0] = x_ref[0] + g_ref[0] * acc


def _combine(x, yg, wts, g2, tm):
    B, S, D = x.shape
    K = yg.shape[0]
    return pl.pallas_call(
        _combine_kernel,
        out_shape=jax.ShapeDtypeStruct((B, S, D), F32),
        grid=(B, S // tm),
        in_specs=[pl.BlockSpec((1, tm, D), lambda b, i: (b, i, 0)),
                  pl.BlockSpec((K, 1, tm, D), lambda b, i: (0, b, i, 0)),
                  pl.BlockSpec((1, tm, K), lambda b, i: (b, i, 0)),
                  pl.BlockSpec((1, 1, D), lambda b, i: (b, 0, 0))],
        out_specs=pl.BlockSpec((1, tm, D), lambda b, i: (b, i, 0)),
        compiler_params=_cparams("arbitrary", "arbitrary"),
        name="moe_combine",
    )(x, yg, wts, g2)


def _route(rec, counts, bm):
    B, _, S = rec.shape
    T = B * S
    E = counts.shape[0]
    I32 = jnp.int32
    rec = jnp.swapaxes(rec, 1, 2).reshape(T, ROUTE_ROWS)
    top_i = rec[:, 0:TOP_K].astype(I32)
    wts = rec[:, TOP_K:2 * TOP_K]
    rank = rec[:, 2 * TOP_K:3 * TOP_K].astype(I32)
    counts = counts.astype(I32)
    A = T * TOP_K
    padded = (counts + bm - 1) // bm * bm
    pend = jnp.cumsum(padded)
    pstart = pend - padded
    start = jnp.cumsum(counts) - counts
    experts = jnp.arange(E, dtype=I32)
    pick = lambda table, idx: jnp.sum(jnp.where(idx[..., None] == experts, table, 0), axis=-1)
    pos = pick(pstart, top_i) + rank
    keys = top_i.reshape(-1) * A + jnp.arange(A, dtype=I32)
    order = jnp.sort(keys) % A
    n_rows = A + E * bm
    n_blocks = n_rows // bm
    blk_start = jnp.arange(n_blocks, dtype=I32) * bm
    block_e = jnp.minimum(jnp.sum((pend[None, :] <= blk_start[:, None]).astype(I32), axis=1), E - 1)
    off = blk_start[:, None] + jnp.arange(bm, dtype=I32)[None, :] - pick(pstart, block_e)[:, None]
    valid = (off >= 0) & (off < pick(counts, block_e)[:, None])
    src = jnp.clip(pick(start, block_e)[:, None] + off, 0, A - 1)
    sorted_tok = order.at[src.reshape(-1)].get(mode="promise_in_bounds").reshape(n_blocks, bm) // TOP_K
    row_tok = jnp.where(valid, sorted_tok, (blk_start[:, None] + jnp.arange(bm, dtype=I32)[None, :]) % T)
    n_used = (pend[-1:] // bm).astype(I32)
    return wts, row_tok.reshape(-1), pos, block_e.astype(I32), n_used


def kernel(x, c, ctx, c_ctx, ada_w, ada_b, norm1_g, norm2_g, w_in, b_in, q_norm_g, k_norm_g, lambda_q1, lambda_k1, lambda_q2, lambda_k2, subln_g, conv_w, conv_b, filt_w1, filt_b1, filt_w2, filt_b2, filt_w3, filt_b3, filt_w4, filt_freq, hyena_bias, w_up_a, w_up_b, w_out, router_w, router_b, exp_w1, exp_b1, exp_w2, exp_b2):
    B, S, D = x.shape
    depth = ada_w.shape[0]
    assert depth == 1, "context-stream update between layers is not implemented"
    l = 0
    QC = DA_HEADS * 2 * DA_QK_DIM
    W = D // 2
    row = lambda a: a.reshape(1, -1)

    lambda_init = 0.8 - 0.6 * math.exp(-0.3 * l)
    lam = (jnp.exp(jnp.sum(lambda_q1[l] * lambda_k1[l])) - jnp.exp(jnp.sum(lambda_q2[l] * lambda_k2[l]))
           + lambda_init).reshape(1).astype(F32)

    R = -(-(B + 1) // 8) * 8
    cc = jnp.concatenate([c, c_ctx[None, :], jnp.zeros((R - B - 1, D), F32)], axis=0)
    mods = _ada(cc, ada_w[l], row(ada_b[l]))
    sh1, sc1, g1, sh2, sc2, g2 = [mods[:B, i * D:(i + 1) * D].reshape(B, 1, D) for i in range(6)]
    sh1c, sc1c = [jnp.broadcast_to(mods[B, i * D:(i + 1) * D].reshape(1, 1, D), (B, 1, D)) for i in range(2)]

    w_in_b = w_in[l].astype(BF16)
    qg = row(jnp.tile(q_norm_g[l], QC // DA_QK_DIM))
    kg = row(jnp.tile(k_norm_g[l], QC // DA_QK_DIM))
    grp = np.arange(QC) // DA_QK_DIM
    bd = jnp.asarray(grp[:, None] == grp[None, :], BF16)
    qscale = DA_QK_DIM ** -0.5 * math.log2(math.e)
    spec = (("q", 0, QC), ("k", QC, QC), ("transposed", 2 * QC, QC), ("plain", 3 * QC, 3 * W),
            ("sigmoid", 3 * QC + 3 * W, 2 * D))
    qtx, kx, vtx, hy, gates = _in_proj(x, sc1, sh1, row(norm1_g[l]), w_in_b, row(b_in[l]), qg, kg, bd,
                                       _rope_tables(S, QC), spec, tm=min(S, 512), qscale=qscale)
    spec_c = (("kc", 0, QC), ("transposed", QC, QC))
    kc, vtc = _in_proj(ctx, sc1c, sh1c, row(norm1_g[l]), w_in_b[:, QC:3 * QC], row(b_in[l][QC:3 * QC]), qg,
                       kg, bd, None, spec_c, tm=ctx.shape[1], qscale=qscale)

    k_all = jnp.concatenate([kc, kx], axis=1)
    Sk = k_all.shape[1]
    vt_all = jnp.concatenate([vtc, vtx], axis=2).reshape(B, DA_HEADS, DA_V_DIM, Sk)
    vt_all = jnp.concatenate([vt_all, jnp.ones((B, DA_HEADS, 16, Sk), BF16)], axis=2)
    oa = _attention(lam, qtx, k_all, vt_all, row(subln_g[l] * (1.0 - lambda_init)), tq=min(S // 2, 256))
    ob = _hyena(hy, conv_w[l], conv_b[l], filt_w1[l], filt_b1[l], filt_w2[l], filt_b2[l], filt_w3[l],
                filt_b3[l], filt_w4[l], filt_freq[l], hyena_bias[l])
    x_new, h2, rec, counts = _merge(x, oa, ob, gates, g1, sc2, sh2, row(norm2_g[l]), w_up_a[l].astype(BF16),
                                    w_up_b[l].astype(BF16), w_out[l].astype(BF16), router_w[l],
                                    row(router_b[l]), tm=min(S, 512))

    T = B * S
    bm = EXPERT_BLOCK
    wts, row_tok, pos, block_e, n_used = _route(rec, counts[:, 0], bm)
    xg = h2.reshape(T, D).at[row_tok].get(mode="promise_in_bounds")
    yb = _moe_experts(block_e, n_used, xg, exp_w1[l], exp_b1[l], exp_w2[l], exp_b2[l], bm)
    yg = yb.at[pos.T].get(mode="promise_in_bounds").reshape(TOP_K, B, S, D)
    return _combine(x_new, yg, wts.reshape(B, S, TOP_K), g2, tm=min(S, 512))
```

```python
import functools
import math

import jax
import jax.numpy as jnp
import numpy as np
from jax import lax
from jax.experimental import pallas as pl
from jax.experimental.pallas import tpu as pltpu

F32 = jnp.float32
BF16 = jnp.bfloat16
HIGHEST = lax.Precision.HIGHEST

GRID_W = 64
DA_HEADS = 4
DA_QK_DIM = 64
DA_V_DIM = 2 * DA_QK_DIM
ROPE_THETA = 10000.0
ROPE_FREQS = DA_QK_DIM // 4
HY_DECAY_TARGET = 1e-2
HY_FAST_DECAY = 0.3
HY_SLOW_DECAY = 1.5
TOP_K = 4
SWIGLU_LIMIT = 7.0
SWIGLU_ALPHA = 1.702
EPS = 1e-6

LANES = 128
V7X_VMEM_LIMIT_BYTES = 48 * 1024 * 1024
V7X_VMEM_LIMIT_MOE_BYTES = 56 * 1024 * 1024
FFT_N2 = 128
EXPERT_BLOCK = 512


def _cparams(*sem):
    return pltpu.CompilerParams(dimension_semantics=sem, vmem_limit_bytes=V7X_VMEM_LIMIT_BYTES)


def _sigmoid(x):
    return 1.0 / (1.0 + jnp.exp(-x))


def _rms(x, width):
    return x * lax.rsqrt(jnp.sum(x * x, axis=-1, keepdims=True) * (1.0 / width) + EPS)


def _ada_kernel(c_ref, w_ref, b_ref, o_ref):
    c = c_ref[...]
    s = c * _sigmoid(c)
    o_ref[...] = jnp.dot(s, w_ref[...], preferred_element_type=F32, precision=HIGHEST) + b_ref[...]


def _ada(cc, w, b):
    R, D = cc.shape
    N = w.shape[1]
    tn = D
    return pl.pallas_call(
        _ada_kernel,
        out_shape=jax.ShapeDtypeStruct((R, N), F32),
        grid=(N // tn,),
        in_specs=[pl.BlockSpec((R, D), lambda j: (0, 0)),
                  pl.BlockSpec((D, tn), lambda j: (0, j)),
                  pl.BlockSpec((1, tn), lambda j: (0, j))],
        out_specs=pl.BlockSpec((R, tn), lambda j: (0, j)),
        compiler_params=_cparams("arbitrary"),
        name="ada",
    )(cc, w, b)


_TRANSPOSED_KINDS = ("q", "transposed")
_KEY_KINDS = ("k", "kc", "transposed")


def _in_kernel(spec, use_rope, qscale, x_ref, sc_ref, sh_ref, g_ref, w_ref, b_ref, qg_ref, kg_ref,
               bd_ref, *rest):
    if use_rope:
        cos_ref, sin_ref, *outs = rest
    else:
        outs = rest
    x = x_ref[0]
    h = _rms(x, x.shape[-1]) * g_ref[...]
    h = (h * (1.0 + sc_ref[0]) + sh_ref[0]).astype(BF16)
    for (kind, c0, cw), o_ref in zip(spec, outs):
        y = jnp.dot(h, w_ref[:, c0:c0 + cw], preferred_element_type=F32) + b_ref[:, c0:c0 + cw]
        if kind in ("q", "k", "kc"):
            y2 = y * y
            hi = y2.astype(BF16)
            lo = (y2 - hi.astype(F32)).astype(BF16)
            ssq = (jnp.dot(hi, bd_ref[...], preferred_element_type=F32)
                   + jnp.dot(lo, bd_ref[...], preferred_element_type=F32))
            gn = qg_ref if kind == "q" else kg_ref
            y = y * lax.rsqrt(ssq * (1.0 / DA_QK_DIM) + EPS) * gn[...]
            if kind != "kc":
                lane = lax.broadcasted_iota(jnp.int32, y.shape, 1)
                partner = jnp.where((lane % (2 * ROPE_FREQS)) < ROPE_FREQS,
                                    pltpu.roll(y, cw - ROPE_FREQS, 1), pltpu.roll(y, ROPE_FREQS, 1))
                y = y * cos_ref[...] + partner * sin_ref[...]
            if kind == "q":
                y = y * qscale
        elif kind == "sigmoid":
            y = _sigmoid(y)
        if kind in _TRANSPOSED_KINDS:
            y = y.T
        o_ref[0] = y.astype(o_ref.dtype)


def _in_proj(x, sc, sh, g, w, b, qg, kg, bd, rope, spec, tm, qscale, key_len=None, key_bufs=None):
    B, S, D = x.shape
    N = w.shape[1]
    use_rope = rope is not None
    key_len = S if key_len is None else key_len
    key_off = (key_len - S) // tm if key_bufs is not None else 0
    assert key_bufs is None or (key_len - S) % tm == 0
    const2 = lambda i, bb: (0, 0)
    in_specs = [pl.BlockSpec((1, tm, D), lambda i, bb: (bb, i, 0)),
                pl.BlockSpec((1, 1, D), lambda i, bb: (bb, 0, 0)),
                pl.BlockSpec((1, 1, D), lambda i, bb: (bb, 0, 0)),
                pl.BlockSpec((1, D), const2),
                pl.BlockSpec((D, N), const2, pipeline_mode=pl.Buffered(1)),
                pl.BlockSpec((1, N), const2),
                pl.BlockSpec(qg.shape, const2),
                pl.BlockSpec(kg.shape, const2),
                pl.BlockSpec(bd.shape, const2)]
    args = [x, sc, sh, g, w, b, qg, kg, bd]
    if use_rope:
        cw = rope[0].shape[1]
        in_specs += [pl.BlockSpec((tm, cw), lambda i, bb: (i, 0))] * 2
        args += list(rope)
    out_shape, out_specs, aliases = [], [], {}
    for n, (kind, _, cw_) in enumerate(spec):
        length, off = (key_len, key_off) if kind in _KEY_KINDS else (S, 0)
        if kind in _TRANSPOSED_KINDS:
            out_shape.append(jax.ShapeDtypeStruct((B, cw_, length), BF16))
            out_specs.append(pl.BlockSpec((1, cw_, tm), functools.partial(lambda off, i, bb: (bb, 0, i + off), off)))
        else:
            out_shape.append(jax.ShapeDtypeStruct((B, length, cw_), BF16))
            out_specs.append(pl.BlockSpec((1, tm, cw_), functools.partial(lambda off, i, bb: (bb, i + off, 0), off)))
        if kind in _KEY_KINDS and key_bufs is not None:
            aliases[len(args)] = n
            in_specs.append(pl.BlockSpec(memory_space=pl.ANY))
            args.append(key_bufs[len(aliases) - 1])
    n_in = len(in_specs)

    def body(*refs):
        _in_kernel(spec, use_rope, qscale, *refs[:n_in - len(aliases)], *refs[n_in:])

    return pl.pallas_call(
        body,
        out_shape=out_shape,
        grid=(S // tm, B),
        in_specs=in_specs,
        out_specs=out_specs,
        input_output_aliases=aliases,
        compiler_params=_cparams("arbitrary", "arbitrary"),
        name="in_proj_rope" if use_rope else "in_proj_ctx",
    )(*args)


def _rope_tables(S, width):
    t = np.arange(S)
    pos = np.stack([t // GRID_W, t % GRID_W], axis=-1).astype(np.float32)
    freqs = (np.float32(ROPE_THETA) ** (-np.arange(ROPE_FREQS, dtype=np.float32) / ROPE_FREQS)).astype(np.float32)
    lane = np.arange(width)
    d = lane % DA_QK_DIM
    axis = d // (2 * ROPE_FREQS)
    half = (d % (2 * ROPE_FREQS)) // ROPE_FREQS
    f = d % ROPE_FREQS
    ang = (pos[:, axis] * freqs[f][None, :]).astype(np.float32)
    cos = np.cos(ang.astype(np.float64)).astype(np.float32)
    sin = np.sin(ang.astype(np.float64)).astype(np.float32)
    sin = np.where(half[None, :] == 0, -sin, sin)
    return jnp.asarray(cos), jnp.asarray(sin)


def _attn_kernel(tq, n_score, kc, lam_ref, q_ref, qn_ref, k_ref, vt_ref, g_ref, o_ref, st0, st1, mx0, mx1):
    Sk = k_ref.shape[1]
    dv = vt_ref.shape[1]
    ones = jnp.ones((16, kc), BF16)
    rows_a = Sk // n_score
    n_out = Sk // kc
    lam = lam_ref[0]

    def stacked(qt):
        row = lax.broadcasted_iota(jnp.int32, qt.shape, 0)
        zero = jnp.zeros_like(qt)
        return jnp.concatenate([jnp.where(row < DA_QK_DIM, qt, zero), jnp.where(row >= DA_QK_DIM, qt, zero)],
                               axis=1)

    def score_piece(qs, st, mx, j):
        r0 = j * rows_a
        st[r0:r0 + rows_a, :] = jnp.dot(k_ref[0, r0:r0 + rows_a, :], qs, preferred_element_type=F32)
        part = jnp.max(st[r0:r0 + rows_a, :].reshape(rows_a // 8, 8, st.shape[-1]), axis=0)
        mx[...] = part if j == 0 else jnp.maximum(mx[...], part)

    def out_piece(st, m, acc, j):
        e = jnp.exp2(st[j * kc:(j + 1) * kc, :] - m).astype(BF16)
        vt1 = jnp.concatenate([vt_ref[0, :, j * kc:(j + 1) * kc], ones], axis=0)
        d = jnp.dot(vt1, e, preferred_element_type=F32)
        return d if acc is None else acc + d

    def finish(acc, c):
        ot = acc[:dv] * (1.0 / acc[dv:dv + 1])
        o = (ot[:, :tq] - lam * ot[:, tq:]).T
        o = _rms(o, o.shape[-1]) * g_ref[...]
        o_ref[0, c * tq:(c + 1) * tq, :] = o.astype(o_ref.dtype)

    def overlapped(st_cur, mx_cur, q_next, st_next, mx_next, c):
        m = jnp.max(mx_cur[...], axis=0, keepdims=True)
        qs = stacked(q_next)
        acc = None
        p = 0
        for j in range(n_out):
            while p < n_score and p * n_out <= j * n_score:
                score_piece(qs, st_next, mx_next, p)
                p += 1
            acc = out_piece(st_cur, m, acc, j)
        for p in range(p, n_score):
            score_piece(qs, st_next, mx_next, p)
        finish(acc, c)

    @pl.when(pl.program_id(2) == 0)
    def _():
        qs = stacked(q_ref[0, :, 0:tq])
        for j in range(n_score):
            score_piece(qs, st0, mx0, j)

    overlapped(st0, mx0, q_ref[0, :, tq:2 * tq], st1, mx1, 0)
    overlapped(st1, mx1, qn_ref[0], st0, mx0, 1)


def _attention(lam, qt, k, vt, g, tq):
    B, W, S = qt.shape
    Sk = k.shape[1]
    H = W // LANES
    tb = 2 * tq
    n_steps = S // tb
    kc = 256 if Sk % 256 == 0 else LANES
    n_score = Sk // kc
    return pl.pallas_call(
        functools.partial(_attn_kernel, tq, n_score, kc),
        out_shape=jax.ShapeDtypeStruct((B, S, W), BF16),
        grid=(B, H, n_steps),
        in_specs=[pl.BlockSpec(memory_space=pltpu.SMEM),
                  pl.BlockSpec((1, LANES, tb), lambda b, h, i: (b, h, i)),
                  pl.BlockSpec((1, LANES, tq), lambda b, h, i: (b, h, jnp.minimum(i + 1, n_steps - 1) * 2)),
                  pl.BlockSpec((1, Sk, LANES), lambda b, h, i: (b, 0, h)),
                  pl.BlockSpec((1, LANES, Sk), lambda b, h, i: (b, h, 0)),
                  pl.BlockSpec((1, LANES), lambda b, h, i: (0, 0))],
        out_specs=pl.BlockSpec((1, tb, LANES), lambda b, h, i: (b, i, h)),
        scratch_shapes=[pltpu.VMEM((Sk, 2 * tq), F32), pltpu.VMEM((Sk, 2 * tq), F32),
                        pltpu.VMEM((8, 2 * tq), F32), pltpu.VMEM((8, 2 * tq), F32)],
        compiler_params=_cparams("arbitrary", "arbitrary", "arbitrary"),
        name="diff_attn",
    )(lam, qt, qt, k, vt, g)


def _filter_kernel(z_ref, w1, b1, w2, b2, w3, b3, w4, fr, dec_ref, o_ref):
    dot = functools.partial(jnp.dot, preferred_element_type=F32, precision=HIGHEST)
    f = fr[...]
    h = jnp.sin(f * (dot(z_ref[...], w1[...]) + b1[...]))
    h = jnp.sin(f * (dot(h, w2[...]) + b2[...]))
    h = jnp.sin(f * (dot(h, w3[...]) + b3[...]))
    taps = dot(h, w4[...]) * dec_ref[...]
    tl, W = o_ref.shape[1:]
    o_ref[0] = taps[:, :W]
    pos = lax.broadcasted_iota(jnp.int32, (tl, W), 0) + pl.program_id(0) * tl
    o_ref[1] = jnp.where(pos == 0, 0.0, taps[:, W:])


def _implicit_filter(z, w1, b1, w2, b2, w3, b3, w4, fr, dec, tl):
    L = z.shape[0]
    N = w4.shape[1]
    full = lambda a: pl.BlockSpec(a.shape, lambda i: (0, 0))
    return pl.pallas_call(
        _filter_kernel,
        out_shape=jax.ShapeDtypeStruct((2, L, N // 2), F32),
        grid=(L // tl,),
        in_specs=[pl.BlockSpec((tl, z.shape[1]), lambda i: (i, 0)),
                  full(w1), full(b1), full(w2), full(b2), full(w3), full(b3), full(w4), full(fr),
                  pl.BlockSpec((tl, N), lambda i: (i, 0))],
        out_specs=pl.BlockSpec((2, tl, N // 2), lambda i: (0, i, 0)),
        compiler_params=_cparams("arbitrary"),
        name="hyena_filter",
    )(z, w1, b1, w2, b2, w3, b3, w4, fr, dec)


def _conv3(a, w_ref, b_ref):
    L = a.shape[0]
    row = lax.broadcasted_iota(jnp.int32, a.shape, 0)
    prev = jnp.where(row == 0, 0.0, pltpu.roll(a, 1, 0))
    nxt = jnp.where(row == L - 1, 0.0, pltpu.roll(a, L - 1, 0))
    return prev * w_ref[0:1, :] + a * w_ref[1:2, :] + nxt * w_ref[2:3, :] + b_ref[...]


def _gate_kernel(x1_ref, v_ref, w1_ref, wv_ref, b1_ref, bv_ref, u_ref):
    u = _conv3(x1_ref[0].astype(F32), w1_ref, b1_ref) * _conv3(v_ref[0].astype(F32), wv_ref, bv_ref)
    u_ref[0] = u.astype(u_ref.dtype)


def _hyena_gate(hy, conv_w, conv_b):
    B, L, W3 = hy.shape
    W = W3 // 3
    nw = W // LANES
    act = lambda off: pl.BlockSpec((1, L, LANES), lambda b, j: (b, 0, j + off))
    cw = lambda off: pl.BlockSpec((3, LANES), lambda b, j: (0, j + off))
    cb = lambda off: pl.BlockSpec((1, LANES), lambda b, j: (0, j + off))
    return pl.pallas_call(
        _gate_kernel,
        out_shape=jax.ShapeDtypeStruct((B, L, W), BF16),
        grid=(B, nw),
        in_specs=[act(nw), act(2 * nw), cw(nw), cw(2 * nw), cb(nw), cb(2 * nw)],
        out_specs=pl.BlockSpec((1, L, LANES), lambda b, j: (b, 0, j)),
        compiler_params=_cparams("arbitrary", "arbitrary"),
        name="hyena_gate",
    )(hy, hy, conv_w, conv_w, conv_b, conv_b)


def _outer_dft_kernel(precision, f_ref, x_ref, o_ref):
    o_ref[0] = jnp.dot(f_ref[...], x_ref[0], preferred_element_type=F32,
                       precision=precision).astype(o_ref.dtype)


def _outer_dft(f, x, out_dtype, tn, name):
    B, K, N = x.shape
    M = f.shape[0]
    precision = HIGHEST if x.dtype == F32 else None
    return pl.pallas_call(
        functools.partial(_outer_dft_kernel, precision),
        out_shape=jax.ShapeDtypeStruct((B, M, N), out_dtype),
        grid=(B, N // tn),
        in_specs=[pl.BlockSpec((M, K), lambda b, j: (0, 0)),
                  pl.BlockSpec((1, K, tn), lambda b, j: (b, 0, j))],
        out_specs=pl.BlockSpec((1, M, tn), lambda b, j: (b, 0, j)),
        compiler_params=_cparams("arbitrary", "arbitrary"),
        name=name,
    )(f, x)


def _twiddle(re, im, tr, ti, conj):
    if conj:
        return re * tr + im * ti, im * tr - re * ti
    return re * tr - im * ti, re * ti + im * tr


def _inner_spectrum_kernel(kb, a_ref, tr_ref, ti_ref, f_ref, o_ref):
    reps = a_ref.shape[-1] // LANES
    for j in range(kb):
        tr = jnp.concatenate([tr_ref[j]] * reps, axis=-1)
        ti = jnp.concatenate([ti_ref[j]] * reps, axis=-1)
        xs = []
        for s in range(2):
            re, im = _twiddle(a_ref[s, 0, j], a_ref[s, 1, j], tr, ti, False)
            xs.append(jnp.dot(f_ref[...], jnp.concatenate([re, im], axis=0), preferred_element_type=F32,
                              precision=HIGHEST))
        n2 = xs[0].shape[0] // 2
        o_ref[0, 0, j] = xs[0][:n2] + xs[1][:n2]
        o_ref[0, 1, j] = xs[0][n2:] - xs[1][n2:]


def _inner_conv_kernel(kb, a_ref, kf_ref, tr_ref, ti_ref, f_ref, fi_ref, o_ref):
    reps = a_ref.shape[-1] // LANES
    for j in range(kb):
        tr = jnp.concatenate([tr_ref[j]] * reps, axis=-1)
        ti = jnp.concatenate([ti_ref[j]] * reps, axis=-1)
        re, im = _twiddle(a_ref[0, 0, j].astype(F32), a_ref[0, 1, j].astype(F32), tr, ti, False)
        n2 = re.shape[0]
        x = jnp.dot(f_ref[...], jnp.concatenate([re, im], axis=0).astype(BF16), preferred_element_type=F32)
        xr, xi = x[:n2], x[n2:]
        kr, ki = kf_ref[0, 0, j], kf_ref[0, 1, j]
        yr = xr * kr - xi * ki
        yi = xr * ki + xi * kr
        y = jnp.dot(fi_ref[...], jnp.concatenate([yr, yi], axis=0).astype(BF16), preferred_element_type=F32)
        re, im = _twiddle(y[:n2], y[n2:], tr, ti, True)
        o_ref[0, 0, j] = re.astype(o_ref.dtype)
        o_ref[0, 1, j] = im.astype(o_ref.dtype)


def _inner_stage(a5, kf5, tr, ti, f2, f2i, kb):
    B, _, N1, N2, W = a5.shape
    blk = lambda: pl.BlockSpec((1, 2, kb, N2, W), lambda g, b: (b, 0, g, 0, 0))
    tw = pl.BlockSpec((kb, N2, LANES), lambda g, b: (g, 0, 0))
    mat = pl.BlockSpec((2 * N2, 2 * N2), lambda g, b: (0, 0))
    if kf5 is None:
        kern = functools.partial(_inner_spectrum_kernel, kb)
        pair = pl.BlockSpec((2, 2, kb, N2, W), lambda g, b: (0, 0, g, 0, 0))
        in_specs, args, out_dtype, name = [pair, tw, tw, mat], (a5, tr, ti, f2), F32, "hyena_filter_spectrum"
        B = 1
    else:
        kern = functools.partial(_inner_conv_kernel, kb)
        kf_spec = pl.BlockSpec((1, 2, kb, N2, W), lambda g, b: (0, 0, g, 0, 0))
        in_specs, args, out_dtype, name = ([blk(), kf_spec, tw, tw, mat, mat], (a5, kf5, tr, ti, f2, f2i),
                                           BF16, "hyena_inner_conv")
    return pl.pallas_call(
        kern,
        out_shape=jax.ShapeDtypeStruct((B,) + a5.shape[1:], out_dtype),
        grid=(N1 // kb, B),
        in_specs=in_specs,
        out_specs=blk(),
        compiler_params=_cparams("arbitrary", "arbitrary"),
        name=name,
    )(*args)


def _hyena_out_kernel(x0_ref, w0_ref, b0_ref, y_ref, u_ref, hb_ref, o_ref):
    x0 = _conv3(x0_ref[0].astype(F32), w0_ref, b0_ref)
    o_ref[0] = (x0 * (y_ref[0] + u_ref[0].astype(F32) * hb_ref[...])).astype(o_ref.dtype)


def _hyena_out(hy, conv_w, conv_b, y, u, hbias):
    B, L, W = u.shape
    nw = W // LANES
    blk = pl.BlockSpec((1, L, LANES), lambda b, j: (b, 0, j))
    vec = lambda r: pl.BlockSpec((r, LANES), lambda b, j: (0, j))
    return pl.pallas_call(
        _hyena_out_kernel,
        out_shape=jax.ShapeDtypeStruct((B, L, W), BF16),
        grid=(B, nw),
        in_specs=[blk, vec(3), vec(1), blk, blk, vec(1)],
        out_specs=blk,
        compiler_params=_cparams("arbitrary", "arbitrary"),
        name="hyena_out",
    )(hy, conv_w, conv_b, y, u, hbias)


def _dft_constants(L):
    N = 2 * L
    N2 = FFT_N2
    N1 = N // N2
    k1 = np.arange(N1, dtype=np.float64)
    th1 = 2.0 * np.pi * np.outer(k1, k1) / N1
    fwd = np.concatenate([np.cos(th1), -np.sin(th1)], axis=0)
    inv = np.concatenate([np.cos(th1), -np.sin(th1)], axis=1)[: N1 // 2] / N
    n2 = np.arange(N2, dtype=np.float64)
    tw = 2.0 * np.pi * np.outer(k1, n2) / N
    tr = np.repeat(np.cos(tw)[:, :, None], LANES, axis=2)
    ti = np.repeat(-np.sin(tw)[:, :, None], LANES, axis=2)
    th2 = 2.0 * np.pi * np.outer(n2, n2) / N2
    c2, s2 = np.cos(th2), np.sin(th2)
    f2 = np.block([[c2, s2], [-s2, c2]])
    f2i = np.block([[c2, -s2], [s2, c2]])
    f = lambda a: jnp.asarray(a, F32)
    return dict(N1=N1, N2=N2, fwd=f(fwd), inv=f(inv), tr=f(tr), ti=f(ti), f2=f(f2), f2i=f(f2i))


def _filter_features(L, emb_dim):
    bands = (emb_dim - 1) // 2
    t = np.linspace(0.0, 1.0, L, dtype=np.float32)[:, None]
    w = (np.float32(2.0 * math.pi) * np.arange(L, dtype=np.float32)[:, None] / np.float32(L)).astype(np.float32)
    f = np.linspace(1e-4, bands - 1, bands, dtype=np.float32)
    fw = (f * w).astype(np.float32)
    z = np.concatenate([t, np.cos(fw.astype(np.float64)).astype(np.float32),
                        -np.sin(fw.astype(np.float64)).astype(np.float32)], axis=-1)
    return jnp.asarray(z), t


def _decay_window(t, W):
    deltas = np.abs(np.linspace(math.log(HY_DECAY_TARGET) / HY_SLOW_DECAY,
                                math.log(HY_DECAY_TARGET) / HY_FAST_DECAY, W, dtype=np.float32))
    dec = np.exp((-t * deltas).astype(np.float32).astype(np.float64)).astype(np.float32)
    return jnp.asarray(np.concatenate([dec, dec], axis=1))


def _hyena(hy, conv_w, conv_b, fw1, fb1, fw2, fb2, fw3, fb3, fw4, ffreq, hbias):
    B, L, W3 = hy.shape
    W = W3 // 3
    C = _dft_constants(L)
    N1, N2 = C["N1"], C["N2"]
    row = lambda a: a.reshape(1, -1)
    z, t = _filter_features(L, fw1.shape[0])
    kpad = LANES - fw1.shape[0]
    z = jnp.pad(z, ((0, 0), (0, kpad)))
    taps = _implicit_filter(z, jnp.pad(fw1, ((0, kpad), (0, 0))), row(fb1), fw2, row(fb2), fw3, row(fb3), fw4,
                            row(ffreq),
                            _decay_window(t, W), tl=min(L, 512))
    lanes = N2 * W
    tn = min(lanes, 8192)
    kf = _outer_dft(C["fwd"][:, : N1 // 2], taps.reshape(2, N1 // 2, lanes), F32, tn, "hyena_filter_outer")
    kf5 = _inner_stage(kf.reshape(2, 2, N1, N2, W), None, C["tr"], C["ti"], C["f2"], None, kb=min(N1, 2))
    u = _hyena_gate(hy, conv_w, row(conv_b))
    a = _outer_dft(C["fwd"][:, : N1 // 2].astype(BF16), u.reshape(B, N1 // 2, lanes), BF16, tn,
                   "hyena_outer_fwd")
    bb = _inner_stage(a.reshape(B, 2, N1, N2, W), kf5, C["tr"], C["ti"], C["f2"].astype(BF16),
                      C["f2i"].astype(BF16), kb=min(N1, 8))
    y = _outer_dft(C["inv"].astype(BF16), bb.reshape(B, 2 * N1, lanes), F32, tn, "hyena_outer_inv")
    return _hyena_out(hy, conv_w, row(conv_b), y.reshape(B, L, W), u, row(hbias))


ROUTE_ROWS = 16


def _route_tile(h, rw_ref, rb_ref, tri_ref, carry_ref):
    tm = h.shape[0]
    E = rw_ref.shape[0] // 2
    nt = (((1,), (1,)), ((), ()))
    h_hi = h.astype(BF16)
    h_lo = (h - h_hi.astype(F32)).astype(BF16)
    a = lax.dot_general(rw_ref[...], h_hi, nt, preferred_element_type=F32)
    b = lax.dot_general(rw_ref[0:E, :], h_lo, nt, preferred_element_type=F32)
    reps = tm // LANES
    logits = a[:E] + a[E:] + b + jnp.concatenate([rb_ref[...]] * reps, axis=1)
    row = lax.broadcasted_iota(jnp.int32, (E, tm), 0)
    v = logits
    ids, vals = [], []
    for _ in range(TOP_K):
        m = jnp.max(v, axis=0, keepdims=True)
        idx = jnp.min(jnp.where(v == m, row, E), axis=0, keepdims=True)
        ids.append(idx)
        vals.append(m)
        v = jnp.where(row == idx, -jnp.inf, v)
    ex = [jnp.exp(vk - vals[0]) for vk in vals]
    den = ex[0]
    for t in ex[1:]:
        den = den + t
    hot = [row == idx for idx in ids]
    member = jnp.zeros((E, tm), F32)
    for hk in hot:
        member = jnp.where(hk, 1.0, member)
    member = member.astype(BF16)
    carry = carry_ref[...]
    before = (jnp.dot(member, tri_ref[...], preferred_element_type=F32)
              + jnp.concatenate([carry] * reps, axis=1))
    ranks = [jnp.sum(jnp.where(hk, before, 0.0), axis=0, keepdims=True) for hk in hot]
    carry_ref[...] = carry + jnp.dot(member, jnp.ones((tm, LANES), BF16), preferred_element_type=F32)
    out_row = lax.broadcasted_iota(jnp.int32, (ROUTE_ROWS, tm), 0)
    rec = jnp.zeros((ROUTE_ROWS, tm), F32)
    for k in range(TOP_K):
        rec = jnp.where(out_row == k, ids[k].astype(F32), rec)
        rec = jnp.where(out_row == TOP_K + k, ex[k] / den, rec)
        rec = jnp.where(out_row == 2 * TOP_K + k, ranks[k], rec)
    return rec


def _merge_kernel(x_ref, oa_ref, ob_ref, gt_ref, g1_ref, sc_ref, sh_ref, ng_ref, wa_ref, wb_ref, wo_ref,
                  rw_ref, rb_ref, tri_ref, xo_ref, h_ref, rec_ref, cnt_ref, carry_ref):
    @pl.when((pl.program_id(0) == 0) & (pl.program_id(1) == 0))
    def _():
        carry_ref[...] = jnp.zeros_like(carry_ref)

    D = x_ref.shape[-1]
    gt = gt_ref[0]
    m = (gt[:, :D].astype(F32) * jnp.dot(oa_ref[0], wa_ref[...], preferred_element_type=F32)
         + gt[:, D:].astype(F32) * jnp.dot(ob_ref[0], wb_ref[...], preferred_element_type=F32))
    y = jnp.dot(m.astype(BF16), wo_ref[...], preferred_element_type=F32)
    xn = x_ref[0] + g1_ref[0] * y
    xo_ref[0] = xn
    h = _rms(xn, D) * ng_ref[...]
    h = h * (1.0 + sc_ref[0]) + sh_ref[0]
    h_ref[0] = h.astype(h_ref.dtype)
    rec_ref[0] = _route_tile(h, rw_ref, rb_ref, tri_ref, carry_ref)
    cnt_ref[...] = carry_ref[...]


def _merge(x, oa, ob, gates, g1, sc2, sh2, ng, wa, wb, wo, rw, rb, tm):
    B, S, D = x.shape
    E = rw.shape[1]
    rw_hi = rw.T.astype(BF16)
    rw2 = jnp.concatenate([rw_hi, (rw.T - rw_hi.astype(F32)).astype(BF16)], axis=0)
    rb2 = jnp.broadcast_to(rb.reshape(E, 1), (E, LANES))
    tri = jnp.asarray(np.triu(np.ones((tm, tm), np.float32), 1), BF16)
    act = lambda w: pl.BlockSpec((1, tm, w), lambda b, i: (b, i, 0))
    mod = pl.BlockSpec((1, 1, D), lambda b, i: (b, 0, 0))
    full = lambda a: pl.BlockSpec(a.shape, lambda b, i: (0, 0))
    return pl.pallas_call(
        _merge_kernel,
        out_shape=[jax.ShapeDtypeStruct((B, S, D), F32), jax.ShapeDtypeStruct((B, S, D), BF16),
                   jax.ShapeDtypeStruct((B, ROUTE_ROWS, S), F32), jax.ShapeDtypeStruct((E, LANES), F32)],
        grid=(B, S // tm),
        in_specs=[act(D), act(oa.shape[-1]), act(ob.shape[-1]), act(2 * D), mod, mod, mod, full(ng),
                  full(wa), full(wb), full(wo), full(rw2), full(rb2), full(tri)],
        out_specs=[act(D), act(D), pl.BlockSpec((1, ROUTE_ROWS, tm), lambda b, i: (b, 0, i)),
                   pl.BlockSpec((E, LANES), lambda b, i: (0, 0))],
        scratch_shapes=[pltpu.VMEM((E, LANES), F32)],
        compiler_params=_cparams("arbitrary", "arbitrary"),
        name="merge_norm_router",
    )(x, oa, ob, gates, g1, sc2, sh2, ng, wa, wb, wo, rw2, rb2, tri)


def _moe_kernel(be_ref, nu_ref, x_ref, w1_ref, b1_ref, w2_ref, b2_ref, o_ref, w1b_ref, w2b_ref):
    i = pl.program_id(0)

    @pl.when(i < nu_ref[0])
    def _():
        @pl.when((i == 0) | (be_ref[i] != be_ref[jnp.maximum(i - 1, 0)]))
        def _():
            w1b_ref[...] = w1_ref[0].astype(BF16)
            w2b_ref[...] = w2_ref[0].astype(BF16)

        F = w2_ref.shape[1]
        gu = jnp.dot(x_ref[...], w1b_ref[...], preferred_element_type=F32) + b1_ref[0]
        gate = jnp.minimum(gu[:, :F], SWIGLU_LIMIT)
        up = jnp.clip(gu[:, F:], -SWIGLU_LIMIT, SWIGLU_LIMIT)
        act = (up + 1.0) * gate * _sigmoid(SWIGLU_ALPHA * gate)
        y = jnp.dot(act.astype(BF16), w2b_ref[...], preferred_element_type=F32) + b2_ref[0]
        o_ref[...] = y.astype(o_ref.dtype)


def _moe_experts(block_e, n_used, xg, w1, b1, w2, b2, bm):
    R, D = xg.shape
    E, _, F2 = w1.shape
    F = w2.shape[1]
    grid_spec = pltpu.PrefetchScalarGridSpec(
        num_scalar_prefetch=2,
        grid=(R // bm,),
        in_specs=[pl.BlockSpec((bm, D), lambda i, be, nu: (i, 0)),
                  pl.BlockSpec((1, D, F2), lambda i, be, nu: (be[i], 0, 0)),
                  pl.BlockSpec((1, 1, F2), lambda i, be, nu: (be[i], 0, 0)),
                  pl.BlockSpec((1, F, D), lambda i, be, nu: (be[i], 0, 0)),
                  pl.BlockSpec((1, 1, D), lambda i, be, nu: (be[i], 0, 0))],
        out_specs=pl.BlockSpec((bm, D), lambda i, be, nu: (i, 0)),
        scratch_shapes=[pltpu.VMEM((D, F2), BF16), pltpu.VMEM((F, D), BF16)],
    )
    return pl.pallas_call(
        _moe_kernel,
        out_shape=jax.ShapeDtypeStruct((R, D), BF16),
        grid_spec=grid_spec,
        compiler_params=pltpu.CompilerParams(dimension_semantics=("arbitrary",),
                                             vmem_limit_bytes=V7X_VMEM_LIMIT_MOE_BYTES),
        name="moe_experts",
    )(block_e, n_used, xg, w1, b1.reshape(E, 1, F2), w2, b2.reshape(E, 1, D))


def _combine_kernel(x_ref, y_ref, w_ref, g_ref, o_ref):
    w = w_ref[0]
    acc = w[:, 0:1] * y_ref[0, 0].astype(F32)
    for k in range(1, y_ref.shape[0]):
        acc = acc + w[:, k:k + 1] * y_ref[k, 0].astype(F32)
    o_ref[0] = x_ref[0] + g_ref[0] * acc


def _combine(x, yg, wts, g2, tm):
    B, S, D = x.shape
    K = yg.shape[0]
    return pl.pallas_call(
        _combine_kernel,
        out_shape=jax.ShapeDtypeStruct((B, S, D), F32),
        grid=(B, S // tm),
        in_specs=[pl.BlockSpec((1, tm, D), lambda b, i: (b, i, 0)),
                  pl.BlockSpec((K, 1, tm, D), lambda b, i: (0, b, i, 0)),
                  pl.BlockSpec((1, tm, K), lambda b, i: (b, i, 0)),
                  pl.BlockSpec((1, 1, D), lambda b, i: (b, 0, 0))],
        out_specs=pl.BlockSpec((1, tm, D), lambda b, i: (b, i, 0)),
        compiler_params=_cparams("arbitrary", "arbitrary"),
        name="moe_combine",
    )(x, yg, wts, g2)


def _route(rec, counts, bm):
    B, _, S = rec.shape
    T = B * S
    E = counts.shape[0]
    I32 = jnp.int32
    rec = jnp.swapaxes(rec, 1, 2).reshape(T, ROUTE_ROWS)
    top_i = rec[:, 0:TOP_K].astype(I32)
    wts = rec[:, TOP_K:2 * TOP_K]
    rank = rec[:, 2 * TOP_K:3 * TOP_K].astype(I32)
    counts = counts.astype(I32)
    A = T * TOP_K
    padded = (counts + bm - 1) // bm * bm
    pend = jnp.cumsum(padded)
    pstart = pend - padded
    start = jnp.cumsum(counts) - counts
    experts = jnp.arange(E, dtype=I32)
    pick = lambda table, idx: jnp.sum(jnp.where(idx[..., None] == experts, table, 0), axis=-1)
    pos = pick(pstart, top_i) + rank
    keys = top_i.reshape(-1) * A + jnp.arange(A, dtype=I32)
    order = jnp.sort(keys) % A
    n_rows = A + E * bm
    n_blocks = n_rows // bm
    blk_start = jnp.arange(n_blocks, dtype=I32) * bm
    block_e = jnp.minimum(jnp.sum((pend[None, :] <= blk_start[:, None]).astype(I32), axis=1), E - 1)
    off = blk_start[:, None] + jnp.arange(bm, dtype=I32)[None, :] - pick(pstart, block_e)[:, None]
    valid = (off >= 0) & (off < pick(counts, block_e)[:, None])
    src = jnp.clip(pick(start, block_e)[:, None] + off, 0, A - 1)
    sorted_tok = order.at[src.reshape(-1)].get(mode="promise_in_bounds").reshape(n_blocks, bm) // TOP_K
    row_tok = jnp.where(valid, sorted_tok, (blk_start[:, None] + jnp.arange(bm, dtype=I32)[None, :]) % T)
    n_used = (pend[-1:] // bm).astype(I32)
    return wts, row_tok.reshape(-1), pos, block_e.astype(I32), n_used


def kernel(x, c, ctx, c_ctx, ada_w, ada_b, norm1_g, norm2_g, w_in, b_in, q_norm_g, k_norm_g, lambda_q1, lambda_k1, lambda_q2, lambda_k2, subln_g, conv_w, conv_b, filt_w1, filt_b1, filt_w2, filt_b2, filt_w3, filt_b3, filt_w4, filt_freq, hyena_bias, w_up_a, w_up_b, w_out, router_w, router_b, exp_w1, exp_b1, exp_w2, exp_b2):
    B, S, D = x.shape
    depth = ada_w.shape[0]
    assert depth == 1, "context-stream update between layers is not implemented"
    l = 0
    QC = DA_HEADS * 2 * DA_QK_DIM
    W = D // 2
    row = lambda a: a.reshape(1, -1)

    lambda_init = 0.8 - 0.6 * math.exp(-0.3 * l)
    lam = (jnp.exp(jnp.sum(lambda_q1[l] * lambda_k1[l])) - jnp.exp(jnp.sum(lambda_q2[l] * lambda_k2[l]))
           + lambda_init).reshape(1).astype(F32)

    R = -(-(B + 1) // 8) * 8
    cc = jnp.concatenate([c, c_ctx[None, :], jnp.zeros((R - B - 1, D), F32)], axis=0)
    mods = _ada(cc, ada_w[l], row(ada_b[l]))
    sh1, sc1, g1, sh2, sc2, g2 = [mods[:B, i * D:(i + 1) * D].reshape(B, 1, D) for i in range(6)]
    sh1c, sc1c = [jnp.broadcast_to(mods[B, i * D:(i + 1) * D].reshape(1, 1, D), (B, 1, D)) for i in range(2)]

    w_in_b = w_in[l].astype(BF16)
    qg = row(jnp.tile(q_norm_g[l], QC // DA_QK_DIM))
    kg = row(jnp.tile(k_norm_g[l], QC // DA_QK_DIM))
    grp = np.arange(QC) // DA_QK_DIM
    bd = jnp.asarray(grp[:, None] == grp[None, :], BF16)
    qscale = DA_QK_DIM ** -0.5 * math.log2(math.e)
    spec = (("q", 0, QC), ("k", QC, QC), ("transposed", 2 * QC, QC), ("plain", 3 * QC, 3 * W),
            ("sigmoid", 3 * QC + 3 * W, 2 * D))
    Sk = S + ctx.shape[1]
    qtx, k_all, vt_all, hy, gates = _in_proj(x, sc1, sh1, row(norm1_g[l]), w_in_b, row(b_in[l]), qg, kg, bd,
                                             _rope_tables(S, QC), spec, tm=min(S, 512), qscale=qscale,
                                             key_len=Sk)
    spec_c = (("kc", 0, QC), ("transposed", QC, QC))
    k_all, vt_all = _in_proj(ctx, sc1c, sh1c, row(norm1_g[l]), w_in_b[:, QC:3 * QC], row(b_in[l][QC:3 * QC]),
                             qg, kg, bd, None, spec_c, tm=ctx.shape[1], qscale=qscale, key_len=Sk,
                             key_bufs=[k_all, vt_all])

    oa = _attention(lam, qtx, k_all, vt_all, row(subln_g[l] * (1.0 - lambda_init)), tq=min(S // 2, 256))
    ob = _hyena(hy, conv_w[l], conv_b[l], filt_w1[l], filt_b1[l], filt_w2[l], filt_b2[l], filt_w3[l],
                filt_b3[l], filt_w4[l], filt_freq[l], hyena_bias[l])
    x_new, h2, rec, counts = _merge(x, oa, ob, gates, g1, sc2, sh2, row(norm2_g[l]), w_up_a[l].astype(BF16),
                                    w_up_b[l].astype(BF16), w_out[l].astype(BF16), router_w[l],
                                    row(router_b[l]), tm=min(S, 512))

    T = B * S
    bm = EXPERT_BLOCK
    wts, row_tok, pos, block_e, n_used = _route(rec, counts[:, 0], bm)
    xg = h2.reshape(T, D).at[row_tok].get(mode="promise_in_bounds")
    yb = _moe_experts(block_e, n_used, xg, exp_w1[l], exp_b1[l], exp_w2[l], exp_b2[l], bm)
    yg = yb.at[pos.T].get(mode="promise_in_bounds").reshape(TOP_K, B, S, D)
    return _combine(x_new, yg, wts.reshape(B, S, TOP_K), g2, tm=min(S, 512))
```

```python
import functools
import math

import jax
import jax.numpy as jnp
import numpy as np
from jax import lax
from jax.experimental import pallas as pl
from jax.experimental.pallas import tpu as pltpu

F32 = jnp.float32
BF16 = jnp.bfloat16
HIGHEST = lax.Precision.HIGHEST

GRID_W = 64
DA_HEADS = 4
DA_QK_DIM = 64
DA_V_DIM = 2 * DA_QK_DIM
ROPE_THETA = 10000.0
ROPE_FREQS = DA_QK_DIM // 4
HY_DECAY_TARGET = 1e-2
HY_FAST_DECAY = 0.3
HY_SLOW_DECAY = 1.5
TOP_K = 4
SWIGLU_LIMIT = 7.0
SWIGLU_ALPHA = 1.702
EPS = 1e-6

LANES = 128
V7X_VMEM_LIMIT_BYTES = 48 * 1024 * 1024
V7X_VMEM_LIMIT_MOE_BYTES = 56 * 1024 * 1024
FFT_N2 = 128
EXPERT_BLOCK = 512
MOE_BATCH_GROUPS = 2


def _cparams(*sem):
    return pltpu.CompilerParams(dimension_semantics=sem, vmem_limit_bytes=V7X_VMEM_LIMIT_BYTES)


def _sigmoid(x):
    return 1.0 / (1.0 + jnp.exp(-x))


def _rms(x, width):
    return x * lax.rsqrt(jnp.sum(x * x, axis=-1, keepdims=True) * (1.0 / width) + EPS)


def _ada_kernel(c_ref, w_ref, b_ref, o_ref):
    c = c_ref[...]
    s = c * _sigmoid(c)
    o_ref[...] = jnp.dot(s, w_ref[...], preferred_element_type=F32, precision=HIGHEST) + b_ref[...]


def _ada(cc, w, b):
    R, D = cc.shape
    N = w.shape[1]
    tn = D
    return pl.pallas_call(
        _ada_kernel,
        out_shape=jax.ShapeDtypeStruct((R, N), F32),
        grid=(N // tn,),
        in_specs=[pl.BlockSpec((R, D), lambda j: (0, 0)),
                  pl.BlockSpec((D, tn), lambda j: (0, j)),
                  pl.BlockSpec((1, tn), lambda j: (0, j))],
        out_specs=pl.BlockSpec((R, tn), lambda j: (0, j)),
        compiler_params=_cparams("arbitrary"),
        name="ada",
    )(cc, w, b)


_TRANSPOSED_KINDS = ("q", "transposed")


def _in_kernel(spec, use_rope, qscale, x_ref, sc_ref, sh_ref, g_ref, w_ref, b_ref, qg_ref, kg_ref,
               bd_ref, *rest):
    if use_rope:
        cos_ref, sin_ref, *outs = rest
    else:
        outs = rest
    x = x_ref[0]
    h = _rms(x, x.shape[-1]) * g_ref[...]
    h = (h * (1.0 + sc_ref[0]) + sh_ref[0]).astype(BF16)
    for (kind, c0, cw), o_ref in zip(spec, outs):
        y = jnp.dot(h, w_ref[:, c0:c0 + cw], preferred_element_type=F32) + b_ref[:, c0:c0 + cw]
        if kind in ("q", "k", "kc"):
            y2 = y * y
            hi = y2.astype(BF16)
            lo = (y2 - hi.astype(F32)).astype(BF16)
            ssq = (jnp.dot(hi, bd_ref[...], preferred_element_type=F32)
                   + jnp.dot(lo, bd_ref[...], preferred_element_type=F32))
            gn = qg_ref if kind == "q" else kg_ref
            y = y * lax.rsqrt(ssq * (1.0 / DA_QK_DIM) + EPS) * gn[...]
            if kind != "kc":
                lane = lax.broadcasted_iota(jnp.int32, y.shape, 1)
                partner = jnp.where((lane % (2 * ROPE_FREQS)) < ROPE_FREQS,
                                    pltpu.roll(y, cw - ROPE_FREQS, 1), pltpu.roll(y, ROPE_FREQS, 1))
                y = y * cos_ref[...] + partner * sin_ref[...]
            if kind == "q":
                y = y * qscale
        elif kind == "sigmoid":
            y = _sigmoid(y)
        if kind in _TRANSPOSED_KINDS:
            y = y.T
        o_ref[0] = y.astype(o_ref.dtype)


def _in_proj(x, sc, sh, g, w, b, qg, kg, bd, rope, spec, tm, qscale):
    B, S, D = x.shape
    N = w.shape[1]
    use_rope = rope is not None
    const2 = lambda i, bb: (0, 0)
    in_specs = [pl.BlockSpec((1, tm, D), lambda i, bb: (bb, i, 0)),
                pl.BlockSpec((1, 1, D), lambda i, bb: (bb, 0, 0)),
                pl.BlockSpec((1, 1, D), lambda i, bb: (bb, 0, 0)),
                pl.BlockSpec((1, D), const2),
                pl.BlockSpec((D, N), const2, pipeline_mode=pl.Buffered(1)),
                pl.BlockSpec((1, N), const2),
                pl.BlockSpec(qg.shape, const2),
                pl.BlockSpec(kg.shape, const2),
                pl.BlockSpec(bd.shape, const2)]
    args = [x, sc, sh, g, w, b, qg, kg, bd]
    if use_rope:
        cw = rope[0].shape[1]
        in_specs += [pl.BlockSpec((tm, cw), lambda i, bb: (i, 0))] * 2
        args += list(rope)
    out_shape = [jax.ShapeDtypeStruct((B, cw_, S) if kind in _TRANSPOSED_KINDS else (B, S, cw_), BF16)
                 for (kind, _, cw_) in spec]
    out_specs = [pl.BlockSpec((1, cw_, tm), lambda i, bb: (bb, 0, i)) if kind in _TRANSPOSED_KINDS
                 else pl.BlockSpec((1, tm, cw_), lambda i, bb: (bb, i, 0)) for (kind, _, cw_) in spec]
    return pl.pallas_call(
        functools.partial(_in_kernel, spec, use_rope, qscale),
        out_shape=out_shape,
        grid=(S // tm, B),
        in_specs=in_specs,
        out_specs=out_specs,
        compiler_params=_cparams("arbitrary", "arbitrary"),
        name="in_proj_rope" if use_rope else "in_proj_ctx",
    )(*args)


def _rope_tables(S, width):
    t = np.arange(S)
    pos = np.stack([t // GRID_W, t % GRID_W], axis=-1).astype(np.float32)
    freqs = (np.float32(ROPE_THETA) ** (-np.arange(ROPE_FREQS, dtype=np.float32) / ROPE_FREQS)).astype(np.float32)
    lane = np.arange(width)
    d = lane % DA_QK_DIM
    axis = d // (2 * ROPE_FREQS)
    half = (d % (2 * ROPE_FREQS)) // ROPE_FREQS
    f = d % ROPE_FREQS
    ang = (pos[:, axis] * freqs[f][None, :]).astype(np.float32)
    cos = np.cos(ang.astype(np.float64)).astype(np.float32)
    sin = np.sin(ang.astype(np.float64)).astype(np.float32)
    sin = np.where(half[None, :] == 0, -sin, sin)
    return jnp.asarray(cos), jnp.asarray(sin)


def _attn_kernel(tq, kc, lam_ref, q_ref, qn_ref, kx_ref, kc_ref, vtx_ref, vtc_ref, g_ref, o_ref,
                 st0, st1, mx0, mx1):
    n_lat = kx_ref.shape[1] // kc
    n_score = n_out = n_lat + kc_ref.shape[1] // kc
    rows_a = kc
    dv = vtx_ref.shape[1]
    ones = jnp.ones((16, kc), BF16)
    lam = lam_ref[0]

    def key_piece(j):
        return kx_ref[0, j * kc:(j + 1) * kc, :] if j < n_lat else kc_ref[0, (j - n_lat) * kc:(j - n_lat + 1) * kc, :]

    def value_piece(j):
        return vtx_ref[0, :, j * kc:(j + 1) * kc] if j < n_lat else vtc_ref[0, :, (j - n_lat) * kc:(j - n_lat + 1) * kc]

    def stacked(qt):
        row = lax.broadcasted_iota(jnp.int32, qt.shape, 0)
        zero = jnp.zeros_like(qt)
        return jnp.concatenate([jnp.where(row < DA_QK_DIM, qt, zero), jnp.where(row >= DA_QK_DIM, qt, zero)],
                               axis=1)

    def score_piece(qs, st, mx, j):
        r0 = j * rows_a
        st[r0:r0 + rows_a, :] = jnp.dot(key_piece(j), qs, preferred_element_type=F32)
        part = jnp.max(st[r0:r0 + rows_a, :].reshape(rows_a // 8, 8, st.shape[-1]), axis=0)
        mx[...] = part if j == 0 else jnp.maximum(mx[...], part)

    def out_piece(st, m, acc, j):
        e = jnp.exp2(st[j * kc:(j + 1) * kc, :] - m).astype(BF16)
        vt1 = jnp.concatenate([value_piece(j), ones], axis=0)
        d = jnp.dot(vt1, e, preferred_element_type=F32)
        return d if acc is None else acc + d

    def finish(acc, c):
        ot = acc[:dv] * (1.0 / acc[dv:dv + 1])
        o = (ot[:, :tq] - lam * ot[:, tq:]).T
        o = _rms(o, o.shape[-1]) * g_ref[...]
        o_ref[0, c * tq:(c + 1) * tq, :] = o.astype(o_ref.dtype)

    def overlapped(st_cur, mx_cur, q_next, st_next, mx_next, c):
        m = jnp.max(mx_cur[...], axis=0, keepdims=True)
        qs = stacked(q_next)
        acc = None
        p = 0
        for j in range(n_out):
            while p < n_score and p * n_out <= j * n_score:
                score_piece(qs, st_next, mx_next, p)
                p += 1
            acc = out_piece(st_cur, m, acc, j)
        for p in range(p, n_score):
            score_piece(qs, st_next, mx_next, p)
        finish(acc, c)

    @pl.when(pl.program_id(2) == 0)
    def _():
        qs = stacked(q_ref[0, :, 0:tq])
        for j in range(n_score):
            score_piece(qs, st0, mx0, j)

    overlapped(st0, mx0, q_ref[0, :, tq:2 * tq], st1, mx1, 0)
    overlapped(st1, mx1, qn_ref[0], st0, mx0, 1)


def _attention(lam, qt, kx, kc_, vtx, vtc, g, tq):
    B, W, S = qt.shape
    C = kc_.shape[1]
    Sk = S + C
    H = W // LANES
    tb = 2 * tq
    n_steps = S // tb
    kc = 256 if (S % 256 == 0 and C % 256 == 0) else LANES
    return pl.pallas_call(
        functools.partial(_attn_kernel, tq, kc),
        out_shape=jax.ShapeDtypeStruct((B, S, W), BF16),
        grid=(B, H, n_steps),
        in_specs=[pl.BlockSpec(memory_space=pltpu.SMEM),
                  pl.BlockSpec((1, LANES, tb), lambda b, h, i: (b, h, i)),
                  pl.BlockSpec((1, LANES, tq), lambda b, h, i: (b, h, jnp.minimum(i + 1, n_steps - 1) * 2)),
                  pl.BlockSpec((1, S, LANES), lambda b, h, i: (b, 0, h)),
                  pl.BlockSpec((1, C, LANES), lambda b, h, i: (b, 0, h)),
                  pl.BlockSpec((1, LANES, S), lambda b, h, i: (b, h, 0)),
                  pl.BlockSpec((1, LANES, C), lambda b, h, i: (b, h, 0)),
                  pl.BlockSpec((1, LANES), lambda b, h, i: (0, 0))],
        out_specs=pl.BlockSpec((1, tb, LANES), lambda b, h, i: (b, i, h)),
        scratch_shapes=[pltpu.VMEM((Sk, 2 * tq), F32), pltpu.VMEM((Sk, 2 * tq), F32),
                        pltpu.VMEM((8, 2 * tq), F32), pltpu.VMEM((8, 2 * tq), F32)],
        compiler_params=_cparams("arbitrary", "arbitrary", "arbitrary"),
        name="diff_attn",
    )(lam, qt, qt, kx, kc_, vtx, vtc, g)


def _filter_kernel(z_ref, w1, b1, w2, b2, w3, b3, w4, fr, dec_ref, o_ref):
    dot = functools.partial(jnp.dot, preferred_element_type=F32, precision=HIGHEST)
    f = fr[...]
    h = jnp.sin(f * (dot(z_ref[...], w1[...]) + b1[...]))
    h = jnp.sin(f * (dot(h, w2[...]) + b2[...]))
    h = jnp.sin(f * (dot(h, w3[...]) + b3[...]))
    taps = dot(h, w4[...]) * dec_ref[...]
    tl, W = o_ref.shape[1:]
    o_ref[0] = taps[:, :W]
    pos = lax.broadcasted_iota(jnp.int32, (tl, W), 0) + pl.program_id(0) * tl
    o_ref[1] = jnp.where(pos == 0, 0.0, taps[:, W:])


def _implicit_filter(z, w1, b1, w2, b2, w3, b3, w4, fr, dec, tl):
    L = z.shape[0]
    N = w4.shape[1]
    full = lambda a: pl.BlockSpec(a.shape, lambda i: (0, 0))
    return pl.pallas_call(
        _filter_kernel,
        out_shape=jax.ShapeDtypeStruct((2, L, N // 2), F32),
        grid=(L // tl,),
        in_specs=[pl.BlockSpec((tl, z.shape[1]), lambda i: (i, 0)),
                  full(w1), full(b1), full(w2), full(b2), full(w3), full(b3), full(w4), full(fr),
                  pl.BlockSpec((tl, N), lambda i: (i, 0))],
        out_specs=pl.BlockSpec((2, tl, N // 2), lambda i: (0, i, 0)),
        compiler_params=_cparams("arbitrary"),
        name="hyena_filter",
    )(z, w1, b1, w2, b2, w3, b3, w4, fr, dec)


def _conv3(a, w_ref, b_ref):
    L = a.shape[0]
    row = lax.broadcasted_iota(jnp.int32, a.shape, 0)
    prev = jnp.where(row == 0, 0.0, pltpu.roll(a, 1, 0))
    nxt = jnp.where(row == L - 1, 0.0, pltpu.roll(a, L - 1, 0))
    return prev * w_ref[0:1, :] + a * w_ref[1:2, :] + nxt * w_ref[2:3, :] + b_ref[...]


def _gate_kernel(x1_ref, v_ref, w1_ref, wv_ref, b1_ref, bv_ref, u_ref):
    u = _conv3(x1_ref[0].astype(F32), w1_ref, b1_ref) * _conv3(v_ref[0].astype(F32), wv_ref, bv_ref)
    u_ref[0] = u.astype(u_ref.dtype)


def _hyena_gate(hy, conv_w, conv_b):
    B, L, W3 = hy.shape
    W = W3 // 3
    nw = W // LANES
    act = lambda off: pl.BlockSpec((1, L, LANES), lambda b, j: (b, 0, j + off))
    cw = lambda off: pl.BlockSpec((3, LANES), lambda b, j: (0, j + off))
    cb = lambda off: pl.BlockSpec((1, LANES), lambda b, j: (0, j + off))
    return pl.pallas_call(
        _gate_kernel,
        out_shape=jax.ShapeDtypeStruct((B, L, W), BF16),
        grid=(B, nw),
        in_specs=[act(nw), act(2 * nw), cw(nw), cw(2 * nw), cb(nw), cb(2 * nw)],
        out_specs=pl.BlockSpec((1, L, LANES), lambda b, j: (b, 0, j)),
        compiler_params=_cparams("arbitrary", "arbitrary"),
        name="hyena_gate",
    )(hy, hy, conv_w, conv_w, conv_b, conv_b)


def _outer_dft_kernel(precision, f_ref, x_ref, o_ref):
    o_ref[0] = jnp.dot(f_ref[...], x_ref[0], preferred_element_type=F32,
                       precision=precision).astype(o_ref.dtype)


def _outer_dft(f, x, out_dtype, tn, name):
    B, K, N = x.shape
    M = f.shape[0]
    precision = HIGHEST if x.dtype == F32 else None
    return pl.pallas_call(
        functools.partial(_outer_dft_kernel, precision),
        out_shape=jax.ShapeDtypeStruct((B, M, N), out_dtype),
        grid=(B, N // tn),
        in_specs=[pl.BlockSpec((M, K), lambda b, j: (0, 0)),
                  pl.BlockSpec((1, K, tn), lambda b, j: (b, 0, j))],
        out_specs=pl.BlockSpec((1, M, tn), lambda b, j: (b, 0, j)),
        compiler_params=_cparams("arbitrary", "arbitrary"),
        name=name,
    )(f, x)


def _twiddle(re, im, tr, ti, conj):
    if conj:
        return re * tr + im * ti, im * tr - re * ti
    return re * tr - im * ti, re * ti + im * tr


def _inner_spectrum_kernel(kb, a_ref, tr_ref, ti_ref, f_ref, o_ref):
    reps = a_ref.shape[-1] // LANES
    for j in range(kb):
        tr = jnp.concatenate([tr_ref[j]] * reps, axis=-1)
        ti = jnp.concatenate([ti_ref[j]] * reps, axis=-1)
        xs = []
        for s in range(2):
            re, im = _twiddle(a_ref[s, 0, j], a_ref[s, 1, j], tr, ti, False)
            xs.append(jnp.dot(f_ref[...], jnp.concatenate([re, im], axis=0), preferred_element_type=F32,
                              precision=HIGHEST))
        n2 = xs[0].shape[0] // 2
        o_ref[0, 0, j] = xs[0][:n2] + xs[1][:n2]
        o_ref[0, 1, j] = xs[0][n2:] - xs[1][n2:]


def _inner_conv_kernel(kb, a_ref, kf_ref, tr_ref, ti_ref, f_ref, fi_ref, o_ref):
    reps = a_ref.shape[-1] // LANES
    for j in range(kb):
        tr = jnp.concatenate([tr_ref[j]] * reps, axis=-1)
        ti = jnp.concatenate([ti_ref[j]] * reps, axis=-1)
        re, im = _twiddle(a_ref[0, 0, j].astype(F32), a_ref[0, 1, j].astype(F32), tr, ti, False)
        n2 = re.shape[0]
        x = jnp.dot(f_ref[...], jnp.concatenate([re, im], axis=0).astype(BF16), preferred_element_type=F32)
        xr, xi = x[:n2], x[n2:]
        kr, ki = kf_ref[0, 0, j], kf_ref[0, 1, j]
        yr = xr * kr - xi * ki
        yi = xr * ki + xi * kr
        y = jnp.dot(fi_ref[...], jnp.concatenate([yr, yi], axis=0).astype(BF16), preferred_element_type=F32)
        re, im = _twiddle(y[:n2], y[n2:], tr, ti, True)
        o_ref[0, 0, j] = re.astype(o_ref.dtype)
        o_ref[0, 1, j] = im.astype(o_ref.dtype)


def _inner_stage(a5, kf5, tr, ti, f2, f2i, kb):
    B, _, N1, N2, W = a5.shape
    blk = lambda: pl.BlockSpec((1, 2, kb, N2, W), lambda g, b: (b, 0, g, 0, 0))
    tw = pl.BlockSpec((kb, N2, LANES), lambda g, b: (g, 0, 0))
    mat = pl.BlockSpec((2 * N2, 2 * N2), lambda g, b: (0, 0))
    if kf5 is None:
        kern = functools.partial(_inner_spectrum_kernel, kb)
        pair = pl.BlockSpec((2, 2, kb, N2, W), lambda g, b: (0, 0, g, 0, 0))
        in_specs, args, out_dtype, name = [pair, tw, tw, mat], (a5, tr, ti, f2), F32, "hyena_filter_spectrum"
        B = 1
    else:
        kern = functools.partial(_inner_conv_kernel, kb)
        kf_spec = pl.BlockSpec((1, 2, kb, N2, W), lambda g, b: (0, 0, g, 0, 0))
        in_specs, args, out_dtype, name = ([blk(), kf_spec, tw, tw, mat, mat], (a5, kf5, tr, ti, f2, f2i),
                                           BF16, "hyena_inner_conv")
    return pl.pallas_call(
        kern,
        out_shape=jax.ShapeDtypeStruct((B,) + a5.shape[1:], out_dtype),
        grid=(N1 // kb, B),
        in_specs=in_specs,
        out_specs=blk(),
        compiler_params=_cparams("arbitrary", "arbitrary"),
        name=name,
    )(*args)


def _hyena_out_kernel(x0_ref, w0_ref, b0_ref, y_ref, u_ref, hb_ref, o_ref):
    x0 = _conv3(x0_ref[0].astype(F32), w0_ref, b0_ref)
    o_ref[0] = (x0 * (y_ref[0] + u_ref[0].astype(F32) * hb_ref[...])).astype(o_ref.dtype)


def _hyena_out(hy, conv_w, conv_b, y, u, hbias):
    B, L, W = u.shape
    nw = W // LANES
    blk = pl.BlockSpec((1, L, LANES), lambda b, j: (b, 0, j))
    vec = lambda r: pl.BlockSpec((r, LANES), lambda b, j: (0, j))
    return pl.pallas_call(
        _hyena_out_kernel,
        out_shape=jax.ShapeDtypeStruct((B, L, W), BF16),
        grid=(B, nw),
        in_specs=[blk, vec(3), vec(1), blk, blk, vec(1)],
        out_specs=blk,
        compiler_params=_cparams("arbitrary", "arbitrary"),
        name="hyena_out",
    )(hy, conv_w, conv_b, y, u, hbias)


def _dft_constants(L):
    N = 2 * L
    N2 = FFT_N2
    N1 = N // N2
    k1 = np.arange(N1, dtype=np.float64)
    th1 = 2.0 * np.pi * np.outer(k1, k1) / N1
    fwd = np.concatenate([np.cos(th1), -np.sin(th1)], axis=0)
    inv = np.concatenate([np.cos(th1), -np.sin(th1)], axis=1)[: N1 // 2] / N
    n2 = np.arange(N2, dtype=np.float64)
    tw = 2.0 * np.pi * np.outer(k1, n2) / N
    tr = np.repeat(np.cos(tw)[:, :, None], LANES, axis=2)
    ti = np.repeat(-np.sin(tw)[:, :, None], LANES, axis=2)
    th2 = 2.0 * np.pi * np.outer(n2, n2) / N2
    c2, s2 = np.cos(th2), np.sin(th2)
    f2 = np.block([[c2, s2], [-s2, c2]])
    f2i = np.block([[c2, -s2], [s2, c2]])
    f = lambda a: jnp.asarray(a, F32)
    return dict(N1=N1, N2=N2, fwd=f(fwd), inv=f(inv), tr=f(tr), ti=f(ti), f2=f(f2), f2i=f(f2i))


def _filter_features(L, emb_dim):
    bands = (emb_dim - 1) // 2
    t = np.linspace(0.0, 1.0, L, dtype=np.float32)[:, None]
    w = (np.float32(2.0 * math.pi) * np.arange(L, dtype=np.float32)[:, None] / np.float32(L)).astype(np.float32)
    f = np.linspace(1e-4, bands - 1, bands, dtype=np.float32)
    fw = (f * w).astype(np.float32)
    z = np.concatenate([t, np.cos(fw.astype(np.float64)).astype(np.float32),
                        -np.sin(fw.astype(np.float64)).astype(np.float32)], axis=-1)
    return jnp.asarray(z), t


def _decay_window(t, W):
    deltas = np.abs(np.linspace(math.log(HY_DECAY_TARGET) / HY_SLOW_DECAY,
                                math.log(HY_DECAY_TARGET) / HY_FAST_DECAY, W, dtype=np.float32))
    dec = np.exp((-t * deltas).astype(np.float32).astype(np.float64)).astype(np.float32)
    return jnp.asarray(np.concatenate([dec, dec], axis=1))


def _hyena(hy, conv_w, conv_b, fw1, fb1, fw2, fb2, fw3, fb3, fw4, ffreq, hbias):
    B, L, W3 = hy.shape
    W = W3 // 3
    C = _dft_constants(L)
    N1, N2 = C["N1"], C["N2"]
    row = lambda a: a.reshape(1, -1)
    z, t = _filter_features(L, fw1.shape[0])
    kpad = LANES - fw1.shape[0]
    z = jnp.pad(z, ((0, 0), (0, kpad)))
    taps = _implicit_filter(z, jnp.pad(fw1, ((0, kpad), (0, 0))), row(fb1), fw2, row(fb2), fw3, row(fb3), fw4,
                            row(ffreq),
                            _decay_window(t, W), tl=min(L, 512))
    lanes = N2 * W
    tn = min(lanes, 8192)
    kf = _outer_dft(C["fwd"][:, : N1 // 2], taps.reshape(2, N1 // 2, lanes), F32, tn, "hyena_filter_outer")
    kf5 = _inner_stage(kf.reshape(2, 2, N1, N2, W), None, C["tr"], C["ti"], C["f2"], None, kb=min(N1, 2))
    u = _hyena_gate(hy, conv_w, row(conv_b))
    a = _outer_dft(C["fwd"][:, : N1 // 2].astype(BF16), u.reshape(B, N1 // 2, lanes), BF16, tn,
                   "hyena_outer_fwd")
    bb = _inner_stage(a.reshape(B, 2, N1, N2, W), kf5, C["tr"], C["ti"], C["f2"].astype(BF16),
                      C["f2i"].astype(BF16), kb=min(N1, 8))
    y = _outer_dft(C["inv"].astype(BF16), bb.reshape(B, 2 * N1, lanes), F32, tn, "hyena_outer_inv")
    return _hyena_out(hy, conv_w, row(conv_b), y.reshape(B, L, W), u, row(hbias))


ROUTE_ROWS = 16


def _route_tile(h, rw_ref, rb_ref, tri_ref, carry_ref):
    tm = h.shape[0]
    E = rw_ref.shape[0] // 2
    nt = (((1,), (1,)), ((), ()))
    h_hi = h.astype(BF16)
    h_lo = (h - h_hi.astype(F32)).astype(BF16)
    a = lax.dot_general(rw_ref[...], h_hi, nt, preferred_element_type=F32)
    b = lax.dot_general(rw_ref[0:E, :], h_lo, nt, preferred_element_type=F32)
    reps = tm // LANES
    logits = a[:E] + a[E:] + b + jnp.concatenate([rb_ref[...]] * reps, axis=1)
    row = lax.broadcasted_iota(jnp.int32, (E, tm), 0)
    v = logits
    ids, vals = [], []
    for _ in range(TOP_K):
        m = jnp.max(v, axis=0, keepdims=True)
        idx = jnp.min(jnp.where(v == m, row, E), axis=0, keepdims=True)
        ids.append(idx)
        vals.append(m)
        v = jnp.where(row == idx, -jnp.inf, v)
    ex = [jnp.exp(vk - vals[0]) for vk in vals]
    den = ex[0]
    for t in ex[1:]:
        den = den + t
    hot = [row == idx for idx in ids]
    member = jnp.zeros((E, tm), F32)
    for hk in hot:
        member = jnp.where(hk, 1.0, member)
    member = member.astype(BF16)
    carry = carry_ref[...]
    before = (jnp.dot(member, tri_ref[...], preferred_element_type=F32)
              + jnp.concatenate([carry] * reps, axis=1))
    ranks = [jnp.sum(jnp.where(hk, before, 0.0), axis=0, keepdims=True) for hk in hot]
    carry_ref[...] = carry + jnp.dot(member, jnp.ones((tm, LANES), BF16), preferred_element_type=F32)
    out_row = lax.broadcasted_iota(jnp.int32, (ROUTE_ROWS, tm), 0)
    rec = jnp.zeros((ROUTE_ROWS, tm), F32)
    for k in range(TOP_K):
        rec = jnp.where(out_row == k, ids[k].astype(F32), rec)
        rec = jnp.where(out_row == TOP_K + k, ex[k] / den, rec)
        rec = jnp.where(out_row == 2 * TOP_K + k, ranks[k], rec)
    return rec


def _merge_kernel(x_ref, oa_ref, ob_ref, gt_ref, g1_ref, sc_ref, sh_ref, ng_ref, wa_ref, wb_ref, wo_ref,
                  rw_ref, rb_ref, tri_ref, xo_ref, h_ref, rec_ref, cnt_ref, carry_ref):
    @pl.when((pl.program_id(0) == 0) & (pl.program_id(1) == 0))
    def _():
        carry_ref[...] = jnp.zeros_like(carry_ref)

    D = x_ref.shape[-1]
    gt = gt_ref[0]
    m = (gt[:, :D].astype(F32) * jnp.dot(oa_ref[0], wa_ref[...], preferred_element_type=F32)
         + gt[:, D:].astype(F32) * jnp.dot(ob_ref[0], wb_ref[...], preferred_element_type=F32))
    y = jnp.dot(m.astype(BF16), wo_ref[...], preferred_element_type=F32)
    xn = x_ref[0] + g1_ref[0] * y
    xo_ref[0] = xn
    h = _rms(xn, D) * ng_ref[...]
    h = h * (1.0 + sc_ref[0]) + sh_ref[0]
    h_ref[0] = h.astype(h_ref.dtype)
    rec_ref[0] = _route_tile(h, rw_ref, rb_ref, tri_ref, carry_ref)
    cnt_ref[...] = carry_ref[...]


def _merge(x, oa, ob, gates, g1, sc2, sh2, ng, wa, wb, wo, rw, rb, tm, b0, nb):
    _, S, D = x.shape
    B = nb
    E = rw.shape[1]
    rw_hi = rw.T.astype(BF16)
    rw2 = jnp.concatenate([rw_hi, (rw.T - rw_hi.astype(F32)).astype(BF16)], axis=0)
    rb2 = jnp.broadcast_to(rb.reshape(E, 1), (E, LANES))
    tri = jnp.asarray(np.triu(np.ones((tm, tm), np.float32), 1), BF16)
    src = lambda w: pl.BlockSpec((1, tm, w), lambda b, i: (b + b0, i, 0))
    act = lambda w: pl.BlockSpec((1, tm, w), lambda b, i: (b, i, 0))
    mod = pl.BlockSpec((1, 1, D), lambda b, i: (b + b0, 0, 0))
    full = lambda a: pl.BlockSpec(a.shape, lambda b, i: (0, 0))
    return pl.pallas_call(
        _merge_kernel,
        out_shape=[jax.ShapeDtypeStruct((B, S, D), F32), jax.ShapeDtypeStruct((B, S, D), BF16),
                   jax.ShapeDtypeStruct((B, ROUTE_ROWS, S), F32), jax.ShapeDtypeStruct((E, LANES), F32)],
        grid=(B, S // tm),
        in_specs=[src(D), src(oa.shape[-1]), src(ob.shape[-1]), src(2 * D), mod, mod, mod, full(ng),
                  full(wa), full(wb), full(wo), full(rw2), full(rb2), full(tri)],
        out_specs=[act(D), act(D), pl.BlockSpec((1, ROUTE_ROWS, tm), lambda b, i: (b, 0, i)),
                   pl.BlockSpec((E, LANES), lambda b, i: (0, 0))],
        scratch_shapes=[pltpu.VMEM((E, LANES), F32)],
        compiler_params=_cparams("arbitrary", "arbitrary"),
        name="merge_norm_router",
    )(x, oa, ob, gates, g1, sc2, sh2, ng, wa, wb, wo, rw2, rb2, tri)


def _moe_kernel(be_ref, nu_ref, x_ref, w1_ref, b1_ref, w2_ref, b2_ref, o_ref, w1b_ref, w2b_ref):
    i = pl.program_id(0)

    @pl.when(i < nu_ref[0])
    def _():
        @pl.when((i == 0) | (be_ref[i] != be_ref[jnp.maximum(i - 1, 0)]))
        def _():
            w1b_ref[...] = w1_ref[0].astype(BF16)
            w2b_ref[...] = w2_ref[0].astype(BF16)

        F = w2_ref.shape[1]
        gu = jnp.dot(x_ref[...], w1b_ref[...], preferred_element_type=F32) + b1_ref[0]
        gate = jnp.minimum(gu[:, :F], SWIGLU_LIMIT)
        up = jnp.clip(gu[:, F:], -SWIGLU_LIMIT, SWIGLU_LIMIT)
        act = (up + 1.0) * gate * _sigmoid(SWIGLU_ALPHA * gate)
        y = jnp.dot(act.astype(BF16), w2b_ref[...], preferred_element_type=F32) + b2_ref[0]
        o_ref[...] = y.astype(o_ref.dtype)


def _moe_experts(block_e, n_used, xg, w1, b1, w2, b2, bm):
    R, D = xg.shape
    E, _, F2 = w1.shape
    F = w2.shape[1]
    grid_spec = pltpu.PrefetchScalarGridSpec(
        num_scalar_prefetch=2,
        grid=(R // bm,),
        in_specs=[pl.BlockSpec((bm, D), lambda i, be, nu: (i, 0)),
                  pl.BlockSpec((1, D, F2), lambda i, be, nu: (be[i], 0, 0)),
                  pl.BlockSpec((1, 1, F2), lambda i, be, nu: (be[i], 0, 0)),
                  pl.BlockSpec((1, F, D), lambda i, be, nu: (be[i], 0, 0)),
                  pl.BlockSpec((1, 1, D), lambda i, be, nu: (be[i], 0, 0))],
        out_specs=pl.BlockSpec((bm, D), lambda i, be, nu: (i, 0)),
        scratch_shapes=[pltpu.VMEM((D, F2), BF16), pltpu.VMEM((F, D), BF16)],
    )
    return pl.pallas_call(
        _moe_kernel,
        out_shape=jax.ShapeDtypeStruct((R, D), BF16),
        grid_spec=grid_spec,
        compiler_params=pltpu.CompilerParams(dimension_semantics=("arbitrary",),
                                             vmem_limit_bytes=V7X_VMEM_LIMIT_MOE_BYTES),
        name="moe_experts",
    )(block_e, n_used, xg, w1, b1.reshape(E, 1, F2), w2, b2.reshape(E, 1, D))


def _combine_kernel(nb, g_ref, *refs):
    o_ref = refs[-1]
    b = pl.program_id(0)
    for grp in range((len(refs) - 1) // 3):
        x_ref, y_ref, w_ref = refs[3 * grp:3 * grp + 3]

        @pl.when((b >= grp * nb) & (b < (grp + 1) * nb))
        def _():
            w = w_ref[0]
            acc = w[:, 0:1] * y_ref[0, 0].astype(F32)
            for k in range(1, y_ref.shape[0]):
                acc = acc + w[:, k:k + 1] * y_ref[k, 0].astype(F32)
            o_ref[0] = x_ref[0] + g_ref[0] * acc


def _combine(xs, ygs, wtss, g2, tm):
    nb, S, D = xs[0].shape
    B = g2.shape[0]
    K = ygs[0].shape[0]
    in_specs = [pl.BlockSpec((1, 1, D), lambda b, i: (b, 0, 0))]
    args = [g2]
    def group_block(grp, b, i):
        mine = (b >= grp * nb) & (b < (grp + 1) * nb)
        return jnp.where(mine, b - grp * nb, 0), jnp.where(mine, i, 0)

    for grp in range(len(xs)):
        at = functools.partial(group_block, grp)
        in_specs += [pl.BlockSpec((1, tm, D), lambda b, i, at=at: (*at(b, i), 0)),
                     pl.BlockSpec((K, 1, tm, D), lambda b, i, at=at: (0, *at(b, i), 0)),
                     pl.BlockSpec((1, tm, K), lambda b, i, at=at: (*at(b, i), 0))]
        args += [xs[grp], ygs[grp], wtss[grp]]
    return pl.pallas_call(
        functools.partial(_combine_kernel, nb),
        out_shape=jax.ShapeDtypeStruct((B, S, D), F32),
        grid=(B, S // tm),
        in_specs=in_specs,
        out_specs=pl.BlockSpec((1, tm, D), lambda b, i: (b, i, 0)),
        compiler_params=_cparams("arbitrary", "arbitrary"),
        name="moe_combine",
    )(*args)


def _route(rec, counts, bm):
    B, _, S = rec.shape
    T = B * S
    E = counts.shape[0]
    I32 = jnp.int32
    rec = jnp.swapaxes(rec, 1, 2).reshape(T, ROUTE_ROWS)
    top_i = rec[:, 0:TOP_K].astype(I32)
    wts = rec[:, TOP_K:2 * TOP_K]
    rank = rec[:, 2 * TOP_K:3 * TOP_K].astype(I32)
    counts = counts.astype(I32)
    A = T * TOP_K
    padded = (counts + bm - 1) // bm * bm
    pend = jnp.cumsum(padded)
    pstart = pend - padded
    start = jnp.cumsum(counts) - counts
    experts = jnp.arange(E, dtype=I32)
    pick = lambda table, idx: jnp.sum(jnp.where(idx[..., None] == experts, table, 0), axis=-1)
    pos = pick(pstart, top_i) + rank
    keys = top_i.reshape(-1) * A + jnp.arange(A, dtype=I32)
    order = jnp.sort(keys) % A
    n_rows = A + E * bm
    n_blocks = n_rows // bm
    blk_start = jnp.arange(n_blocks, dtype=I32) * bm
    block_e = jnp.minimum(jnp.sum((pend[None, :] <= blk_start[:, None]).astype(I32), axis=1), E - 1)
    off = blk_start[:, None] + jnp.arange(bm, dtype=I32)[None, :] - pick(pstart, block_e)[:, None]
    valid = (off >= 0) & (off < pick(counts, block_e)[:, None])
    src = jnp.clip(pick(start, block_e)[:, None] + off, 0, A - 1)
    sorted_tok = order.at[src.reshape(-1)].get(mode="promise_in_bounds").reshape(n_blocks, bm) // TOP_K
    row_tok = jnp.where(valid, sorted_tok, (blk_start[:, None] + jnp.arange(bm, dtype=I32)[None, :]) % T)
    n_used = (pend[-1:] // bm).astype(I32)
    return wts, row_tok.reshape(-1), pos, block_e.astype(I32), n_used


def kernel(x, c, ctx, c_ctx, ada_w, ada_b, norm1_g, norm2_g, w_in, b_in, q_norm_g, k_norm_g, lambda_q1, lambda_k1, lambda_q2, lambda_k2, subln_g, conv_w, conv_b, filt_w1, filt_b1, filt_w2, filt_b2, filt_w3, filt_b3, filt_w4, filt_freq, hyena_bias, w_up_a, w_up_b, w_out, router_w, router_b, exp_w1, exp_b1, exp_w2, exp_b2):
    B, S, D = x.shape
    depth = ada_w.shape[0]
    assert depth == 1, "context-stream update between layers is not implemented"
    l = 0
    QC = DA_HEADS * 2 * DA_QK_DIM
    W = D // 2
    row = lambda a: a.reshape(1, -1)

    lambda_init = 0.8 - 0.6 * math.exp(-0.3 * l)
    lam = (jnp.exp(jnp.sum(lambda_q1[l] * lambda_k1[l])) - jnp.exp(jnp.sum(lambda_q2[l] * lambda_k2[l]))
           + lambda_init).reshape(1).astype(F32)

    R = -(-(B + 1) // 8) * 8
    cc = jnp.concatenate([c, c_ctx[None, :], jnp.zeros((R - B - 1, D), F32)], axis=0)
    mods = _ada(cc, ada_w[l], row(ada_b[l]))
    sh1, sc1, g1, sh2, sc2, g2 = [mods[:B, i * D:(i + 1) * D].reshape(B, 1, D) for i in range(6)]
    sh1c, sc1c = [jnp.broadcast_to(mods[B, i * D:(i + 1) * D].reshape(1, 1, D), (B, 1, D)) for i in range(2)]

    w_in_b = w_in[l].astype(BF16)
    qg = row(jnp.tile(q_norm_g[l], QC // DA_QK_DIM))
    kg = row(jnp.tile(k_norm_g[l], QC // DA_QK_DIM))
    grp = np.arange(QC) // DA_QK_DIM
    bd = jnp.asarray(grp[:, None] == grp[None, :], BF16)
    qscale = DA_QK_DIM ** -0.5 * math.log2(math.e)
    spec = (("q", 0, QC), ("k", QC, QC), ("transposed", 2 * QC, QC), ("plain", 3 * QC, 3 * W),
            ("sigmoid", 3 * QC + 3 * W, 2 * D))
    qtx, kx, vtx, hy, gates = _in_proj(x, sc1, sh1, row(norm1_g[l]), w_in_b, row(b_in[l]), qg, kg, bd,
                                       _rope_tables(S, QC), spec, tm=min(S, 512), qscale=qscale)
    spec_c = (("kc", 0, QC), ("transposed", QC, QC))
    kc, vtc = _in_proj(ctx, sc1c, sh1c, row(norm1_g[l]), w_in_b[:, QC:3 * QC], row(b_in[l][QC:3 * QC]), qg,
                       kg, bd, None, spec_c, tm=ctx.shape[1], qscale=qscale)

    oa = _attention(lam, qtx, kx, kc, vtx, vtc, row(subln_g[l] * (1.0 - lambda_init)), tq=min(S // 2, 256))
    ob = _hyena(hy, conv_w[l], conv_b[l], filt_w1[l], filt_b1[l], filt_w2[l], filt_b2[l], filt_w3[l],
                filt_b3[l], filt_w4[l], filt_freq[l], hyena_bias[l])
    groups = MOE_BATCH_GROUPS if B % MOE_BATCH_GROUPS == 0 else 1
    nb = B // groups
    bm = EXPERT_BLOCK
    wa, wb, wo = w_up_a[l].astype(BF16), w_up_b[l].astype(BF16), w_out[l].astype(BF16)
    xs, ygs, wtss = [], [], []
    for grp in range(groups):
        x_new, h2, rec, counts = _merge(x, oa, ob, gates, g1, sc2, sh2, row(norm2_g[l]), wa, wb, wo,
                                        router_w[l], row(router_b[l]), tm=min(S, 512), b0=grp * nb, nb=nb)
        wts, row_tok, pos, block_e, n_used = _route(rec, counts[:, 0], bm)
        xg = h2.reshape(nb * S, D).at[row_tok].get(mode="promise_in_bounds")
        yb = _moe_experts(block_e, n_used, xg, exp_w1[l], exp_b1[l], exp_w2[l], exp_b2[l], bm)
        xs.append(x_new)
        ygs.append(yb.at[pos.T].get(mode="promise_in_bounds").reshape(TOP_K, nb, S, D))
        wtss.append(wts.reshape(nb, S, TOP_K))
    return _combine(xs, ygs, wtss, g2, tm=min(S, 512))
```

```python
import functools
import math

import jax
import jax.numpy as jnp
import numpy as np
from jax import lax
from jax.experimental import pallas as pl
from jax.experimental.pallas import tpu as pltpu

F32 = jnp.float32
BF16 = jnp.bfloat16
HIGHEST = lax.Precision.HIGHEST

GRID_W = 64
DA_HEADS = 4
DA_QK_DIM = 64
DA_V_DIM = 2 * DA_QK_DIM
ROPE_THETA = 10000.0
ROPE_FREQS = DA_QK_DIM // 4
HY_DECAY_TARGET = 1e-2
HY_FAST_DECAY = 0.3
HY_SLOW_DECAY = 1.5
TOP_K = 4
SWIGLU_LIMIT = 7.0
SWIGLU_ALPHA = 1.702
EPS = 1e-6

LANES = 128
V7X_VMEM_LIMIT_BYTES = 48 * 1024 * 1024
V7X_VMEM_LIMIT_MOE_BYTES = 56 * 1024 * 1024
FFT_N2 = 128
ANCHOR_LAG = 2
EXPERT_BLOCK = 512
MOE_BATCH_GROUPS = 1


def _cparams(*sem):
    return pltpu.CompilerParams(dimension_semantics=sem, vmem_limit_bytes=V7X_VMEM_LIMIT_BYTES)


def _sigmoid(x):
    return 1.0 / (1.0 + jnp.exp(-x))


def _rms(x, width):
    return x * lax.rsqrt(jnp.sum(x * x, axis=-1, keepdims=True) * (1.0 / width) + EPS)


def _ada_kernel(c_ref, w_ref, b_ref, o_ref):
    c = c_ref[...]
    s = c * _sigmoid(c)
    o_ref[...] = jnp.dot(s, w_ref[...], preferred_element_type=F32, precision=HIGHEST) + b_ref[...]


def _ada(cc, w, b):
    R, D = cc.shape
    N = w.shape[1]
    tn = D
    return pl.pallas_call(
        _ada_kernel,
        out_shape=jax.ShapeDtypeStruct((R, N), F32),
        grid=(N // tn,),
        in_specs=[pl.BlockSpec((R, D), lambda j: (0, 0)),
                  pl.BlockSpec((D, tn), lambda j: (0, j)),
                  pl.BlockSpec((1, tn), lambda j: (0, j))],
        out_specs=pl.BlockSpec((R, tn), lambda j: (0, j)),
        compiler_params=_cparams("arbitrary"),
        name="ada",
    )(cc, w, b)


_TRANSPOSED_KINDS = ("q", "transposed")


def _in_kernel(spec, use_rope, qscale, x_ref, sc_ref, sh_ref, g_ref, w_ref, b_ref, qg_ref, kg_ref,
               bd_ref, *rest):
    if use_rope:
        cos_ref, sin_ref, *outs = rest
    else:
        outs = rest
    x = x_ref[0]
    h = _rms(x, x.shape[-1]) * g_ref[...]
    h = (h * (1.0 + sc_ref[0]) + sh_ref[0]).astype(BF16)
    for (kind, c0, cw), o_ref in zip(spec, outs):
        y = jnp.dot(h, w_ref[:, c0:c0 + cw], preferred_element_type=F32) + b_ref[:, c0:c0 + cw]
        if kind in ("q", "k", "kc"):
            y2 = y * y
            hi = y2.astype(BF16)
            lo = (y2 - hi.astype(F32)).astype(BF16)
            ssq = (jnp.dot(hi, bd_ref[...], preferred_element_type=F32)
                   + jnp.dot(lo, bd_ref[...], preferred_element_type=F32))
            gn = qg_ref if kind == "q" else kg_ref
            y = y * lax.rsqrt(ssq * (1.0 / DA_QK_DIM) + EPS) * gn[...]
            if kind != "kc":
                lane = lax.broadcasted_iota(jnp.int32, y.shape, 1)
                partner = jnp.where((lane % (2 * ROPE_FREQS)) < ROPE_FREQS,
                                    pltpu.roll(y, cw - ROPE_FREQS, 1), pltpu.roll(y, ROPE_FREQS, 1))
                y = y * cos_ref[...] + partner * sin_ref[...]
            if kind == "q":
                y = y * qscale
        elif kind == "sigmoid":
            y = _sigmoid(y)
        if kind in _TRANSPOSED_KINDS:
            y = y.T
        o_ref[0] = y.astype(o_ref.dtype)


def _in_proj(x, sc, sh, g, w, b, qg, kg, bd, rope, spec, tm, qscale):
    B, S, D = x.shape
    N = w.shape[1]
    use_rope = rope is not None
    const2 = lambda i, bb: (0, 0)
    in_specs = [pl.BlockSpec((1, tm, D), lambda i, bb: (bb, i, 0)),
                pl.BlockSpec((1, 1, D), lambda i, bb: (bb, 0, 0)),
                pl.BlockSpec((1, 1, D), lambda i, bb: (bb, 0, 0)),
                pl.BlockSpec((1, D), const2),
                pl.BlockSpec((D, N), const2, pipeline_mode=pl.Buffered(1)),
                pl.BlockSpec((1, N), const2),
                pl.BlockSpec(qg.shape, const2),
                pl.BlockSpec(kg.shape, const2),
                pl.BlockSpec(bd.shape, const2)]
    args = [x, sc, sh, g, w, b, qg, kg, bd]
    if use_rope:
        cw = rope[0].shape[1]
        in_specs += [pl.BlockSpec((tm, cw), lambda i, bb: (i, 0))] * 2
        args += list(rope)
    out_shape = [jax.ShapeDtypeStruct((B, cw_, S) if kind in _TRANSPOSED_KINDS else (B, S, cw_), BF16)
                 for (kind, _, cw_) in spec]
    out_specs = [pl.BlockSpec((1, cw_, tm), lambda i, bb: (bb, 0, i)) if kind in _TRANSPOSED_KINDS
                 else pl.BlockSpec((1, tm, cw_), lambda i, bb: (bb, i, 0)) for (kind, _, cw_) in spec]
    return pl.pallas_call(
        functools.partial(_in_kernel, spec, use_rope, qscale),
        out_shape=out_shape,
        grid=(S // tm, B),
        in_specs=in_specs,
        out_specs=out_specs,
        compiler_params=_cparams("arbitrary", "arbitrary"),
        name="in_proj_rope" if use_rope else "in_proj_ctx",
    )(*args)


def _rope_tables(S, width):
    t = np.arange(S)
    pos = np.stack([t // GRID_W, t % GRID_W], axis=-1).astype(np.float32)
    freqs = (np.float32(ROPE_THETA) ** (-np.arange(ROPE_FREQS, dtype=np.float32) / ROPE_FREQS)).astype(np.float32)
    lane = np.arange(width)
    d = lane % DA_QK_DIM
    axis = d // (2 * ROPE_FREQS)
    half = (d % (2 * ROPE_FREQS)) // ROPE_FREQS
    f = d % ROPE_FREQS
    ang = (pos[:, axis] * freqs[f][None, :]).astype(np.float32)
    cos = np.cos(ang.astype(np.float64)).astype(np.float32)
    sin = np.sin(ang.astype(np.float64)).astype(np.float32)
    sin = np.where(half[None, :] == 0, -sin, sin)
    return jnp.asarray(cos), jnp.asarray(sin)


def _attn_kernel(tq, kc, lam_ref, q_ref, qn_ref, kx_ref, kc_ref, vtx_ref, vtc_ref, g_ref, o_ref,
                 st0, st1, mx0, mx1):
    n_lat = kx_ref.shape[1] // kc
    n_score = n_out = n_lat + kc_ref.shape[1] // kc
    rows_a = kc
    dv = vtx_ref.shape[1]
    ones = jnp.ones((16, kc), BF16)
    lam = lam_ref[0]

    def key_piece(j):
        return kx_ref[0, j * kc:(j + 1) * kc, :] if j < n_lat else kc_ref[0, (j - n_lat) * kc:(j - n_lat + 1) * kc, :]

    def value_piece(j):
        return vtx_ref[0, :, j * kc:(j + 1) * kc] if j < n_lat else vtc_ref[0, :, (j - n_lat) * kc:(j - n_lat + 1) * kc]

    def stacked(qt):
        row = lax.broadcasted_iota(jnp.int32, qt.shape, 0)
        zero = jnp.zeros_like(qt)
        return jnp.concatenate([jnp.where(row < DA_QK_DIM, qt, zero), jnp.where(row >= DA_QK_DIM, qt, zero)],
                               axis=1)

    def score_piece(qs, st, mx, j):
        r0 = j * rows_a
        st[r0:r0 + rows_a, :] = jnp.dot(key_piece(j), qs, preferred_element_type=F32)
        part = jnp.max(st[r0:r0 + rows_a, :].reshape(rows_a // 8, 8, st.shape[-1]), axis=0)
        mx[...] = part if j == 0 else jnp.maximum(mx[...], part)
        return part

    def out_piece(st, m, acc, j):
        e = jnp.exp2(st[j * kc:(j + 1) * kc, :] - m).astype(BF16)
        vt1 = jnp.concatenate([value_piece(j), ones], axis=0)
        d = jnp.dot(vt1, e, preferred_element_type=F32)
        return d if acc is None else acc + d

    def finish(acc, c):
        ot = acc[:dv] * (1.0 / acc[dv:dv + 1])
        o = (ot[:, :tq] - lam * ot[:, tq:]).T
        o = _rms(o, o.shape[-1]) * g_ref[...]
        o_ref[0, c * tq:(c + 1) * tq, :] = o.astype(o_ref.dtype)

    def overlapped(st_cur, mx_cur, q_next, st_next, mx_next, c):
        m = jnp.max(mx_cur[...], axis=0, keepdims=True)
        qs = stacked(q_next)
        acc = None
        p = 0
        parts = []
        for j in range(n_out):
            while p < n_score and p * n_out <= j * n_score:
                parts.append(score_piece(qs, st_next, mx_next, p))
                p += 1
            m_j = m + 0.0 * parts[j - ANCHOR_LAG][0:1, :] if ANCHOR_LAG <= j < len(parts) + ANCHOR_LAG else m
            acc = out_piece(st_cur, m_j, acc, j)
        for p in range(p, n_score):
            score_piece(qs, st_next, mx_next, p)
        finish(acc, c)

    @pl.when(pl.program_id(2) == 0)
    def _():
        qs = stacked(q_ref[0, :, 0:tq])
        for j in range(n_score):
            score_piece(qs, st0, mx0, j)

    overlapped(st0, mx0, q_ref[0, :, tq:2 * tq], st1, mx1, 0)
    overlapped(st1, mx1, qn_ref[0], st0, mx0, 1)


def _attention(lam, qt, kx, kc_, vtx, vtc, g, tq):
    B, W, S = qt.shape
    C = kc_.shape[1]
    Sk = S + C
    H = W // LANES
    tb = 2 * tq
    n_steps = S // tb
    kc = 256 if (S % 256 == 0 and C % 256 == 0) else LANES
    return pl.pallas_call(
        functools.partial(_attn_kernel, tq, kc),
        out_shape=jax.ShapeDtypeStruct((B, S, W), BF16),
        grid=(B, H, n_steps),
        in_specs=[pl.BlockSpec(memory_space=pltpu.SMEM),
                  pl.BlockSpec((1, LANES, tb), lambda b, h, i: (b, h, i)),
                  pl.BlockSpec((1, LANES, tq), lambda b, h, i: (b, h, jnp.minimum(i + 1, n_steps - 1) * 2)),
                  pl.BlockSpec((1, S, LANES), lambda b, h, i: (b, 0, h)),
                  pl.BlockSpec((1, C, LANES), lambda b, h, i: (b, 0, h)),
                  pl.BlockSpec((1, LANES, S), lambda b, h, i: (b, h, 0)),
                  pl.BlockSpec((1, LANES, C), lambda b, h, i: (b, h, 0)),
                  pl.BlockSpec((1, LANES), lambda b, h, i: (0, 0))],
        out_specs=pl.BlockSpec((1, tb, LANES), lambda b, h, i: (b, i, h)),
        scratch_shapes=[pltpu.VMEM((Sk, 2 * tq), F32), pltpu.VMEM((Sk, 2 * tq), F32),
                        pltpu.VMEM((8, 2 * tq), F32), pltpu.VMEM((8, 2 * tq), F32)],
        compiler_params=_cparams("arbitrary", "arbitrary", "arbitrary"),
        name="diff_attn",
    )(lam, qt, qt, kx, kc_, vtx, vtc, g)


def _filter_kernel(z_ref, w1, b1, w2, b2, w3, b3, w4, fr, dec_ref, o_ref):
    dot = functools.partial(jnp.dot, preferred_element_type=F32, precision=HIGHEST)
    f = fr[...]
    h = jnp.sin(f * (dot(z_ref[...], w1[...]) + b1[...]))
    h = jnp.sin(f * (dot(h, w2[...]) + b2[...]))
    h = jnp.sin(f * (dot(h, w3[...]) + b3[...]))
    taps = dot(h, w4[...]) * dec_ref[...]
    tl, W = o_ref.shape[1:]
    o_ref[0] = taps[:, :W]
    pos = lax.broadcasted_iota(jnp.int32, (tl, W), 0) + pl.program_id(0) * tl
    o_ref[1] = jnp.where(pos == 0, 0.0, taps[:, W:])


def _implicit_filter(z, w1, b1, w2, b2, w3, b3, w4, fr, dec, tl):
    L = z.shape[0]
    N = w4.shape[1]
    full = lambda a: pl.BlockSpec(a.shape, lambda i: (0, 0))
    return pl.pallas_call(
        _filter_kernel,
        out_shape=jax.ShapeDtypeStruct((2, L, N // 2), F32),
        grid=(L // tl,),
        in_specs=[pl.BlockSpec((tl, z.shape[1]), lambda i: (i, 0)),
                  full(w1), full(b1), full(w2), full(b2), full(w3), full(b3), full(w4), full(fr),
                  pl.BlockSpec((tl, N), lambda i: (i, 0))],
        out_specs=pl.BlockSpec((2, tl, N // 2), lambda i: (0, i, 0)),
        compiler_params=_cparams("arbitrary"),
        name="hyena_filter",
    )(z, w1, b1, w2, b2, w3, b3, w4, fr, dec)


def _conv3(a, w_ref, b_ref):
    L = a.shape[0]
    row = lax.broadcasted_iota(jnp.int32, a.shape, 0)
    prev = jnp.where(row == 0, 0.0, pltpu.roll(a, 1, 0))
    nxt = jnp.where(row == L - 1, 0.0, pltpu.roll(a, L - 1, 0))
    return prev * w_ref[0:1, :] + a * w_ref[1:2, :] + nxt * w_ref[2:3, :] + b_ref[...]


def _gate_kernel(x1_ref, v_ref, w1_ref, wv_ref, b1_ref, bv_ref, u_ref):
    u = _conv3(x1_ref[0].astype(F32), w1_ref, b1_ref) * _conv3(v_ref[0].astype(F32), wv_ref, bv_ref)
    u_ref[0] = u.astype(u_ref.dtype)


def _hyena_gate(hy, conv_w, conv_b):
    B, L, W3 = hy.shape
    W = W3 // 3
    nw = W // LANES
    act = lambda off: pl.BlockSpec((1, L, LANES), lambda b, j: (b, 0, j + off))
    cw = lambda off: pl.BlockSpec((3, LANES), lambda b, j: (0, j + off))
    cb = lambda off: pl.BlockSpec((1, LANES), lambda b, j: (0, j + off))
    return pl.pallas_call(
        _gate_kernel,
        out_shape=jax.ShapeDtypeStruct((B, L, W), BF16),
        grid=(B, nw),
        in_specs=[act(nw), act(2 * nw), cw(nw), cw(2 * nw), cb(nw), cb(2 * nw)],
        out_specs=pl.BlockSpec((1, L, LANES), lambda b, j: (b, 0, j)),
        compiler_params=_cparams("arbitrary", "arbitrary"),
        name="hyena_gate",
    )(hy, hy, conv_w, conv_w, conv_b, conv_b)


def _outer_dft_kernel(precision, f_ref, x_ref, o_ref):
    o_ref[0] = jnp.dot(f_ref[...], x_ref[0], preferred_element_type=F32,
                       precision=precision).astype(o_ref.dtype)


def _outer_dft(f, x, out_dtype, tn, name):
    B, K, N = x.shape
    M = f.shape[0]
    precision = HIGHEST if x.dtype == F32 else None
    return pl.pallas_call(
        functools.partial(_outer_dft_kernel, precision),
        out_shape=jax.ShapeDtypeStruct((B, M, N), out_dtype),
        grid=(B, N // tn),
        in_specs=[pl.BlockSpec((M, K), lambda b, j: (0, 0)),
                  pl.BlockSpec((1, K, tn), lambda b, j: (b, 0, j))],
        out_specs=pl.BlockSpec((1, M, tn), lambda b, j: (b, 0, j)),
        compiler_params=_cparams("arbitrary", "arbitrary"),
        name=name,
    )(f, x)


def _twiddle(re, im, tr, ti, conj):
    if conj:
        return re * tr + im * ti, im * tr - re * ti
    return re * tr - im * ti, re * ti + im * tr


def _inner_spectrum_kernel(kb, a_ref, tr_ref, ti_ref, f_ref, o_ref):
    reps = a_ref.shape[-1] // LANES
    for j in range(kb):
        tr = jnp.concatenate([tr_ref[j]] * reps, axis=-1)
        ti = jnp.concatenate([ti_ref[j]] * reps, axis=-1)
        xs = []
        for s in range(2):
            re, im = _twiddle(a_ref[s, 0, j], a_ref[s, 1, j], tr, ti, False)
            xs.append(jnp.dot(f_ref[...], jnp.concatenate([re, im], axis=0), preferred_element_type=F32,
                              precision=HIGHEST))
        n2 = xs[0].shape[0] // 2
        o_ref[0, 0, j] = xs[0][:n2] + xs[1][:n2]
        o_ref[0, 1, j] = xs[0][n2:] - xs[1][n2:]


def _inner_conv_kernel(kb, a_ref, kf_ref, tr_ref, ti_ref, f_ref, fi_ref, o_ref):
    reps = a_ref.shape[-1] // LANES
    for j in range(kb):
        tr = jnp.concatenate([tr_ref[j]] * reps, axis=-1)
        ti = jnp.concatenate([ti_ref[j]] * reps, axis=-1)
        re, im = _twiddle(a_ref[0, 0, j].astype(F32), a_ref[0, 1, j].astype(F32), tr, ti, False)
        n2 = re.shape[0]
        x = jnp.dot(f_ref[...], jnp.concatenate([re, im], axis=0).astype(BF16), preferred_element_type=F32)
        xr, xi = x[:n2], x[n2:]
        kr, ki = kf_ref[0, 0, j], kf_ref[0, 1, j]
        yr = xr * kr - xi * ki
        yi = xr * ki + xi * kr
        y = jnp.dot(fi_ref[...], jnp.concatenate([yr, yi], axis=0).astype(BF16), preferred_element_type=F32)
        re, im = _twiddle(y[:n2], y[n2:], tr, ti, True)
        o_ref[0, 0, j] = re.astype(o_ref.dtype)
        o_ref[0, 1, j] = im.astype(o_ref.dtype)


def _inner_stage(a5, kf5, tr, ti, f2, f2i, kb):
    B, _, N1, N2, W = a5.shape
    blk = lambda: pl.BlockSpec((1, 2, kb, N2, W), lambda g, b: (b, 0, g, 0, 0))
    tw = pl.BlockSpec((kb, N2, LANES), lambda g, b: (g, 0, 0))
    mat = pl.BlockSpec((2 * N2, 2 * N2), lambda g, b: (0, 0))
    if kf5 is None:
        kern = functools.partial(_inner_spectrum_kernel, kb)
        pair = pl.BlockSpec((2, 2, kb, N2, W), lambda g, b: (0, 0, g, 0, 0))
        in_specs, args, out_dtype, name = [pair, tw, tw, mat], (a5, tr, ti, f2), F32, "hyena_filter_spectrum"
        B = 1
    else:
        kern = functools.partial(_inner_conv_kernel, kb)
        kf_spec = pl.BlockSpec((1, 2, kb, N2, W), lambda g, b: (0, 0, g, 0, 0))
        in_specs, args, out_dtype, name = ([blk(), kf_spec, tw, tw, mat, mat], (a5, kf5, tr, ti, f2, f2i),
                                           BF16, "hyena_inner_conv")
    return pl.pallas_call(
        kern,
        out_shape=jax.ShapeDtypeStruct((B,) + a5.shape[1:], out_dtype),
        grid=(N1 // kb, B),
        in_specs=in_specs,
        out_specs=blk(),
        compiler_params=_cparams("arbitrary", "arbitrary"),
        name=name,
    )(*args)


def _hyena_out_kernel(x0_ref, w0_ref, b0_ref, y_ref, u_ref, hb_ref, o_ref):
    x0 = _conv3(x0_ref[0].astype(F32), w0_ref, b0_ref)
    o_ref[0] = (x0 * (y_ref[0] + u_ref[0].astype(F32) * hb_ref[...])).astype(o_ref.dtype)


def _hyena_out(hy, conv_w, conv_b, y, u, hbias):
    B, L, W = u.shape
    nw = W // LANES
    blk = pl.BlockSpec((1, L, LANES), lambda b, j: (b, 0, j))
    vec = lambda r: pl.BlockSpec((r, LANES), lambda b, j: (0, j))
    return pl.pallas_call(
        _hyena_out_kernel,
        out_shape=jax.ShapeDtypeStruct((B, L, W), BF16),
        grid=(B, nw),
        in_specs=[blk, vec(3), vec(1), blk, blk, vec(1)],
        out_specs=blk,
        compiler_params=_cparams("arbitrary", "arbitrary"),
        name="hyena_out",
    )(hy, conv_w, conv_b, y, u, hbias)


def _dft_constants(L):
    N = 2 * L
    N2 = FFT_N2
    N1 = N // N2
    k1 = np.arange(N1, dtype=np.float64)
    th1 = 2.0 * np.pi * np.outer(k1, k1) / N1
    fwd = np.concatenate([np.cos(th1), -np.sin(th1)], axis=0)
    inv = np.concatenate([np.cos(th1), -np.sin(th1)], axis=1)[: N1 // 2] / N
    n2 = np.arange(N2, dtype=np.float64)
    tw = 2.0 * np.pi * np.outer(k1, n2) / N
    tr = np.repeat(np.cos(tw)[:, :, None], LANES, axis=2)
    ti = np.repeat(-np.sin(tw)[:, :, None], LANES, axis=2)
    th2 = 2.0 * np.pi * np.outer(n2, n2) / N2
    c2, s2 = np.cos(th2), np.sin(th2)
    f2 = np.block([[c2, s2], [-s2, c2]])
    f2i = np.block([[c2, -s2], [s2, c2]])
    f = lambda a: jnp.asarray(a, F32)
    return dict(N1=N1, N2=N2, fwd=f(fwd), inv=f(inv), tr=f(tr), ti=f(ti), f2=f(f2), f2i=f(f2i))


def _filter_features(L, emb_dim):
    bands = (emb_dim - 1) // 2
    t = np.linspace(0.0, 1.0, L, dtype=np.float32)[:, None]
    w = (np.float32(2.0 * math.pi) * np.arange(L, dtype=np.float32)[:, None] / np.float32(L)).astype(np.float32)
    f = np.linspace(1e-4, bands - 1, bands, dtype=np.float32)
    fw = (f * w).astype(np.float32)
    z = np.concatenate([t, np.cos(fw.astype(np.float64)).astype(np.float32),
                        -np.sin(fw.astype(np.float64)).astype(np.float32)], axis=-1)
    return jnp.asarray(z), t


def _decay_window(t, W):
    deltas = np.abs(np.linspace(math.log(HY_DECAY_TARGET) / HY_SLOW_DECAY,
                                math.log(HY_DECAY_TARGET) / HY_FAST_DECAY, W, dtype=np.float32))
    dec = np.exp((-t * deltas).astype(np.float32).astype(np.float64)).astype(np.float32)
    return jnp.asarray(np.concatenate([dec, dec], axis=1))


def _hyena(hy, conv_w, conv_b, fw1, fb1, fw2, fb2, fw3, fb3, fw4, ffreq, hbias):
    B, L, W3 = hy.shape
    W = W3 // 3
    C = _dft_constants(L)
    N1, N2 = C["N1"], C["N2"]
    row = lambda a: a.reshape(1, -1)
    z, t = _filter_features(L, fw1.shape[0])
    kpad = LANES - fw1.shape[0]
    z = jnp.pad(z, ((0, 0), (0, kpad)))
    taps = _implicit_filter(z, jnp.pad(fw1, ((0, kpad), (0, 0))), row(fb1), fw2, row(fb2), fw3, row(fb3), fw4,
                            row(ffreq),
                            _decay_window(t, W), tl=min(L, 512))
    lanes = N2 * W
    tn = min(lanes, 8192)
    kf = _outer_dft(C["fwd"][:, : N1 // 2], taps.reshape(2, N1 // 2, lanes), F32, tn, "hyena_filter_outer")
    kf5 = _inner_stage(kf.reshape(2, 2, N1, N2, W), None, C["tr"], C["ti"], C["f2"], None, kb=min(N1, 2))
    u = _hyena_gate(hy, conv_w, row(conv_b))
    a = _outer_dft(C["fwd"][:, : N1 // 2].astype(BF16), u.reshape(B, N1 // 2, lanes), BF16, tn,
                   "hyena_outer_fwd")
    bb = _inner_stage(a.reshape(B, 2, N1, N2, W), kf5, C["tr"], C["ti"], C["f2"].astype(BF16),
                      C["f2i"].astype(BF16), kb=min(N1, 8))
    y = _outer_dft(C["inv"].astype(BF16), bb.reshape(B, 2 * N1, lanes), F32, tn, "hyena_outer_inv")
    return _hyena_out(hy, conv_w, row(conv_b), y.reshape(B, L, W), u, row(hbias))


ROUTE_ROWS = 16


def _route_tile(h, rw_ref, rb_ref, tri_ref, carry_ref):
    tm = h.shape[0]
    E = rw_ref.shape[0] // 2
    nt = (((1,), (1,)), ((), ()))
    h_hi = h.astype(BF16)
    h_lo = (h - h_hi.astype(F32)).astype(BF16)
    a = lax.dot_general(rw_ref[...], h_hi, nt, preferred_element_type=F32)
    b = lax.dot_general(rw_ref[0:E, :], h_lo, nt, preferred_element_type=F32)
    reps = tm // LANES
    logits = a[:E] + a[E:] + b + jnp.concatenate([rb_ref[...]] * reps, axis=1)
    row = lax.broadcasted_iota(jnp.int32, (E, tm), 0)
    v = logits
    ids, vals = [], []
    for _ in range(TOP_K):
        m = jnp.max(v, axis=0, keepdims=True)
        idx = jnp.min(jnp.where(v == m, row, E), axis=0, keepdims=True)
        ids.append(idx)
        vals.append(m)
        v = jnp.where(row == idx, -jnp.inf, v)
    ex = [jnp.exp(vk - vals[0]) for vk in vals]
    den = ex[0]
    for t in ex[1:]:
        den = den + t
    hot = [row == idx for idx in ids]
    member = jnp.zeros((E, tm), F32)
    for hk in hot:
        member = jnp.where(hk, 1.0, member)
    member = member.astype(BF16)
    carry = carry_ref[...]
    before = (jnp.dot(member, tri_ref[...], preferred_element_type=F32)
              + jnp.concatenate([carry] * reps, axis=1))
    ranks = [jnp.sum(jnp.where(hk, before, 0.0), axis=0, keepdims=True) for hk in hot]
    carry_ref[...] = carry + jnp.dot(member, jnp.ones((tm, LANES), BF16), preferred_element_type=F32)
    out_row = lax.broadcasted_iota(jnp.int32, (ROUTE_ROWS, tm), 0)
    rec = jnp.zeros((ROUTE_ROWS, tm), F32)
    for k in range(TOP_K):
        rec = jnp.where(out_row == k, ids[k].astype(F32), rec)
        rec = jnp.where(out_row == TOP_K + k, ex[k] / den, rec)
        rec = jnp.where(out_row == 2 * TOP_K + k, ranks[k], rec)
    return rec


def _merge_kernel(x_ref, oa_ref, ob_ref, gt_ref, g1_ref, sc_ref, sh_ref, ng_ref, wa_ref, wb_ref, wo_ref,
                  rw_ref, rb_ref, tri_ref, xo_ref, h_ref, rec_ref, cnt_ref, carry_ref):
    @pl.when((pl.program_id(0) == 0) & (pl.program_id(1) == 0))
    def _():
        carry_ref[...] = jnp.zeros_like(carry_ref)

    D = x_ref.shape[-1]
    gt = gt_ref[0]
    m = (gt[:, :D].astype(F32) * jnp.dot(oa_ref[0], wa_ref[...], preferred_element_type=F32)
         + gt[:, D:].astype(F32) * jnp.dot(ob_ref[0], wb_ref[...], preferred_element_type=F32))
    y = jnp.dot(m.astype(BF16), wo_ref[...], preferred_element_type=F32)
    xn = x_ref[0] + g1_ref[0] * y
    xo_ref[0] = xn
    h = _rms(xn, D) * ng_ref[...]
    h = h * (1.0 + sc_ref[0]) + sh_ref[0]
    h_ref[0] = h.astype(h_ref.dtype)
    rec_ref[0] = _route_tile(h, rw_ref, rb_ref, tri_ref, carry_ref)
    cnt_ref[...] = carry_ref[...]


def _merge(x, oa, ob, gates, g1, sc2, sh2, ng, wa, wb, wo, rw, rb, tm, b0, nb):
    _, S, D = x.shape
    B = nb
    E = rw.shape[1]
    rw_hi = rw.T.astype(BF16)
    rw2 = jnp.concatenate([rw_hi, (rw.T - rw_hi.astype(F32)).astype(BF16)], axis=0)
    rb2 = jnp.broadcast_to(rb.reshape(E, 1), (E, LANES))
    tri = jnp.asarray(np.triu(np.ones((tm, tm), np.float32), 1), BF16)
    src = lambda w: pl.BlockSpec((1, tm, w), lambda b, i: (b + b0, i, 0))
    act = lambda w: pl.BlockSpec((1, tm, w), lambda b, i: (b, i, 0))
    mod = pl.BlockSpec((1, 1, D), lambda b, i: (b + b0, 0, 0))
    full = lambda a: pl.BlockSpec(a.shape, lambda b, i: (0, 0))
    return pl.pallas_call(
        _merge_kernel,
        out_shape=[jax.ShapeDtypeStruct((B, S, D), F32), jax.ShapeDtypeStruct((B, S, D), BF16),
                   jax.ShapeDtypeStruct((B, ROUTE_ROWS, S), F32), jax.ShapeDtypeStruct((E, LANES), F32)],
        grid=(B, S // tm),
        in_specs=[src(D), src(oa.shape[-1]), src(ob.shape[-1]), src(2 * D), mod, mod, mod, full(ng),
                  full(wa), full(wb), full(wo), full(rw2), full(rb2), full(tri)],
        out_specs=[act(D), act(D), pl.BlockSpec((1, ROUTE_ROWS, tm), lambda b, i: (b, 0, i)),
                   pl.BlockSpec((E, LANES), lambda b, i: (0, 0))],
        scratch_shapes=[pltpu.VMEM((E, LANES), F32)],
        compiler_params=_cparams("arbitrary", "arbitrary"),
        name="merge_norm_router",
    )(x, oa, ob, gates, g1, sc2, sh2, ng, wa, wb, wo, rw2, rb2, tri)


def _moe_kernel(be_ref, nu_ref, x_ref, w1_ref, b1_ref, w2_ref, b2_ref, o_ref, w1b_ref, w2b_ref):
    i = pl.program_id(0)

    @pl.when(i < nu_ref[0])
    def _():
        @pl.when((i == 0) | (be_ref[i] != be_ref[jnp.maximum(i - 1, 0)]))
        def _():
            w1b_ref[...] = w1_ref[0].astype(BF16)
            w2b_ref[...] = w2_ref[0].astype(BF16)

        F = w2_ref.shape[1]
        gu = jnp.dot(x_ref[...], w1b_ref[...], preferred_element_type=F32) + b1_ref[0]
        gate = jnp.minimum(gu[:, :F], SWIGLU_LIMIT)
        up = jnp.clip(gu[:, F:], -SWIGLU_LIMIT, SWIGLU_LIMIT)
        act = (up + 1.0) * gate * _sigmoid(SWIGLU_ALPHA * gate)
        y = jnp.dot(act.astype(BF16), w2b_ref[...], preferred_element_type=F32) + b2_ref[0]
        o_ref[...] = y.astype(o_ref.dtype)


def _moe_experts(block_e, n_used, xg, w1, b1, w2, b2, bm):
    R, D = xg.shape
    E, _, F2 = w1.shape
    F = w2.shape[1]
    grid_spec = pltpu.PrefetchScalarGridSpec(
        num_scalar_prefetch=2,
        grid=(R // bm,),
        in_specs=[pl.BlockSpec((bm, D), lambda i, be, nu: (i, 0)),
                  pl.BlockSpec((1, D, F2), lambda i, be, nu: (be[i], 0, 0)),
                  pl.BlockSpec((1, 1, F2), lambda i, be, nu: (be[i], 0, 0)),
                  pl.BlockSpec((1, F, D), lambda i, be, nu: (be[i], 0, 0)),
                  pl.BlockSpec((1, 1, D), lambda i, be, nu: (be[i], 0, 0))],
        out_specs=pl.BlockSpec((bm, D), lambda i, be, nu: (i, 0)),
        scratch_shapes=[pltpu.VMEM((D, F2), BF16), pltpu.VMEM((F, D), BF16)],
    )
    return pl.pallas_call(
        _moe_kernel,
        out_shape=jax.ShapeDtypeStruct((R, D), BF16),
        grid_spec=grid_spec,
        compiler_params=pltpu.CompilerParams(dimension_semantics=("arbitrary",),
                                             vmem_limit_bytes=V7X_VMEM_LIMIT_MOE_BYTES),
        name="moe_experts",
    )(block_e, n_used, xg, w1, b1.reshape(E, 1, F2), w2, b2.reshape(E, 1, D))


def _combine_kernel(nb, g_ref, *refs):
    o_ref = refs[-1]
    b = pl.program_id(0)
    for grp in range((len(refs) - 1) // 3):
        x_ref, y_ref, w_ref = refs[3 * grp:3 * grp + 3]

        @pl.when((b >= grp * nb) & (b < (grp + 1) * nb))
        def _():
            w = w_ref[0]
            acc = w[:, 0:1] * y_ref[0, 0].astype(F32)
            for k in range(1, y_ref.shape[0]):
                acc = acc + w[:, k:k + 1] * y_ref[k, 0].astype(F32)
            o_ref[0] = x_ref[0] + g_ref[0] * acc


def _combine(xs, ygs, wtss, g2, tm):
    nb, S, D = xs[0].shape
    B = g2.shape[0]
    K = ygs[0].shape[0]
    in_specs = [pl.BlockSpec((1, 1, D), lambda b, i: (b, 0, 0))]
    args = [g2]
    def group_block(grp, b, i):
        mine = (b >= grp * nb) & (b < (grp + 1) * nb)
        return jnp.where(mine, b - grp * nb, 0), jnp.where(mine, i, 0)

    for grp in range(len(xs)):
        at = functools.partial(group_block, grp)
        in_specs += [pl.BlockSpec((1, tm, D), lambda b, i, at=at: (*at(b, i), 0)),
                     pl.BlockSpec((K, 1, tm, D), lambda b, i, at=at: (0, *at(b, i), 0)),
                     pl.BlockSpec((1, tm, K), lambda b, i, at=at: (*at(b, i), 0))]
        args += [xs[grp], ygs[grp], wtss[grp]]
    return pl.pallas_call(
        functools.partial(_combine_kernel, nb),
        out_shape=jax.ShapeDtypeStruct((B, S, D), F32),
        grid=(B, S // tm),
        in_specs=in_specs,
        out_specs=pl.BlockSpec((1, tm, D), lambda b, i: (b, i, 0)),
        compiler_params=_cparams("arbitrary", "arbitrary"),
        name="moe_combine",
    )(*args)


def _route(rec, counts, bm):
    B, _, S = rec.shape
    T = B * S
    E = counts.shape[0]
    I32 = jnp.int32
    rec = jnp.swapaxes(rec, 1, 2).reshape(T, ROUTE_ROWS)
    top_i = rec[:, 0:TOP_K].astype(I32)
    wts = rec[:, TOP_K:2 * TOP_K]
    rank = rec[:, 2 * TOP_K:3 * TOP_K].astype(I32)
    counts = counts.astype(I32)
    A = T * TOP_K
    padded = (counts + bm - 1) // bm * bm
    pend = jnp.cumsum(padded)
    pstart = pend - padded
    start = jnp.cumsum(counts) - counts
    experts = jnp.arange(E, dtype=I32)
    pick = lambda table, idx: jnp.sum(jnp.where(idx[..., None] == experts, table, 0), axis=-1)
    pos = pick(pstart, top_i) + rank
    keys = top_i.reshape(-1) * A + jnp.arange(A, dtype=I32)
    order = jnp.sort(keys) % A
    n_rows = A + E * bm
    n_blocks = n_rows // bm
    blk_start = jnp.arange(n_blocks, dtype=I32) * bm
    block_e = jnp.minimum(jnp.sum((pend[None, :] <= blk_start[:, None]).astype(I32), axis=1), E - 1)
    off = blk_start[:, None] + jnp.arange(bm, dtype=I32)[None, :] - pick(pstart, block_e)[:, None]
    valid = (off >= 0) & (off < pick(counts, block_e)[:, None])
    src = jnp.clip(pick(start, block_e)[:, None] + off, 0, A - 1)
    sorted_tok = order.at[src.reshape(-1)].get(mode="promise_in_bounds").reshape(n_blocks, bm) // TOP_K
    row_tok = jnp.where(valid, sorted_tok, (blk_start[:, None] + jnp.arange(bm, dtype=I32)[None, :]) % T)
    n_used = (pend[-1:] // bm).astype(I32)
    return wts, row_tok.reshape(-1), pos, block_e.astype(I32), n_used


def kernel(x, c, ctx, c_ctx, ada_w, ada_b, norm1_g, norm2_g, w_in, b_in, q_norm_g, k_norm_g, lambda_q1, lambda_k1, lambda_q2, lambda_k2, subln_g, conv_w, conv_b, filt_w1, filt_b1, filt_w2, filt_b2, filt_w3, filt_b3, filt_w4, filt_freq, hyena_bias, w_up_a, w_up_b, w_out, router_w, router_b, exp_w1, exp_b1, exp_w2, exp_b2):
    B, S, D = x.shape
    depth = ada_w.shape[0]
    assert depth == 1, "context-stream update between layers is not implemented"
    l = 0
    QC = DA_HEADS * 2 * DA_QK_DIM
    W = D // 2
    row = lambda a: a.reshape(1, -1)

    lambda_init = 0.8 - 0.6 * math.exp(-0.3 * l)
    lam = (jnp.exp(jnp.sum(lambda_q1[l] * lambda_k1[l])) - jnp.exp(jnp.sum(lambda_q2[l] * lambda_k2[l]))
           + lambda_init).reshape(1).astype(F32)

    R = -(-(B + 1) // 8) * 8
    cc = jnp.concatenate([c, c_ctx[None, :], jnp.zeros((R - B - 1, D), F32)], axis=0)
    mods = _ada(cc, ada_w[l], row(ada_b[l]))
    sh1, sc1, g1, sh2, sc2, g2 = [mods[:B, i * D:(i + 1) * D].reshape(B, 1, D) for i in range(6)]
    sh1c, sc1c = [jnp.broadcast_to(mods[B, i * D:(i + 1) * D].reshape(1, 1, D), (B, 1, D)) for i in range(2)]

    w_in_b = w_in[l].astype(BF16)
    qg = row(jnp.tile(q_norm_g[l], QC // DA_QK_DIM))
    kg = row(jnp.tile(k_norm_g[l], QC // DA_QK_DIM))
    grp = np.arange(QC) // DA_QK_DIM
    bd = jnp.asarray(grp[:, None] == grp[None, :], BF16)
    qscale = DA_QK_DIM ** -0.5 * math.log2(math.e)
    spec = (("q", 0, QC), ("k", QC, QC), ("transposed", 2 * QC, QC), ("plain", 3 * QC, 3 * W),
            ("sigmoid", 3 * QC + 3 * W, 2 * D))
    qtx, kx, vtx, hy, gates = _in_proj(x, sc1, sh1, row(norm1_g[l]), w_in_b, row(b_in[l]), qg, kg, bd,
                                       _rope_tables(S, QC), spec, tm=min(S, 512), qscale=qscale)
    spec_c = (("kc", 0, QC), ("transposed", QC, QC))
    kc, vtc = _in_proj(ctx, sc1c, sh1c, row(norm1_g[l]), w_in_b[:, QC:3 * QC], row(b_in[l][QC:3 * QC]), qg,
                       kg, bd, None, spec_c, tm=ctx.shape[1], qscale=qscale)

    oa = _attention(lam, qtx, kx, kc, vtx, vtc, row(subln_g[l] * (1.0 - lambda_init)), tq=min(S // 2, 256))
    ob = _hyena(hy, conv_w[l], conv_b[l], filt_w1[l], filt_b1[l], filt_w2[l], filt_b2[l], filt_w3[l],
                filt_b3[l], filt_w4[l], filt_freq[l], hyena_bias[l])
    groups = MOE_BATCH_GROUPS if B % MOE_BATCH_GROUPS == 0 else 1
    nb = B // groups
    bm = EXPERT_BLOCK
    wa, wb, wo = w_up_a[l].astype(BF16), w_up_b[l].astype(BF16), w_out[l].astype(BF16)
    xs, ygs, wtss = [], [], []
    for grp in range(groups):
        x_new, h2, rec, counts = _merge(x, oa, ob, gates, g1, sc2, sh2, row(norm2_g[l]), wa, wb, wo,
                                        router_w[l], row(router_b[l]), tm=min(S, 512), b0=grp * nb, nb=nb)
        wts, row_tok, pos, block_e, n_used = _route(rec, counts[:, 0], bm)
        xg = h2.reshape(nb * S, D).at[row_tok].get(mode="promise_in_bounds")
        yb = _moe_experts(block_e, n_used, xg, exp_w1[l], exp_b1[l], exp_w2[l], exp_b2[l], bm)
        xs.append(x_new)
        ygs.append(yb.at[pos.T].get(mode="promise_in_bounds").reshape(TOP_K, nb, S, D))
        wtss.append(wts.reshape(nb, S, TOP_K))
    return _combine(xs, ygs, wtss, g2, tm=min(S, 512))
```

```python
import functools
import math

import jax
import jax.numpy as jnp
import numpy as np
from jax import lax
from jax.experimental import pallas as pl
from jax.experimental.pallas import tpu as pltpu

F32 = jnp.float32
BF16 = jnp.bfloat16
HIGHEST = lax.Precision.HIGHEST

GRID_W = 64
DA_HEADS = 4
DA_QK_DIM = 64
DA_V_DIM = 2 * DA_QK_DIM
ROPE_THETA = 10000.0
ROPE_FREQS = DA_QK_DIM // 4
HY_DECAY_TARGET = 1e-2
HY_FAST_DECAY = 0.3
HY_SLOW_DECAY = 1.5
TOP_K = 4
SWIGLU_LIMIT = 7.0
SWIGLU_ALPHA = 1.702
EPS = 1e-6

LANES = 128
V7X_VMEM_LIMIT_BYTES = 48 * 1024 * 1024
V7X_VMEM_LIMIT_MOE_BYTES = 56 * 1024 * 1024
FFT_N2 = 128
ANCHOR_LAG = 2
EXPERT_BLOCK = 512
MOE_BATCH_GROUPS = 2


def _cparams(*sem):
    return pltpu.CompilerParams(dimension_semantics=sem, vmem_limit_bytes=V7X_VMEM_LIMIT_BYTES)


def _sigmoid(x):
    return 1.0 / (1.0 + jnp.exp(-x))


def _rms(x, width):
    return x * lax.rsqrt(jnp.sum(x * x, axis=-1, keepdims=True) * (1.0 / width) + EPS)


def _ada_kernel(c_ref, w_ref, b_ref, o_ref):
    c = c_ref[...]
    s = c * _sigmoid(c)
    o_ref[...] = jnp.dot(s, w_ref[...], preferred_element_type=F32, precision=HIGHEST) + b_ref[...]


def _ada(cc, w, b):
    R, D = cc.shape
    N = w.shape[1]
    tn = D
    return pl.pallas_call(
        _ada_kernel,
        out_shape=jax.ShapeDtypeStruct((R, N), F32),
        grid=(N // tn,),
        in_specs=[pl.BlockSpec((R, D), lambda j: (0, 0)),
                  pl.BlockSpec((D, tn), lambda j: (0, j)),
                  pl.BlockSpec((1, tn), lambda j: (0, j))],
        out_specs=pl.BlockSpec((R, tn), lambda j: (0, j)),
        compiler_params=_cparams("arbitrary"),
        name="ada",
    )(cc, w, b)


_TRANSPOSED_KINDS = ("q", "transposed")


def _in_kernel(spec, use_rope, qscale, x_ref, sc_ref, sh_ref, g_ref, w_ref, b_ref, qg_ref, kg_ref,
               bd_ref, *rest):
    if use_rope:
        cos_ref, sin_ref, *outs = rest
    else:
        outs = rest
    x = x_ref[0]
    h = _rms(x, x.shape[-1]) * g_ref[...]
    h = (h * (1.0 + sc_ref[0]) + sh_ref[0]).astype(BF16)
    for (kind, c0, cw), o_ref in zip(spec, outs):
        y = jnp.dot(h, w_ref[:, c0:c0 + cw], preferred_element_type=F32) + b_ref[:, c0:c0 + cw]
        if kind in ("q", "k", "kc"):
            y2 = y * y
            hi = y2.astype(BF16)
            lo = (y2 - hi.astype(F32)).astype(BF16)
            ssq = (jnp.dot(hi, bd_ref[...], preferred_element_type=F32)
                   + jnp.dot(lo, bd_ref[...], preferred_element_type=F32))
            gn = qg_ref if kind == "q" else kg_ref
            y = y * lax.rsqrt(ssq * (1.0 / DA_QK_DIM) + EPS) * gn[...]
            if kind != "kc":
                lane = lax.broadcasted_iota(jnp.int32, y.shape, 1)
                partner = jnp.where((lane % (2 * ROPE_FREQS)) < ROPE_FREQS,
                                    pltpu.roll(y, cw - ROPE_FREQS, 1), pltpu.roll(y, ROPE_FREQS, 1))
                y = y * cos_ref[...] + partner * sin_ref[...]
            if kind == "q":
                y = y * qscale
        elif kind == "sigmoid":
            y = _sigmoid(y)
        if kind in _TRANSPOSED_KINDS:
            y = y.T
        o_ref[0] = y.astype(o_ref.dtype)


def _in_proj(x, sc, sh, g, w, b, qg, kg, bd, rope, spec, tm, qscale):
    B, S, D = x.shape
    N = w.shape[1]
    use_rope = rope is not None
    const2 = lambda i, bb: (0, 0)
    in_specs = [pl.BlockSpec((1, tm, D), lambda i, bb: (bb, i, 0)),
                pl.BlockSpec((1, 1, D), lambda i, bb: (bb, 0, 0)),
                pl.BlockSpec((1, 1, D), lambda i, bb: (bb, 0, 0)),
                pl.BlockSpec((1, D), const2),
                pl.BlockSpec((D, N), const2, pipeline_mode=pl.Buffered(1)),
                pl.BlockSpec((1, N), const2),
                pl.BlockSpec(qg.shape, const2),
                pl.BlockSpec(kg.shape, const2),
                pl.BlockSpec(bd.shape, const2)]
    args = [x, sc, sh, g, w, b, qg, kg, bd]
    if use_rope:
        cw = rope[0].shape[1]
        in_specs += [pl.BlockSpec((tm, cw), lambda i, bb: (i, 0))] * 2
        args += list(rope)
    out_shape = [jax.ShapeDtypeStruct((B, cw_, S) if kind in _TRANSPOSED_KINDS else (B, S, cw_), BF16)
                 for (kind, _, cw_) in spec]
    out_specs = [pl.BlockSpec((1, cw_, tm), lambda i, bb: (bb, 0, i)) if kind in _TRANSPOSED_KINDS
                 else pl.BlockSpec((1, tm, cw_), lambda i, bb: (bb, i, 0)) for (kind, _, cw_) in spec]
    return pl.pallas_call(
        functools.partial(_in_kernel, spec, use_rope, qscale),
        out_shape=out_shape,
        grid=(S // tm, B),
        in_specs=in_specs,
        out_specs=out_specs,
        compiler_params=_cparams("arbitrary", "arbitrary"),
        name="in_proj_rope" if use_rope else "in_proj_ctx",
    )(*args)


def _rope_tables(S, width):
    t = np.arange(S)
    pos = np.stack([t // GRID_W, t % GRID_W], axis=-1).astype(np.float32)
    freqs = (np.float32(ROPE_THETA) ** (-np.arange(ROPE_FREQS, dtype=np.float32) / ROPE_FREQS)).astype(np.float32)
    lane = np.arange(width)
    d = lane % DA_QK_DIM
    axis = d // (2 * ROPE_FREQS)
    half = (d % (2 * ROPE_FREQS)) // ROPE_FREQS
    f = d % ROPE_FREQS
    ang = (pos[:, axis] * freqs[f][None, :]).astype(np.float32)
    cos = np.cos(ang.astype(np.float64)).astype(np.float32)
    sin = np.sin(ang.astype(np.float64)).astype(np.float32)
    sin = np.where(half[None, :] == 0, -sin, sin)
    return jnp.asarray(cos), jnp.asarray(sin)


def _attn_kernel(tq, kc, lam_ref, q_ref, qn_ref, kx_ref, kc_ref, vtx_ref, vtc_ref, g_ref, o_ref,
                 st0, st1, mx0, mx1):
    n_lat = kx_ref.shape[1] // kc
    n_score = n_out = n_lat + kc_ref.shape[1] // kc
    rows_a = kc
    dv = vtx_ref.shape[1]
    ones = jnp.ones((16, kc), BF16)
    lam = lam_ref[0]

    def key_piece(j):
        return kx_ref[0, j * kc:(j + 1) * kc, :] if j < n_lat else kc_ref[0, (j - n_lat) * kc:(j - n_lat + 1) * kc, :]

    def value_piece(j):
        return vtx_ref[0, :, j * kc:(j + 1) * kc] if j < n_lat else vtc_ref[0, :, (j - n_lat) * kc:(j - n_lat + 1) * kc]

    def stacked(qt):
        row = lax.broadcasted_iota(jnp.int32, qt.shape, 0)
        zero = jnp.zeros_like(qt)
        return jnp.concatenate([jnp.where(row < DA_QK_DIM, qt, zero), jnp.where(row >= DA_QK_DIM, qt, zero)],
                               axis=1)

    def score_piece(qs, st, mx, j):
        r0 = j * rows_a
        st[r0:r0 + rows_a, :] = jnp.dot(key_piece(j), qs, preferred_element_type=F32)
        part = jnp.max(st[r0:r0 + rows_a, :].reshape(rows_a // 8, 8, st.shape[-1]), axis=0)
        mx[...] = part if j == 0 else jnp.maximum(mx[...], part)
        return part

    def out_piece(st, m, acc, j):
        e = jnp.exp2(st[j * kc:(j + 1) * kc, :] - m).astype(BF16)
        vt1 = jnp.concatenate([value_piece(j), ones], axis=0)
        d = jnp.dot(vt1, e, preferred_element_type=F32)
        return d if acc is None else acc + d

    def finish(acc, c):
        ot = acc[:dv] * (1.0 / acc[dv:dv + 1])
        o = (ot[:, :tq] - lam * ot[:, tq:]).T
        o = _rms(o, o.shape[-1]) * g_ref[...]
        o_ref[0, c * tq:(c + 1) * tq, :] = o.astype(o_ref.dtype)

    def overlapped(st_cur, mx_cur, q_next, st_next, mx_next, c):
        m = jnp.max(mx_cur[...], axis=0, keepdims=True)
        qs = stacked(q_next)
        acc = None
        p = 0
        parts = []
        for j in range(n_out):
            while p < n_score and p * n_out <= j * n_score:
                parts.append(score_piece(qs, st_next, mx_next, p))
                p += 1
            m_j = m + 0.0 * parts[j - ANCHOR_LAG][0:1, :] if ANCHOR_LAG <= j < len(parts) + ANCHOR_LAG else m
            acc = out_piece(st_cur, m_j, acc, j)
        for p in range(p, n_score):
            score_piece(qs, st_next, mx_next, p)
        finish(acc, c)

    @pl.when(pl.program_id(2) == 0)
    def _():
        qs = stacked(q_ref[0, :, 0:tq])
        for j in range(n_score):
            score_piece(qs, st0, mx0, j)

    overlapped(st0, mx0, q_ref[0, :, tq:2 * tq], st1, mx1, 0)
    overlapped(st1, mx1, qn_ref[0], st0, mx0, 1)


def _attention(lam, qt, kx, kc_, vtx, vtc, g, tq):
    B, W, S = qt.shape
    C = kc_.shape[1]
    Sk = S + C
    H = W // LANES
    tb = 2 * tq
    n_steps = S // tb
    kc = 256 if (S % 256 == 0 and C % 256 == 0) else LANES
    return pl.pallas_call(
        functools.partial(_attn_kernel, tq, kc),
        out_shape=jax.ShapeDtypeStruct((B, S, W), BF16),
        grid=(B, H, n_steps),
        in_specs=[pl.BlockSpec(memory_space=pltpu.SMEM),
                  pl.BlockSpec((1, LANES, tb), lambda b, h, i: (b, h, i)),
                  pl.BlockSpec((1, LANES, tq), lambda b, h, i: (b, h, jnp.minimum(i + 1, n_steps - 1) * 2)),
                  pl.BlockSpec((1, S, LANES), lambda b, h, i: (b, 0, h)),
                  pl.BlockSpec((1, C, LANES), lambda b, h, i: (b, 0, h)),
                  pl.BlockSpec((1, LANES, S), lambda b, h, i: (b, h, 0)),
                  pl.BlockSpec((1, LANES, C), lambda b, h, i: (b, h, 0)),
                  pl.BlockSpec((1, LANES), lambda b, h, i: (0, 0))],
        out_specs=pl.BlockSpec((1, tb, LANES), lambda b, h, i: (b, i, h)),
        scratch_shapes=[pltpu.VMEM((Sk, 2 * tq), F32), pltpu.VMEM((Sk, 2 * tq), F32),
                        pltpu.VMEM((8, 2 * tq), F32), pltpu.VMEM((8, 2 * tq), F32)],
        compiler_params=_cparams("arbitrary", "arbitrary", "arbitrary"),
        name="diff_attn",
    )(lam, qt, qt, kx, kc_, vtx, vtc, g)


def _filter_kernel(z_ref, w1, b1, w2, b2, w3, b3, w4, fr, dec_ref, o_ref):
    dot = functools.partial(jnp.dot, preferred_element_type=F32, precision=HIGHEST)
    f = fr[...]
    h = jnp.sin(f * (dot(z_ref[...], w1[...]) + b1[...]))
    h = jnp.sin(f * (dot(h, w2[...]) + b2[...]))
    h = jnp.sin(f * (dot(h, w3[...]) + b3[...]))
    taps = dot(h, w4[...]) * dec_ref[...]
    tl, W = o_ref.shape[1:]
    o_ref[0] = taps[:, :W]
    pos = lax.broadcasted_iota(jnp.int32, (tl, W), 0) + pl.program_id(0) * tl
    o_ref[1] = jnp.where(pos == 0, 0.0, taps[:, W:])


def _implicit_filter(z, w1, b1, w2, b2, w3, b3, w4, fr, dec, tl):
    L = z.shape[0]
    N = w4.shape[1]
    full = lambda a: pl.BlockSpec(a.shape, lambda i: (0, 0))
    return pl.pallas_call(
        _filter_kernel,
        out_shape=jax.ShapeDtypeStruct((2, L, N // 2), F32),
        grid=(L // tl,),
        in_specs=[pl.BlockSpec((tl, z.shape[1]), lambda i: (i, 0)),
                  full(w1), full(b1), full(w2), full(b2), full(w3), full(b3), full(w4), full(fr),
                  pl.BlockSpec((tl, N), lambda i: (i, 0))],
        out_specs=pl.BlockSpec((2, tl, N // 2), lambda i: (0, i, 0)),
        compiler_params=_cparams("arbitrary"),
        name="hyena_filter",
    )(z, w1, b1, w2, b2, w3, b3, w4, fr, dec)


def _conv3(a, w_ref, b_ref):
    L = a.shape[0]
    row = lax.broadcasted_iota(jnp.int32, a.shape, 0)
    prev = jnp.where(row == 0, 0.0, pltpu.roll(a, 1, 0))
    nxt = jnp.where(row == L - 1, 0.0, pltpu.roll(a, L - 1, 0))
    return prev * w_ref[0:1, :] + a * w_ref[1:2, :] + nxt * w_ref[2:3, :] + b_ref[...]


def _gate_kernel(x1_ref, v_ref, w1_ref, wv_ref, b1_ref, bv_ref, u_ref):
    u = _conv3(x1_ref[0].astype(F32), w1_ref, b1_ref) * _conv3(v_ref[0].astype(F32), wv_ref, bv_ref)
    u_ref[0] = u.astype(u_ref.dtype)


def _hyena_gate(hy, conv_w, conv_b):
    B, L, W3 = hy.shape
    W = W3 // 3
    nw = W // LANES
    act = lambda off: pl.BlockSpec((1, L, LANES), lambda b, j: (b, 0, j + off))
    cw = lambda off: pl.BlockSpec((3, LANES), lambda b, j: (0, j + off))
    cb = lambda off: pl.BlockSpec((1, LANES), lambda b, j: (0, j + off))
    return pl.pallas_call(
        _gate_kernel,
        out_shape=jax.ShapeDtypeStruct((B, L, W), BF16),
        grid=(B, nw),
        in_specs=[act(nw), act(2 * nw), cw(nw), cw(2 * nw), cb(nw), cb(2 * nw)],
        out_specs=pl.BlockSpec((1, L, LANES), lambda b, j: (b, 0, j)),
        compiler_params=_cparams("arbitrary", "arbitrary"),
        name="hyena_gate",
    )(hy, hy, conv_w, conv_w, conv_b, conv_b)


def _outer_dft_kernel(precision, f_ref, x_ref, o_ref):
    o_ref[0] = jnp.dot(f_ref[...], x_ref[0], preferred_element_type=F32,
                       precision=precision).astype(o_ref.dtype)


def _outer_dft(f, x, out_dtype, tn, name):
    B, K, N = x.shape
    M = f.shape[0]
    precision = HIGHEST if x.dtype == F32 else None
    return pl.pallas_call(
        functools.partial(_outer_dft_kernel, precision),
        out_shape=jax.ShapeDtypeStruct((B, M, N), out_dtype),
        grid=(B, N // tn),
        in_specs=[pl.BlockSpec((M, K), lambda b, j: (0, 0)),
                  pl.BlockSpec((1, K, tn), lambda b, j: (b, 0, j))],
        out_specs=pl.BlockSpec((1, M, tn), lambda b, j: (b, 0, j)),
        compiler_params=_cparams("arbitrary", "arbitrary"),
        name=name,
    )(f, x)


def _twiddle(re, im, tr, ti, conj):
    if conj:
        return re * tr + im * ti, im * tr - re * ti
    return re * tr - im * ti, re * ti + im * tr


def _inner_spectrum_kernel(kb, a_ref, tr_ref, ti_ref, f_ref, o_ref):
    reps = a_ref.shape[-1] // LANES
    for j in range(kb):
        tr = jnp.concatenate([tr_ref[j]] * reps, axis=-1)
        ti = jnp.concatenate([ti_ref[j]] * reps, axis=-1)
        xs = []
        for s in range(2):
            re, im = _twiddle(a_ref[s, 0, j], a_ref[s, 1, j], tr, ti, False)
            xs.append(jnp.dot(f_ref[...], jnp.concatenate([re, im], axis=0), preferred_element_type=F32,
                              precision=HIGHEST))
        n2 = xs[0].shape[0] // 2
        o_ref[0, 0, j] = xs[0][:n2] + xs[1][:n2]
        o_ref[0, 1, j] = xs[0][n2:] - xs[1][n2:]


def _inner_conv_kernel(kb, a_ref, kf_ref, tr_ref, ti_ref, f_ref, fi_ref, o_ref):
    reps = a_ref.shape[-1] // LANES
    for j in range(kb):
        tr = jnp.concatenate([tr_ref[j]] * reps, axis=-1)
        ti = jnp.concatenate([ti_ref[j]] * reps, axis=-1)
        re, im = _twiddle(a_ref[0, 0, j].astype(F32), a_ref[0, 1, j].astype(F32), tr, ti, False)
        n2 = re.shape[0]
        x = jnp.dot(f_ref[...], jnp.concatenate([re, im], axis=0).astype(BF16), preferred_element_type=F32)
        xr, xi = x[:n2], x[n2:]
        kr, ki = kf_ref[0, 0, j], kf_ref[0, 1, j]
        yr = xr * kr - xi * ki
        yi = xr * ki + xi * kr
        y = jnp.dot(fi_ref[...], jnp.concatenate([yr, yi], axis=0).astype(BF16), preferred_element_type=F32)
        re, im = _twiddle(y[:n2], y[n2:], tr, ti, True)
        o_ref[0, 0, j] = re.astype(o_ref.dtype)
        o_ref[0, 1, j] = im.astype(o_ref.dtype)


def _inner_stage(a5, kf5, tr, ti, f2, f2i, kb):
    B, _, N1, N2, W = a5.shape
    blk = lambda: pl.BlockSpec((1, 2, kb, N2, W), lambda g, b: (b, 0, g, 0, 0))
    tw = pl.BlockSpec((kb, N2, LANES), lambda g, b: (g, 0, 0))
    mat = pl.BlockSpec((2 * N2, 2 * N2), lambda g, b: (0, 0))
    if kf5 is None:
        kern = functools.partial(_inner_spectrum_kernel, kb)
        pair = pl.BlockSpec((2, 2, kb, N2, W), lambda g, b: (0, 0, g, 0, 0))
        in_specs, args, out_dtype, name = [pair, tw, tw, mat], (a5, tr, ti, f2), F32, "hyena_filter_spectrum"
        B = 1
    else:
        kern = functools.partial(_inner_conv_kernel, kb)
        kf_spec = pl.BlockSpec((1, 2, kb, N2, W), lambda g, b: (0, 0, g, 0, 0))
        in_specs, args, out_dtype, name = ([blk(), kf_spec, tw, tw, mat, mat], (a5, kf5, tr, ti, f2, f2i),
                                           BF16, "hyena_inner_conv")
    return pl.pallas_call(
        kern,
        out_shape=jax.ShapeDtypeStruct((B,) + a5.shape[1:], out_dtype),
        grid=(N1 // kb, B),
        in_specs=in_specs,
        out_specs=blk(),
        compiler_params=_cparams("arbitrary", "arbitrary"),
        name=name,
    )(*args)


def _hyena_out_kernel(x0_ref, w0_ref, b0_ref, y_ref, u_ref, hb_ref, o_ref):
    x0 = _conv3(x0_ref[0].astype(F32), w0_ref, b0_ref)
    o_ref[0] = (x0 * (y_ref[0] + u_ref[0].astype(F32) * hb_ref[...])).astype(o_ref.dtype)


def _hyena_out(hy, conv_w, conv_b, y, u, hbias):
    B, L, W = u.shape
    nw = W // LANES
    blk = pl.BlockSpec((1, L, LANES), lambda b, j: (b, 0, j))
    vec = lambda r: pl.BlockSpec((r, LANES), lambda b, j: (0, j))
    return pl.pallas_call(
        _hyena_out_kernel,
        out_shape=jax.ShapeDtypeStruct((B, L, W), BF16),
        grid=(B, nw),
        in_specs=[blk, vec(3), vec(1), blk, blk, vec(1)],
        out_specs=blk,
        compiler_params=_cparams("arbitrary", "arbitrary"),
        name="hyena_out",
    )(hy, conv_w, conv_b, y, u, hbias)


def _dft_constants(L):
    N = 2 * L
    N2 = FFT_N2
    N1 = N // N2
    k1 = np.arange(N1, dtype=np.float64)
    th1 = 2.0 * np.pi * np.outer(k1, k1) / N1
    fwd = np.concatenate([np.cos(th1), -np.sin(th1)], axis=0)
    inv = np.concatenate([np.cos(th1), -np.sin(th1)], axis=1)[: N1 // 2] / N
    n2 = np.arange(N2, dtype=np.float64)
    tw = 2.0 * np.pi * np.outer(k1, n2) / N
    tr = np.repeat(np.cos(tw)[:, :, None], LANES, axis=2)
    ti = np.repeat(-np.sin(tw)[:, :, None], LANES, axis=2)
    th2 = 2.0 * np.pi * np.outer(n2, n2) / N2
    c2, s2 = np.cos(th2), np.sin(th2)
    f2 = np.block([[c2, s2], [-s2, c2]])
    f2i = np.block([[c2, -s2], [s2, c2]])
    f = lambda a: jnp.asarray(a, F32)
    return dict(N1=N1, N2=N2, fwd=f(fwd), inv=f(inv), tr=f(tr), ti=f(ti), f2=f(f2), f2i=f(f2i))


def _filter_features(L, emb_dim):
    bands = (emb_dim - 1) // 2
    t = np.linspace(0.0, 1.0, L, dtype=np.float32)[:, None]
    w = (np.float32(2.0 * math.pi) * np.arange(L, dtype=np.float32)[:, None] / np.float32(L)).astype(np.float32)
    f = np.linspace(1e-4, bands - 1, bands, dtype=np.float32)
    fw = (f * w).astype(np.float32)
    z = np.concatenate([t, np.cos(fw.astype(np.float64)).astype(np.float32),
                        -np.sin(fw.astype(np.float64)).astype(np.float32)], axis=-1)
    return jnp.asarray(z), t


def _decay_window(t, W):
    deltas = np.abs(np.linspace(math.log(HY_DECAY_TARGET) / HY_SLOW_DECAY,
                                math.log(HY_DECAY_TARGET) / HY_FAST_DECAY, W, dtype=np.float32))
    dec = np.exp((-t * deltas).astype(np.float32).astype(np.float64)).astype(np.float32)
    return jnp.asarray(np.concatenate([dec, dec], axis=1))


def _hyena(hy, conv_w, conv_b, fw1, fb1, fw2, fb2, fw3, fb3, fw4, ffreq, hbias):
    B, L, W3 = hy.shape
    W = W3 // 3
    C = _dft_constants(L)
    N1, N2 = C["N1"], C["N2"]
    row = lambda a: a.reshape(1, -1)
    z, t = _filter_features(L, fw1.shape[0])
    kpad = LANES - fw1.shape[0]
    z = jnp.pad(z, ((0, 0), (0, kpad)))
    taps = _implicit_filter(z, jnp.pad(fw1, ((0, kpad), (0, 0))), row(fb1), fw2, row(fb2), fw3, row(fb3), fw4,
                            row(ffreq),
                            _decay_window(t, W), tl=min(L, 512))
    lanes = N2 * W
    tn = min(lanes, 8192)
    kf = _outer_dft(C["fwd"][:, : N1 // 2], taps.reshape(2, N1 // 2, lanes), F32, tn, "hyena_filter_outer")
    kf5 = _inner_stage(kf.reshape(2, 2, N1, N2, W), None, C["tr"], C["ti"], C["f2"], None, kb=min(N1, 2))
    u = _hyena_gate(hy, conv_w, row(conv_b))
    a = _outer_dft(C["fwd"][:, : N1 // 2].astype(BF16), u.reshape(B, N1 // 2, lanes), BF16, tn,
                   "hyena_outer_fwd")
    bb = _inner_stage(a.reshape(B, 2, N1, N2, W), kf5, C["tr"], C["ti"], C["f2"].astype(BF16),
                      C["f2i"].astype(BF16), kb=min(N1, 8))
    y = _outer_dft(C["inv"].astype(BF16), bb.reshape(B, 2 * N1, lanes), F32, tn, "hyena_outer_inv")
    return _hyena_out(hy, conv_w, row(conv_b), y.reshape(B, L, W), u, row(hbias))


ROUTE_ROWS = 16


def _route_tile(h, rw_ref, rb_ref, tri_ref, carry_ref):
    tm = h.shape[0]
    E = rw_ref.shape[0] // 2
    nt = (((1,), (1,)), ((), ()))
    h_hi = h.astype(BF16)
    h_lo = (h - h_hi.astype(F32)).astype(BF16)
    a = lax.dot_general(rw_ref[...], h_hi, nt, preferred_element_type=F32)
    b = lax.dot_general(rw_ref[0:E, :], h_lo, nt, preferred_element_type=F32)
    reps = tm // LANES
    logits = a[:E] + a[E:] + b + jnp.concatenate([rb_ref[...]] * reps, axis=1)
    row = lax.broadcasted_iota(jnp.int32, (E, tm), 0)
    v = logits
    ids, vals = [], []
    for _ in range(TOP_K):
        m = jnp.max(v, axis=0, keepdims=True)
        idx = jnp.min(jnp.where(v == m, row, E), axis=0, keepdims=True)
        ids.append(idx)
        vals.append(m)
        v = jnp.where(row == idx, -jnp.inf, v)
    ex = [jnp.exp(vk - vals[0]) for vk in vals]
    den = ex[0]
    for t in ex[1:]:
        den = den + t
    hot = [row == idx for idx in ids]
    member = jnp.zeros((E, tm), F32)
    for hk in hot:
        member = jnp.where(hk, 1.0, member)
    member = member.astype(BF16)
    carry = carry_ref[...]
    before = (jnp.dot(member, tri_ref[...], preferred_element_type=F32)
              + jnp.concatenate([carry] * reps, axis=1))
    ranks = [jnp.sum(jnp.where(hk, before, 0.0), axis=0, keepdims=True) for hk in hot]
    carry_ref[...] = carry + jnp.dot(member, jnp.ones((tm, LANES), BF16), preferred_element_type=F32)
    out_row = lax.broadcasted_iota(jnp.int32, (ROUTE_ROWS, tm), 0)
    rec = jnp.zeros((ROUTE_ROWS, tm), F32)
    for k in range(TOP_K):
        rec = jnp.where(out_row == k, ids[k].astype(F32), rec)
        rec = jnp.where(out_row == TOP_K + k, ex[k] / den, rec)
        rec = jnp.where(out_row == 2 * TOP_K + k, ranks[k], rec)
    return rec


def _merge_kernel(x_ref, oa_ref, ob_ref, gt_ref, g1_ref, sc_ref, sh_ref, ng_ref, wa_ref, wb_ref, wo_ref,
                  rw_ref, rb_ref, tri_ref, xo_ref, h_ref, rec_ref, cnt_ref, before_ref, carry_ref):
    @pl.when((pl.program_id(0) == 0) & (pl.program_id(1) == 0))
    def _():
        carry_ref[...] = jnp.zeros_like(carry_ref)

    @pl.when(pl.program_id(1) == 0)
    def _():
        before_ref[0] = carry_ref[...]

    D = x_ref.shape[-1]
    gt = gt_ref[0]
    m = (gt[:, :D].astype(F32) * jnp.dot(oa_ref[0], wa_ref[...], preferred_element_type=F32)
         + gt[:, D:].astype(F32) * jnp.dot(ob_ref[0], wb_ref[...], preferred_element_type=F32))
    y = jnp.dot(m.astype(BF16), wo_ref[...], preferred_element_type=F32)
    xn = x_ref[0] + g1_ref[0] * y
    xo_ref[0] = xn
    h = _rms(xn, D) * ng_ref[...]
    h = h * (1.0 + sc_ref[0]) + sh_ref[0]
    h_ref[0] = h.astype(h_ref.dtype)
    rec_ref[0] = _route_tile(h, rw_ref, rb_ref, tri_ref, carry_ref)
    cnt_ref[...] = carry_ref[...]


def _merge(x, oa, ob, gates, g1, sc2, sh2, ng, wa, wb, wo, rw, rb, tm):
    B, S, D = x.shape
    E = rw.shape[1]
    rw_hi = rw.T.astype(BF16)
    rw2 = jnp.concatenate([rw_hi, (rw.T - rw_hi.astype(F32)).astype(BF16)], axis=0)
    rb2 = jnp.broadcast_to(rb.reshape(E, 1), (E, LANES))
    tri = jnp.asarray(np.triu(np.ones((tm, tm), np.float32), 1), BF16)
    act = lambda w: pl.BlockSpec((1, tm, w), lambda b, i: (b, i, 0))
    mod = pl.BlockSpec((1, 1, D), lambda b, i: (b, 0, 0))
    full = lambda a: pl.BlockSpec(a.shape, lambda b, i: (0, 0))
    return pl.pallas_call(
        _merge_kernel,
        out_shape=[jax.ShapeDtypeStruct((B, S, D), F32), jax.ShapeDtypeStruct((B, S, D), BF16),
                   jax.ShapeDtypeStruct((B, ROUTE_ROWS, S), F32), jax.ShapeDtypeStruct((E, LANES), F32),
                   jax.ShapeDtypeStruct((B, E, LANES), F32)],
        grid=(B, S // tm),
        in_specs=[act(D), act(oa.shape[-1]), act(ob.shape[-1]), act(2 * D), mod, mod, mod, full(ng),
                  full(wa), full(wb), full(wo), full(rw2), full(rb2), full(tri)],
        out_specs=[act(D), act(D), pl.BlockSpec((1, ROUTE_ROWS, tm), lambda b, i: (b, 0, i)),
                   pl.BlockSpec((E, LANES), lambda b, i: (0, 0)),
                   pl.BlockSpec((1, E, LANES), lambda b, i: (b, 0, 0))],
        scratch_shapes=[pltpu.VMEM((E, LANES), F32)],
        compiler_params=_cparams("arbitrary", "arbitrary"),
        name="merge_norm_router",
    )(x, oa, ob, gates, g1, sc2, sh2, ng, wa, wb, wo, rw2, rb2, tri)


def _moe_kernel(be_ref, nu_ref, x_ref, w1_ref, b1_ref, w2_ref, b2_ref, o_ref, w1b_ref, w2b_ref):
    i = pl.program_id(0)

    @pl.when(i < nu_ref[0])
    def _():
        @pl.when((i == 0) | (be_ref[i] != be_ref[jnp.maximum(i - 1, 0)]))
        def _():
            w1b_ref[...] = w1_ref[0].astype(BF16)
            w2b_ref[...] = w2_ref[0].astype(BF16)

        F = w2_ref.shape[1]
        gu = jnp.dot(x_ref[...], w1b_ref[...], preferred_element_type=F32) + b1_ref[0]
        gate = jnp.minimum(gu[:, :F], SWIGLU_LIMIT)
        up = jnp.clip(gu[:, F:], -SWIGLU_LIMIT, SWIGLU_LIMIT)
        act = (up + 1.0) * gate * _sigmoid(SWIGLU_ALPHA * gate)
        y = jnp.dot(act.astype(BF16), w2b_ref[...], preferred_element_type=F32) + b2_ref[0]
        o_ref[...] = y.astype(o_ref.dtype)


def _moe_experts(block_e, n_used, xg, w1, b1, w2, b2, bm):
    R, D = xg.shape
    E, _, F2 = w1.shape
    F = w2.shape[1]
    grid_spec = pltpu.PrefetchScalarGridSpec(
        num_scalar_prefetch=2,
        grid=(R // bm,),
        in_specs=[pl.BlockSpec((bm, D), lambda i, be, nu: (i, 0)),
                  pl.BlockSpec((1, D, F2), lambda i, be, nu: (be[i], 0, 0)),
                  pl.BlockSpec((1, 1, F2), lambda i, be, nu: (be[i], 0, 0)),
                  pl.BlockSpec((1, F, D), lambda i, be, nu: (be[i], 0, 0)),
                  pl.BlockSpec((1, 1, D), lambda i, be, nu: (be[i], 0, 0))],
        out_specs=pl.BlockSpec((bm, D), lambda i, be, nu: (i, 0)),
        scratch_shapes=[pltpu.VMEM((D, F2), BF16), pltpu.VMEM((F, D), BF16)],
    )
    return pl.pallas_call(
        _moe_kernel,
        out_shape=jax.ShapeDtypeStruct((R, D), BF16),
        grid_spec=grid_spec,
        compiler_params=pltpu.CompilerParams(dimension_semantics=("arbitrary",),
                                             vmem_limit_bytes=V7X_VMEM_LIMIT_MOE_BYTES),
        name="moe_experts",
    )(block_e, n_used, xg, w1, b1.reshape(E, 1, F2), w2, b2.reshape(E, 1, D))


def _combine_kernel(nb, x_ref, g_ref, *refs):
    o_ref = refs[-1]
    b = pl.program_id(0)
    for grp in range((len(refs) - 1) // 2):
        y_ref, w_ref = refs[2 * grp:2 * grp + 2]

        @pl.when((b >= grp * nb) & (b < (grp + 1) * nb))
        def _():
            w = w_ref[0]
            acc = w[:, 0:1] * y_ref[0, 0].astype(F32)
            for k in range(1, y_ref.shape[0]):
                acc = acc + w[:, k:k + 1] * y_ref[k, 0].astype(F32)
            o_ref[0] = x_ref[0] + g_ref[0] * acc


def _combine(x, ygs, wtss, g2, tm):
    B, S, D = x.shape
    K, nb = ygs[0].shape[:2]
    in_specs = [pl.BlockSpec((1, tm, D), lambda b, i: (b, i, 0)), pl.BlockSpec((1, 1, D), lambda b, i: (b, 0, 0))]
    args = [x, g2]

    def group_block(grp, b, i):
        mine = (b >= grp * nb) & (b < (grp + 1) * nb)
        return jnp.where(mine, b - grp * nb, 0), jnp.where(mine, i, 0)

    for grp in range(len(ygs)):
        at = functools.partial(group_block, grp)
        in_specs += [pl.BlockSpec((K, 1, tm, D), lambda b, i, at=at: (0, *at(b, i), 0)),
                     pl.BlockSpec((1, tm, K), lambda b, i, at=at: (*at(b, i), 0))]
        args += [ygs[grp], wtss[grp]]
    return pl.pallas_call(
        functools.partial(_combine_kernel, nb),
        out_shape=jax.ShapeDtypeStruct((B, S, D), F32),
        grid=(B, S // tm),
        in_specs=in_specs,
        out_specs=pl.BlockSpec((1, tm, D), lambda b, i: (b, i, 0)),
        compiler_params=_cparams("arbitrary", "arbitrary"),
        name="moe_combine",
    )(*args)


def _route(rec, counts, before, tok0, bm):
    B, _, S = rec.shape
    T = B * S
    E = counts.shape[0]
    I32 = jnp.int32
    experts = jnp.arange(E, dtype=I32)
    pick = lambda table, idx: jnp.sum(jnp.where(idx[..., None] == experts, table, 0), axis=-1)
    rec = jnp.swapaxes(rec, 1, 2).reshape(T, ROUTE_ROWS)
    top_i = rec[:, 0:TOP_K].astype(I32)
    wts = rec[:, TOP_K:2 * TOP_K]
    rank = rec[:, 2 * TOP_K:3 * TOP_K].astype(I32) - pick(before.astype(I32), top_i)
    counts = counts.astype(I32)
    A = T * TOP_K
    padded = (counts + bm - 1) // bm * bm
    pend = jnp.cumsum(padded)
    pstart = pend - padded
    start = jnp.cumsum(counts) - counts
    pos = pick(pstart, top_i) + rank
    keys = top_i.reshape(-1) * A + jnp.arange(A, dtype=I32)
    order = jnp.sort(keys) % A
    n_rows = A + E * bm
    n_blocks = n_rows // bm
    blk_start = jnp.arange(n_blocks, dtype=I32) * bm
    block_e = jnp.minimum(jnp.sum((pend[None, :] <= blk_start[:, None]).astype(I32), axis=1), E - 1)
    off = blk_start[:, None] + jnp.arange(bm, dtype=I32)[None, :] - pick(pstart, block_e)[:, None]
    valid = (off >= 0) & (off < pick(counts, block_e)[:, None])
    src = jnp.clip(pick(start, block_e)[:, None] + off, 0, A - 1)
    sorted_tok = order.at[src.reshape(-1)].get(mode="promise_in_bounds").reshape(n_blocks, bm) // TOP_K
    row_tok = jnp.where(valid, sorted_tok, (blk_start[:, None] + jnp.arange(bm, dtype=I32)[None, :]) % T)
    n_used = (pend[-1:] // bm).astype(I32)
    return wts, row_tok.reshape(-1) + tok0, pos, block_e.astype(I32), n_used


def kernel(x, c, ctx, c_ctx, ada_w, ada_b, norm1_g, norm2_g, w_in, b_in, q_norm_g, k_norm_g, lambda_q1, lambda_k1, lambda_q2, lambda_k2, subln_g, conv_w, conv_b, filt_w1, filt_b1, filt_w2, filt_b2, filt_w3, filt_b3, filt_w4, filt_freq, hyena_bias, w_up_a, w_up_b, w_out, router_w, router_b, exp_w1, exp_b1, exp_w2, exp_b2):
    B, S, D = x.shape
    depth = ada_w.shape[0]
    assert depth == 1, "context-stream update between layers is not implemented"
    l = 0
    QC = DA_HEADS * 2 * DA_QK_DIM
    W = D // 2
    row = lambda a: a.reshape(1, -1)

    lambda_init = 0.8 - 0.6 * math.exp(-0.3 * l)
    lam = (jnp.exp(jnp.sum(lambda_q1[l] * lambda_k1[l])) - jnp.exp(jnp.sum(lambda_q2[l] * lambda_k2[l]))
           + lambda_init).reshape(1).astype(F32)

    R = -(-(B + 1) // 8) * 8
    cc = jnp.concatenate([c, c_ctx[None, :], jnp.zeros((R - B - 1, D), F32)], axis=0)
    mods = _ada(cc, ada_w[l], row(ada_b[l]))
    sh1, sc1, g1, sh2, sc2, g2 = [mods[:B, i * D:(i + 1) * D].reshape(B, 1, D) for i in range(6)]
    sh1c, sc1c = [jnp.broadcast_to(mods[B, i * D:(i + 1) * D].reshape(1, 1, D), (B, 1, D)) for i in range(2)]

    w_in_b = w_in[l].astype(BF16)
    qg = row(jnp.tile(q_norm_g[l], QC // DA_QK_DIM))
    kg = row(jnp.tile(k_norm_g[l], QC // DA_QK_DIM))
    grp = np.arange(QC) // DA_QK_DIM
    bd = jnp.asarray(grp[:, None] == grp[None, :], BF16)
    qscale = DA_QK_DIM ** -0.5 * math.log2(math.e)
    spec = (("q", 0, QC), ("k", QC, QC), ("transposed", 2 * QC, QC), ("plain", 3 * QC, 3 * W),
            ("sigmoid", 3 * QC + 3 * W, 2 * D))
    qtx, kx, vtx, hy, gates = _in_proj(x, sc1, sh1, row(norm1_g[l]), w_in_b, row(b_in[l]), qg, kg, bd,
                                       _rope_tables(S, QC), spec, tm=min(S, 512), qscale=qscale)
    spec_c = (("kc", 0, QC), ("transposed", QC, QC))
    kc, vtc = _in_proj(ctx, sc1c, sh1c, row(norm1_g[l]), w_in_b[:, QC:3 * QC], row(b_in[l][QC:3 * QC]), qg,
                       kg, bd, None, spec_c, tm=ctx.shape[1], qscale=qscale)

    oa = _attention(lam, qtx, kx, kc, vtx, vtc, row(subln_g[l] * (1.0 - lambda_init)), tq=min(S // 2, 256))
    ob = _hyena(hy, conv_w[l], conv_b[l], filt_w1[l], filt_b1[l], filt_w2[l], filt_b2[l], filt_w3[l],
                filt_b3[l], filt_w4[l], filt_freq[l], hyena_bias[l])
    x_new, h2, rec, counts, before = _merge(x, oa, ob, gates, g1, sc2, sh2, row(norm2_g[l]),
                                            w_up_a[l].astype(BF16), w_up_b[l].astype(BF16), w_out[l].astype(BF16),
                                            router_w[l], row(router_b[l]), tm=min(S, 512))

    groups = MOE_BATCH_GROUPS if B % MOE_BATCH_GROUPS == 0 else 1
    nb = B // groups
    bm = EXPERT_BLOCK
    edges = jnp.concatenate([before[:, :, 0], counts[None, :, 0]], axis=0)
    ygs, wtss = [], []
    for grp in range(groups):
        b0 = grp * nb
        wts, row_tok, pos, block_e, n_used = _route(rec[b0:b0 + nb], edges[b0 + nb] - edges[b0], edges[b0],
                                                    b0 * S, bm)
        xg = h2.reshape(B * S, D).at[row_tok].get(mode="promise_in_bounds")
        yb = _moe_experts(block_e, n_used, xg, exp_w1[l], exp_b1[l], exp_w2[l], exp_b2[l], bm)
        ygs.append(yb.at[pos.T].get(mode="promise_in_bounds").reshape(TOP_K, nb, S, D))
        wtss.append(wts.reshape(nb, S, TOP_K))
    return _combine(x_new, ygs, wtss, g2, tm=min(S, 512))
```

```python
import functools
import math

import jax
import jax.numpy as jnp
import numpy as np
from jax import lax
from jax.experimental import pallas as pl
from jax.experimental.pallas import tpu as pltpu

F32 = jnp.float32
BF16 = jnp.bfloat16
HIGHEST = lax.Precision.HIGHEST

GRID_W = 64
DA_HEADS = 4
DA_QK_DIM = 64
DA_V_DIM = 2 * DA_QK_DIM
ROPE_THETA = 10000.0
ROPE_FREQS = DA_QK_DIM // 4
HY_DECAY_TARGET = 1e-2
HY_FAST_DECAY = 0.3
HY_SLOW_DECAY = 1.5
TOP_K = 4
SWIGLU_LIMIT = 7.0
SWIGLU_ALPHA = 1.702
EPS = 1e-6

LANES = 128
V7X_VMEM_LIMIT_BYTES = 48 * 1024 * 1024
V7X_VMEM_LIMIT_MOE_BYTES = 56 * 1024 * 1024
FFT_N2 = 128
ANCHOR_LAG = 2
EXPERT_BLOCK = 512
MOE_BATCH_GROUPS = 1


def _cparams(*sem):
    return pltpu.CompilerParams(dimension_semantics=sem, vmem_limit_bytes=V7X_VMEM_LIMIT_BYTES)


def _sigmoid(x):
    return 1.0 / (1.0 + jnp.exp(-x))


def _rms(x, width):
    return x * lax.rsqrt(jnp.sum(x * x, axis=-1, keepdims=True) * (1.0 / width) + EPS)


def _ada_kernel(c_ref, w_ref, b_ref, o_ref):
    c = c_ref[...]
    s = c * _sigmoid(c)
    o_ref[...] = jnp.dot(s, w_ref[...], preferred_element_type=F32, precision=HIGHEST) + b_ref[...]


def _ada(cc, w, b):
    R, D = cc.shape
    N = w.shape[1]
    tn = D
    return pl.pallas_call(
        _ada_kernel,
        out_shape=jax.ShapeDtypeStruct((R, N), F32),
        grid=(N // tn,),
        in_specs=[pl.BlockSpec((R, D), lambda j: (0, 0)),
                  pl.BlockSpec((D, tn), lambda j: (0, j)),
                  pl.BlockSpec((1, tn), lambda j: (0, j))],
        out_specs=pl.BlockSpec((R, tn), lambda j: (0, j)),
        compiler_params=_cparams("arbitrary"),
        name="ada",
    )(cc, w, b)


_TRANSPOSED_KINDS = ("q", "transposed")


def _in_kernel(spec, use_rope, qscale, x_ref, sc_ref, sh_ref, g_ref, w_ref, b_ref, qg_ref, kg_ref,
               bd_ref, *rest):
    if use_rope:
        cos_ref, sin_ref, *outs = rest
    else:
        outs = rest
    x = x_ref[0]
    h = _rms(x, x.shape[-1]) * g_ref[...]
    h = (h * (1.0 + sc_ref[0]) + sh_ref[0]).astype(BF16)
    for (kind, c0, cw), o_ref in zip(spec, outs):
        y = jnp.dot(h, w_ref[:, c0:c0 + cw], preferred_element_type=F32) + b_ref[:, c0:c0 + cw]
        if kind in ("q", "k", "kc"):
            ssq = jnp.dot((y * y).astype(BF16), bd_ref[...], preferred_element_type=F32)
            gn = qg_ref if kind == "q" else kg_ref
            y = y * lax.rsqrt(ssq * (1.0 / DA_QK_DIM) + EPS) * gn[...]
            if kind != "kc":
                lane = lax.broadcasted_iota(jnp.int32, y.shape, 1)
                partner = jnp.where((lane % (2 * ROPE_FREQS)) < ROPE_FREQS,
                                    pltpu.roll(y, cw - ROPE_FREQS, 1), pltpu.roll(y, ROPE_FREQS, 1))
                y = y * cos_ref[...] + partner * sin_ref[...]
            if kind == "q":
                y = y * qscale
        elif kind == "sigmoid":
            y = _sigmoid(y)
        if kind in _TRANSPOSED_KINDS:
            y = y.T
        o_ref[0] = y.astype(o_ref.dtype)


def _in_proj(x, sc, sh, g, w, b, qg, kg, bd, rope, spec, tm, qscale):
    B, S, D = x.shape
    N = w.shape[1]
    use_rope = rope is not None
    const2 = lambda i, bb: (0, 0)
    in_specs = [pl.BlockSpec((1, tm, D), lambda i, bb: (bb, i, 0)),
                pl.BlockSpec((1, 1, D), lambda i, bb: (bb, 0, 0)),
                pl.BlockSpec((1, 1, D), lambda i, bb: (bb, 0, 0)),
                pl.BlockSpec((1, D), const2),
                pl.BlockSpec((D, N), const2, pipeline_mode=pl.Buffered(1)),
                pl.BlockSpec((1, N), const2),
                pl.BlockSpec(qg.shape, const2),
                pl.BlockSpec(kg.shape, const2),
                pl.BlockSpec(bd.shape, const2)]
    args = [x, sc, sh, g, w, b, qg, kg, bd]
    if use_rope:
        cw = rope[0].shape[1]
        in_specs += [pl.BlockSpec((tm, cw), lambda i, bb: (i, 0))] * 2
        args += list(rope)
    out_shape = [jax.ShapeDtypeStruct((B, cw_, S) if kind in _TRANSPOSED_KINDS else (B, S, cw_), BF16)
                 for (kind, _, cw_) in spec]
    out_specs = [pl.BlockSpec((1, cw_, tm), lambda i, bb: (bb, 0, i)) if kind in _TRANSPOSED_KINDS
                 else pl.BlockSpec((1, tm, cw_), lambda i, bb: (bb, i, 0)) for (kind, _, cw_) in spec]
    return pl.pallas_call(
        functools.partial(_in_kernel, spec, use_rope, qscale),
        out_shape=out_shape,
        grid=(S // tm, B),
        in_specs=in_specs,
        out_specs=out_specs,
        compiler_params=_cparams("arbitrary", "arbitrary"),
        name="in_proj_rope" if use_rope else "in_proj_ctx",
    )(*args)


def _rope_tables(S, width):
    t = np.arange(S)
    pos = np.stack([t // GRID_W, t % GRID_W], axis=-1).astype(np.float32)
    freqs = (np.float32(ROPE_THETA) ** (-np.arange(ROPE_FREQS, dtype=np.float32) / ROPE_FREQS)).astype(np.float32)
    lane = np.arange(width)
    d = lane % DA_QK_DIM
    axis = d // (2 * ROPE_FREQS)
    half = (d % (2 * ROPE_FREQS)) // ROPE_FREQS
    f = d % ROPE_FREQS
    ang = (pos[:, axis] * freqs[f][None, :]).astype(np.float32)
    cos = np.cos(ang.astype(np.float64)).astype(np.float32)
    sin = np.sin(ang.astype(np.float64)).astype(np.float32)
    sin = np.where(half[None, :] == 0, -sin, sin)
    return jnp.asarray(cos), jnp.asarray(sin)


def _attn_kernel(tq, kc, lam_ref, q_ref, qn_ref, kx_ref, kc_ref, vtx_ref, vtc_ref, g_ref, o_ref,
                 st0, st1, mx0, mx1):
    n_lat = kx_ref.shape[1] // kc
    n_score = n_out = n_lat + kc_ref.shape[1] // kc
    rows_a = kc
    dv = vtx_ref.shape[1]
    ones = jnp.ones((16, kc), BF16)
    lam = lam_ref[0]

    def key_piece(j):
        return kx_ref[0, j * kc:(j + 1) * kc, :] if j < n_lat else kc_ref[0, (j - n_lat) * kc:(j - n_lat + 1) * kc, :]

    def value_piece(j):
        return vtx_ref[0, :, j * kc:(j + 1) * kc] if j < n_lat else vtc_ref[0, :, (j - n_lat) * kc:(j - n_lat + 1) * kc]

    def stacked(qt):
        row = lax.broadcasted_iota(jnp.int32, qt.shape, 0)
        zero = jnp.zeros_like(qt)
        return jnp.concatenate([jnp.where(row < DA_QK_DIM, qt, zero), jnp.where(row >= DA_QK_DIM, qt, zero)],
                               axis=1)

    def score_piece(qs, st, mx, j):
        r0 = j * rows_a
        st[r0:r0 + rows_a, :] = jnp.dot(key_piece(j), qs, preferred_element_type=F32)
        part = jnp.max(st[r0:r0 + rows_a, :].reshape(rows_a // 8, 8, st.shape[-1]), axis=0)
        mx[...] = part if j == 0 else jnp.maximum(mx[...], part)
        return part

    def out_piece(st, m, acc, j):
        e = jnp.exp2(st[j * kc:(j + 1) * kc, :] - m).astype(BF16)
        vt1 = jnp.concatenate([value_piece(j), ones], axis=0)
        d = jnp.dot(vt1, e, preferred_element_type=F32)
        return d if acc is None else acc + d

    def finish(acc, c):
        ot = acc[:dv] * (1.0 / acc[dv:dv + 1])
        o = (ot[:, :tq] - lam * ot[:, tq:]).T
        o = _rms(o, o.shape[-1]) * g_ref[...]
        o_ref[0, c * tq:(c + 1) * tq, :] = o.astype(o_ref.dtype)

    def overlapped(st_cur, mx_cur, q_next, st_next, mx_next, c):
        m = jnp.max(mx_cur[...], axis=0, keepdims=True)
        qs = stacked(q_next)
        acc = None
        p = 0
        parts = []
        for j in range(n_out):
            while p < n_score and p * n_out <= j * n_score:
                parts.append(score_piece(qs, st_next, mx_next, p))
                p += 1
            m_j = m + 0.0 * parts[j - ANCHOR_LAG][0:1, :] if ANCHOR_LAG <= j < len(parts) + ANCHOR_LAG else m
            acc = out_piece(st_cur, m_j, acc, j)
        for p in range(p, n_score):
            score_piece(qs, st_next, mx_next, p)
        finish(acc, c)

    @pl.when(pl.program_id(2) == 0)
    def _():
        qs = stacked(q_ref[0, :, 0:tq])
        for j in range(n_score):
            score_piece(qs, st0, mx0, j)

    overlapped(st0, mx0, q_ref[0, :, tq:2 * tq], st1, mx1, 0)
    overlapped(st1, mx1, qn_ref[0], st0, mx0, 1)


def _attention(lam, qt, kx, kc_, vtx, vtc, g, tq):
    B, W, S = qt.shape
    C = kc_.shape[1]
    Sk = S + C
    H = W // LANES
    tb = 2 * tq
    n_steps = S // tb
    kc = 256 if (S % 256 == 0 and C % 256 == 0) else LANES
    return pl.pallas_call(
        functools.partial(_attn_kernel, tq, kc),
        out_shape=jax.ShapeDtypeStruct((B, S, W), BF16),
        grid=(B, H, n_steps),
        in_specs=[pl.BlockSpec(memory_space=pltpu.SMEM),
                  pl.BlockSpec((1, LANES, tb), lambda b, h, i: (b, h, i)),
                  pl.BlockSpec((1, LANES, tq), lambda b, h, i: (b, h, jnp.minimum(i + 1, n_steps - 1) * 2)),
                  pl.BlockSpec((1, S, LANES), lambda b, h, i: (b, 0, h)),
                  pl.BlockSpec((1, C, LANES), lambda b, h, i: (b, 0, h)),
                  pl.BlockSpec((1, LANES, S), lambda b, h, i: (b, h, 0)),
                  pl.BlockSpec((1, LANES, C), lambda b, h, i: (b, h, 0)),
                  pl.BlockSpec((1, LANES), lambda b, h, i: (0, 0))],
        out_specs=pl.BlockSpec((1, tb, LANES), lambda b, h, i: (b, i, h)),
        scratch_shapes=[pltpu.VMEM((Sk, 2 * tq), F32), pltpu.VMEM((Sk, 2 * tq), F32),
                        pltpu.VMEM((8, 2 * tq), F32), pltpu.VMEM((8, 2 * tq), F32)],
        compiler_params=_cparams("arbitrary", "arbitrary", "arbitrary"),
        name="diff_attn",
    )(lam, qt, qt, kx, kc_, vtx, vtc, g)


def _filter_kernel(z_ref, w1, b1, w2, b2, w3, b3, w4, fr, dec_ref, o_ref):
    dot = functools.partial(jnp.dot, preferred_element_type=F32, precision=HIGHEST)
    f = fr[...]
    h = jnp.sin(f * (dot(z_ref[...], w1[...]) + b1[...]))
    h = jnp.sin(f * (dot(h, w2[...]) + b2[...]))
    h = jnp.sin(f * (dot(h, w3[...]) + b3[...]))
    taps = dot(h, w4[...]) * dec_ref[...]
    tl, W = o_ref.shape[1:]
    o_ref[0] = taps[:, :W]
    pos = lax.broadcasted_iota(jnp.int32, (tl, W), 0) + pl.program_id(0) * tl
    o_ref[1] = jnp.where(pos == 0, 0.0, taps[:, W:])


def _implicit_filter(z, w1, b1, w2, b2, w3, b3, w4, fr, dec, tl):
    L = z.shape[0]
    N = w4.shape[1]
    full = lambda a: pl.BlockSpec(a.shape, lambda i: (0, 0))
    return pl.pallas_call(
        _filter_kernel,
        out_shape=jax.ShapeDtypeStruct((2, L, N // 2), F32),
        grid=(L // tl,),
        in_specs=[pl.BlockSpec((tl, z.shape[1]), lambda i: (i, 0)),
                  full(w1), full(b1), full(w2), full(b2), full(w3), full(b3), full(w4), full(fr),
                  pl.BlockSpec((tl, N), lambda i: (i, 0))],
        out_specs=pl.BlockSpec((2, tl, N // 2), lambda i: (0, i, 0)),
        compiler_params=_cparams("arbitrary"),
        name="hyena_filter",
    )(z, w1, b1, w2, b2, w3, b3, w4, fr, dec)


def _conv3(a, w_ref, b_ref):
    L = a.shape[0]
    row = lax.broadcasted_iota(jnp.int32, a.shape, 0)
    prev = jnp.where(row == 0, 0.0, pltpu.roll(a, 1, 0))
    nxt = jnp.where(row == L - 1, 0.0, pltpu.roll(a, L - 1, 0))
    return prev * w_ref[0:1, :] + a * w_ref[1:2, :] + nxt * w_ref[2:3, :] + b_ref[...]


def _gate_kernel(x1_ref, v_ref, w1_ref, wv_ref, b1_ref, bv_ref, u_ref):
    u = _conv3(x1_ref[0].astype(F32), w1_ref, b1_ref) * _conv3(v_ref[0].astype(F32), wv_ref, bv_ref)
    u_ref[0] = u.astype(u_ref.dtype)


def _hyena_gate(hy, conv_w, conv_b):
    B, L, W3 = hy.shape
    W = W3 // 3
    nw = W // LANES
    act = lambda off: pl.BlockSpec((1, L, LANES), lambda b, j: (b, 0, j + off))
    cw = lambda off: pl.BlockSpec((3, LANES), lambda b, j: (0, j + off))
    cb = lambda off: pl.BlockSpec((1, LANES), lambda b, j: (0, j + off))
    return pl.pallas_call(
        _gate_kernel,
        out_shape=jax.ShapeDtypeStruct((B, L, W), BF16),
        grid=(B, nw),
        in_specs=[act(nw), act(2 * nw), cw(nw), cw(2 * nw), cb(nw), cb(2 * nw)],
        out_specs=pl.BlockSpec((1, L, LANES), lambda b, j: (b, 0, j)),
        compiler_params=_cparams("arbitrary", "arbitrary"),
        name="hyena_gate",
    )(hy, hy, conv_w, conv_w, conv_b, conv_b)


def _dot_split(a, b):
    a_hi, b_hi = a.astype(BF16), b.astype(BF16)
    a_lo = (a - a_hi.astype(F32)).astype(BF16)
    b_lo = (b - b_hi.astype(F32)).astype(BF16)
    dot = functools.partial(jnp.dot, preferred_element_type=F32)
    return dot(a_hi, b_hi) + (dot(a_lo, b_hi) + dot(a_hi, b_lo))


def _outer_dft_kernel(split, f_ref, x_ref, o_ref):
    if split:
        o_ref[0] = _dot_split(f_ref[...], x_ref[0]).astype(o_ref.dtype)
    else:
        o_ref[0] = jnp.dot(f_ref[...], x_ref[0], preferred_element_type=F32).astype(o_ref.dtype)


def _outer_dft(f, x, out_dtype, tn, name):
    B, K, N = x.shape
    M = f.shape[0]
    return pl.pallas_call(
        functools.partial(_outer_dft_kernel, x.dtype == F32),
        out_shape=jax.ShapeDtypeStruct((B, M, N), out_dtype),
        grid=(B, N // tn),
        in_specs=[pl.BlockSpec((M, K), lambda b, j: (0, 0)),
                  pl.BlockSpec((1, K, tn), lambda b, j: (b, 0, j))],
        out_specs=pl.BlockSpec((1, M, tn), lambda b, j: (b, 0, j)),
        compiler_params=_cparams("arbitrary", "arbitrary"),
        name=name,
    )(f, x)


def _twiddle(re, im, tr, ti, conj):
    if conj:
        return re * tr + im * ti, im * tr - re * ti
    return re * tr - im * ti, re * ti + im * tr


def _inner_spectrum_kernel(kb, a_ref, tr_ref, ti_ref, f_ref, o_ref):
    reps = a_ref.shape[-1] // LANES
    for j in range(kb):
        tr = jnp.concatenate([tr_ref[j]] * reps, axis=-1)
        ti = jnp.concatenate([ti_ref[j]] * reps, axis=-1)
        xs = []
        for s in range(2):
            re, im = _twiddle(a_ref[s, 0, j], a_ref[s, 1, j], tr, ti, False)
            xs.append(_dot_split(f_ref[...], jnp.concatenate([re, im], axis=0)))
        n2 = xs[0].shape[0] // 2
        o_ref[0, 0, j] = xs[0][:n2] + xs[1][:n2]
        o_ref[0, 1, j] = xs[0][n2:] - xs[1][n2:]


def _inner_conv_kernel(kb, a_ref, kf_ref, tr_ref, ti_ref, f_ref, fi_ref, o_ref):
    reps = a_ref.shape[-1] // LANES
    for j in range(kb):
        tr = jnp.concatenate([tr_ref[j]] * reps, axis=-1)
        ti = jnp.concatenate([ti_ref[j]] * reps, axis=-1)
        re, im = _twiddle(a_ref[0, 0, j].astype(F32), a_ref[0, 1, j].astype(F32), tr, ti, False)
        n2 = re.shape[0]
        x = jnp.dot(f_ref[...], jnp.concatenate([re, im], axis=0).astype(BF16), preferred_element_type=F32)
        xr, xi = x[:n2], x[n2:]
        kr, ki = kf_ref[0, 0, j], kf_ref[0, 1, j]
        yr = xr * kr - xi * ki
        yi = xr * ki + xi * kr
        y = jnp.dot(fi_ref[...], jnp.concatenate([yr, yi], axis=0).astype(BF16), preferred_element_type=F32)
        re, im = _twiddle(y[:n2], y[n2:], tr, ti, True)
        o_ref[0, 0, j] = re.astype(o_ref.dtype)
        o_ref[0, 1, j] = im.astype(o_ref.dtype)


def _inner_stage(a5, kf5, tr, ti, f2, f2i, kb):
    B, _, N1, N2, W = a5.shape
    blk = lambda: pl.BlockSpec((1, 2, kb, N2, W), lambda g, b: (b, 0, g, 0, 0))
    tw = pl.BlockSpec((kb, N2, LANES), lambda g, b: (g, 0, 0))
    mat = pl.BlockSpec((2 * N2, 2 * N2), lambda g, b: (0, 0))
    if kf5 is None:
        kern = functools.partial(_inner_spectrum_kernel, kb)
        pair = pl.BlockSpec((2, 2, kb, N2, W), lambda g, b: (0, 0, g, 0, 0))
        in_specs, args, out_dtype, name = [pair, tw, tw, mat], (a5, tr, ti, f2), F32, "hyena_filter_spectrum"
        B = 1
    else:
        kern = functools.partial(_inner_conv_kernel, kb)
        kf_spec = pl.BlockSpec((1, 2, kb, N2, W), lambda g, b: (0, 0, g, 0, 0))
        in_specs, args, out_dtype, name = ([blk(), kf_spec, tw, tw, mat, mat], (a5, kf5, tr, ti, f2, f2i),
                                           BF16, "hyena_inner_conv")
    return pl.pallas_call(
        kern,
        out_shape=jax.ShapeDtypeStruct((B,) + a5.shape[1:], out_dtype),
        grid=(N1 // kb, B),
        in_specs=in_specs,
        out_specs=blk(),
        compiler_params=_cparams("arbitrary", "arbitrary"),
        name=name,
    )(*args)


def _hyena_out_kernel(x0_ref, w0_ref, b0_ref, y_ref, u_ref, hb_ref, o_ref):
    x0 = _conv3(x0_ref[0].astype(F32), w0_ref, b0_ref)
    o_ref[0] = (x0 * (y_ref[0] + u_ref[0].astype(F32) * hb_ref[...])).astype(o_ref.dtype)


def _hyena_out(hy, conv_w, conv_b, y, u, hbias):
    B, L, W = u.shape
    nw = W // LANES
    blk = pl.BlockSpec((1, L, LANES), lambda b, j: (b, 0, j))
    vec = lambda r: pl.BlockSpec((r, LANES), lambda b, j: (0, j))
    return pl.pallas_call(
        _hyena_out_kernel,
        out_shape=jax.ShapeDtypeStruct((B, L, W), BF16),
        grid=(B, nw),
        in_specs=[blk, vec(3), vec(1), blk, blk, vec(1)],
        out_specs=blk,
        compiler_params=_cparams("arbitrary", "arbitrary"),
        name="hyena_out",
    )(hy, conv_w, conv_b, y, u, hbias)


def _dft_constants(L):
    N = 2 * L
    N2 = FFT_N2
    N1 = N // N2
    k1 = np.arange(N1, dtype=np.float64)
    th1 = 2.0 * np.pi * np.outer(k1, k1) / N1
    fwd = np.concatenate([np.cos(th1), -np.sin(th1)], axis=0)
    inv = np.concatenate([np.cos(th1), -np.sin(th1)], axis=1)[: N1 // 2] / N
    n2 = np.arange(N2, dtype=np.float64)
    tw = 2.0 * np.pi * np.outer(k1, n2) / N
    tr = np.repeat(np.cos(tw)[:, :, None], LANES, axis=2)
    ti = np.repeat(-np.sin(tw)[:, :, None], LANES, axis=2)
    th2 = 2.0 * np.pi * np.outer(n2, n2) / N2
    c2, s2 = np.cos(th2), np.sin(th2)
    f2 = np.block([[c2, s2], [-s2, c2]])
    f2i = np.block([[c2, -s2], [s2, c2]])
    f = lambda a: jnp.asarray(a, F32)
    return dict(N1=N1, N2=N2, fwd=f(fwd), inv=f(inv), tr=f(tr), ti=f(ti), f2=f(f2), f2i=f(f2i))


def _filter_features(L, emb_dim):
    bands = (emb_dim - 1) // 2
    t = np.linspace(0.0, 1.0, L, dtype=np.float32)[:, None]
    w = (np.float32(2.0 * math.pi) * np.arange(L, dtype=np.float32)[:, None] / np.float32(L)).astype(np.float32)
    f = np.linspace(1e-4, bands - 1, bands, dtype=np.float32)
    fw = (f * w).astype(np.float32)
    z = np.concatenate([t, np.cos(fw.astype(np.float64)).astype(np.float32),
                        -np.sin(fw.astype(np.float64)).astype(np.float32)], axis=-1)
    return jnp.asarray(z), t


def _decay_window(t, W):
    deltas = np.abs(np.linspace(math.log(HY_DECAY_TARGET) / HY_SLOW_DECAY,
                                math.log(HY_DECAY_TARGET) / HY_FAST_DECAY, W, dtype=np.float32))
    dec = np.exp((-t * deltas).astype(np.float32).astype(np.float64)).astype(np.float32)
    return jnp.asarray(np.concatenate([dec, dec], axis=1))


def _hyena(hy, conv_w, conv_b, fw1, fb1, fw2, fb2, fw3, fb3, fw4, ffreq, hbias):
    B, L, W3 = hy.shape
    W = W3 // 3
    C = _dft_constants(L)
    N1, N2 = C["N1"], C["N2"]
    row = lambda a: a.reshape(1, -1)
    z, t = _filter_features(L, fw1.shape[0])
    kpad = LANES - fw1.shape[0]
    z = jnp.pad(z, ((0, 0), (0, kpad)))
    taps = _implicit_filter(z, jnp.pad(fw1, ((0, kpad), (0, 0))), row(fb1), fw2, row(fb2), fw3, row(fb3), fw4,
                            row(ffreq),
                            _decay_window(t, W), tl=min(L, 512))
    lanes = N2 * W
    tn = min(lanes, 8192)
    kf = _outer_dft(C["fwd"][:, : N1 // 2], taps.reshape(2, N1 // 2, lanes), F32, tn, "hyena_filter_outer")
    kf5 = _inner_stage(kf.reshape(2, 2, N1, N2, W), None, C["tr"], C["ti"], C["f2"], None, kb=min(N1, 2))
    u = _hyena_gate(hy, conv_w, row(conv_b))
    a = _outer_dft(C["fwd"][:, : N1 // 2].astype(BF16), u.reshape(B, N1 // 2, lanes), BF16, tn,
                   "hyena_outer_fwd")
    bb = _inner_stage(a.reshape(B, 2, N1, N2, W), kf5, C["tr"], C["ti"], C["f2"].astype(BF16),
                      C["f2i"].astype(BF16), kb=min(N1, 8))
    y = _outer_dft(C["inv"].astype(BF16), bb.reshape(B, 2 * N1, lanes), F32, tn, "hyena_outer_inv")
    return _hyena_out(hy, conv_w, row(conv_b), y.reshape(B, L, W), u, row(hbias))


ROUTE_ROWS = 16


def _route_tile(h, rw_ref, rb_ref, tri_ref, carry_ref):
    tm = h.shape[0]
    E = rw_ref.shape[0] // 2
    nt = (((1,), (1,)), ((), ()))
    h_hi = h.astype(BF16)
    h_lo = (h - h_hi.astype(F32)).astype(BF16)
    a = lax.dot_general(rw_ref[...], h_hi, nt, preferred_element_type=F32)
    b = lax.dot_general(rw_ref[0:E, :], h_lo, nt, preferred_element_type=F32)
    reps = tm // LANES
    logits = a[:E] + a[E:] + b + jnp.concatenate([rb_ref[...]] * reps, axis=1)
    row = lax.broadcasted_iota(jnp.int32, (E, tm), 0)
    v = logits
    ids, vals = [], []
    for _ in range(TOP_K):
        m = jnp.max(v, axis=0, keepdims=True)
        idx = jnp.min(jnp.where(v == m, row, E), axis=0, keepdims=True)
        ids.append(idx)
        vals.append(m)
        v = jnp.where(row == idx, -jnp.inf, v)
    ex = [jnp.exp(vk - vals[0]) for vk in vals]
    den = ex[0]
    for t in ex[1:]:
        den = den + t
    hot = [row == idx for idx in ids]
    member = jnp.zeros((E, tm), F32)
    for hk in hot:
        member = jnp.where(hk, 1.0, member)
    member = member.astype(BF16)
    carry = carry_ref[...]
    before = (jnp.dot(member, tri_ref[...], preferred_element_type=F32)
              + jnp.concatenate([carry] * reps, axis=1))
    ranks = [jnp.sum(jnp.where(hk, before, 0.0), axis=0, keepdims=True) for hk in hot]
    carry_ref[...] = carry + jnp.dot(member, jnp.ones((tm, LANES), BF16), preferred_element_type=F32)
    out_row = lax.broadcasted_iota(jnp.int32, (ROUTE_ROWS, tm), 0)
    rec = jnp.zeros((ROUTE_ROWS, tm), F32)
    for k in range(TOP_K):
        rec = jnp.where(out_row == k, ids[k].astype(F32), rec)
        rec = jnp.where(out_row == TOP_K + k, ex[k] / den, rec)
        rec = jnp.where(out_row == 2 * TOP_K + k, ranks[k], rec)
    return rec


def _merge_kernel(x_ref, oa_ref, ob_ref, gt_ref, g1_ref, sc_ref, sh_ref, ng_ref, wa_ref, wb_ref, wo_ref,
                  rw_ref, rb_ref, tri_ref, xo_ref, h_ref, rec_ref, cnt_ref, before_ref, carry_ref):
    @pl.when((pl.program_id(0) == 0) & (pl.program_id(1) == 0))
    def _():
        carry_ref[...] = jnp.zeros_like(carry_ref)

    @pl.when(pl.program_id(1) == 0)
    def _():
        before_ref[0] = carry_ref[...]

    D = x_ref.shape[-1]
    gt = gt_ref[0]
    m = (gt[:, :D].astype(F32) * jnp.dot(oa_ref[0], wa_ref[...], preferred_element_type=F32)
         + gt[:, D:].astype(F32) * jnp.dot(ob_ref[0], wb_ref[...], preferred_element_type=F32))
    y = jnp.dot(m.astype(BF16), wo_ref[...], preferred_element_type=F32)
    xn = x_ref[0] + g1_ref[0] * y
    xo_ref[0] = xn
    h = _rms(xn, D) * ng_ref[...]
    h = h * (1.0 + sc_ref[0]) + sh_ref[0]
    h_ref[0] = h.astype(h_ref.dtype)
    rec_ref[0] = _route_tile(h, rw_ref, rb_ref, tri_ref, carry_ref)
    cnt_ref[...] = carry_ref[...]


def _merge(x, oa, ob, gates, g1, sc2, sh2, ng, wa, wb, wo, rw, rb, tm):
    B, S, D = x.shape
    E = rw.shape[1]
    rw_hi = rw.T.astype(BF16)
    rw2 = jnp.concatenate([rw_hi, (rw.T - rw_hi.astype(F32)).astype(BF16)], axis=0)
    rb2 = jnp.broadcast_to(rb.reshape(E, 1), (E, LANES))
    tri = jnp.asarray(np.triu(np.ones((tm, tm), np.float32), 1), BF16)
    act = lambda w: pl.BlockSpec((1, tm, w), lambda b, i: (b, i, 0))
    mod = pl.BlockSpec((1, 1, D), lambda b, i: (b, 0, 0))
    full = lambda a: pl.BlockSpec(a.shape, lambda b, i: (0, 0))
    return pl.pallas_call(
        _merge_kernel,
        out_shape=[jax.ShapeDtypeStruct((B, S, D), F32), jax.ShapeDtypeStruct((B, S, D), BF16),
                   jax.ShapeDtypeStruct((B, ROUTE_ROWS, S), F32), jax.ShapeDtypeStruct((E, LANES), F32),
                   jax.ShapeDtypeStruct((B, E, LANES), F32)],
        grid=(B, S // tm),
        in_specs=[act(D), act(oa.shape[-1]), act(ob.shape[-1]), act(2 * D), mod, mod, mod, full(ng),
                  full(wa), full(wb), full(wo), full(rw2), full(rb2), full(tri)],
        out_specs=[act(D), act(D), pl.BlockSpec((1, ROUTE_ROWS, tm), lambda b, i: (b, 0, i)),
                   pl.BlockSpec((E, LANES), lambda b, i: (0, 0)),
                   pl.BlockSpec((1, E, LANES), lambda b, i: (b, 0, 0))],
        scratch_shapes=[pltpu.VMEM((E, LANES), F32)],
        compiler_params=_cparams("arbitrary", "arbitrary"),
        name="merge_norm_router",
    )(x, oa, ob, gates, g1, sc2, sh2, ng, wa, wb, wo, rw2, rb2, tri)


def _moe_kernel(be_ref, nu_ref, x_ref, w1_ref, b1_ref, w2_ref, b2_ref, o_ref, w1b_ref, w2b_ref):
    i = pl.program_id(0)

    @pl.when(i < nu_ref[0])
    def _():
        @pl.when((i == 0) | (be_ref[i] != be_ref[jnp.maximum(i - 1, 0)]))
        def _():
            w1b_ref[...] = w1_ref[0].astype(BF16)
            w2b_ref[...] = w2_ref[0].astype(BF16)

        F = w2_ref.shape[1]
        gu = jnp.dot(x_ref[...], w1b_ref[...], preferred_element_type=F32) + b1_ref[0]
        gate = jnp.minimum(gu[:, :F], SWIGLU_LIMIT)
        up = jnp.clip(gu[:, F:], -SWIGLU_LIMIT, SWIGLU_LIMIT)
        act = (up + 1.0) * gate * _sigmoid(SWIGLU_ALPHA * gate)
        y = jnp.dot(act.astype(BF16), w2b_ref[...], preferred_element_type=F32) + b2_ref[0]
        o_ref[...] = y.astype(o_ref.dtype)


def _moe_experts(block_e, n_used, xg, w1, b1, w2, b2, bm):
    R, D = xg.shape
    E, _, F2 = w1.shape
    F = w2.shape[1]
    grid_spec = pltpu.PrefetchScalarGridSpec(
        num_scalar_prefetch=2,
        grid=(R // bm,),
        in_specs=[pl.BlockSpec((bm, D), lambda i, be, nu: (i, 0)),
                  pl.BlockSpec((1, D, F2), lambda i, be, nu: (be[i], 0, 0)),
                  pl.BlockSpec((1, 1, F2), lambda i, be, nu: (be[i], 0, 0)),
                  pl.BlockSpec((1, F, D), lambda i, be, nu: (be[i], 0, 0)),
                  pl.BlockSpec((1, 1, D), lambda i, be, nu: (be[i], 0, 0))],
        out_specs=pl.BlockSpec((bm, D), lambda i, be, nu: (i, 0)),
        scratch_shapes=[pltpu.VMEM((D, F2), BF16), pltpu.VMEM((F, D), BF16)],
    )
    return pl.pallas_call(
        _moe_kernel,
        out_shape=jax.ShapeDtypeStruct((R, D), BF16),
        grid_spec=grid_spec,
        compiler_params=pltpu.CompilerParams(dimension_semantics=("arbitrary",),
                                             vmem_limit_bytes=V7X_VMEM_LIMIT_MOE_BYTES),
        name="moe_experts",
    )(block_e, n_used, xg, w1, b1.reshape(E, 1, F2), w2, b2.reshape(E, 1, D))


def _combine_kernel(nb, x_ref, g_ref, *refs):
    o_ref = refs[-1]
    b = pl.program_id(0)
    for grp in range((len(refs) - 1) // 2):
        y_ref, w_ref = refs[2 * grp:2 * grp + 2]

        @pl.when((b >= grp * nb) & (b < (grp + 1) * nb))
        def _():
            w = w_ref[0]
            acc = w[:, 0:1] * y_ref[0, 0].astype(F32)
            for k in range(1, y_ref.shape[0]):
                acc = acc + w[:, k:k + 1] * y_ref[k, 0].astype(F32)
            o_ref[0] = x_ref[0] + g_ref[0] * acc


def _combine(x, ygs, wtss, g2, tm):
    B, S, D = x.shape
    K, nb = ygs[0].shape[:2]
    in_specs = [pl.BlockSpec((1, tm, D), lambda b, i: (b, i, 0)), pl.BlockSpec((1, 1, D), lambda b, i: (b, 0, 0))]
    args = [x, g2]

    def group_block(grp, b, i):
        mine = (b >= grp * nb) & (b < (grp + 1) * nb)
        return jnp.where(mine, b - grp * nb, 0), jnp.where(mine, i, 0)

    for grp in range(len(ygs)):
        at = functools.partial(group_block, grp)
        in_specs += [pl.BlockSpec((K, 1, tm, D), lambda b, i, at=at: (0, *at(b, i), 0)),
                     pl.BlockSpec((1, tm, K), lambda b, i, at=at: (*at(b, i), 0))]
        args += [ygs[grp], wtss[grp]]
    return pl.pallas_call(
        functools.partial(_combine_kernel, nb),
        out_shape=jax.ShapeDtypeStruct((B, S, D), F32),
        grid=(B, S // tm),
        in_specs=in_specs,
        out_specs=pl.BlockSpec((1, tm, D), lambda b, i: (b, i, 0)),
        compiler_params=_cparams("arbitrary", "arbitrary"),
        name="moe_combine",
    )(*args)


def _route(rec, counts, before, tok0, bm):
    B, _, S = rec.shape
    T = B * S
    E = counts.shape[0]
    I32 = jnp.int32
    experts = jnp.arange(E, dtype=I32)
    pick = lambda table, idx: jnp.sum(jnp.where(idx[..., None] == experts, table, 0), axis=-1)
    rec = jnp.swapaxes(rec, 1, 2).reshape(T, ROUTE_ROWS)
    top_i = rec[:, 0:TOP_K].astype(I32)
    wts = rec[:, TOP_K:2 * TOP_K]
    rank = rec[:, 2 * TOP_K:3 * TOP_K].astype(I32) - pick(before.astype(I32), top_i)
    counts = counts.astype(I32)
    A = T * TOP_K
    padded = (counts + bm - 1) // bm * bm
    pend = jnp.cumsum(padded)
    pstart = pend - padded
    start = jnp.cumsum(counts) - counts
    pos = pick(pstart, top_i) + rank
    keys = top_i.reshape(-1) * A + jnp.arange(A, dtype=I32)
    order = jnp.sort(keys) % A
    n_rows = A + E * bm
    n_blocks = n_rows // bm
    blk_start = jnp.arange(n_blocks, dtype=I32) * bm
    block_e = jnp.minimum(jnp.sum((pend[None, :] <= blk_start[:, None]).astype(I32), axis=1), E - 1)
    off = blk_start[:, None] + jnp.arange(bm, dtype=I32)[None, :] - pick(pstart, block_e)[:, None]
    valid = (off >= 0) & (off < pick(counts, block_e)[:, None])
    src = jnp.clip(pick(start, block_e)[:, None] + off, 0, A - 1)
    sorted_tok = order.at[src.reshape(-1)].get(mode="promise_in_bounds").reshape(n_blocks, bm) // TOP_K
    row_tok = jnp.where(valid, sorted_tok, (blk_start[:, None] + jnp.arange(bm, dtype=I32)[None, :]) % T)
    n_used = (pend[-1:] // bm).astype(I32)
    return wts, row_tok.reshape(-1) + tok0, pos, block_e.astype(I32), n_used


def kernel(x, c, ctx, c_ctx, ada_w, ada_b, norm1_g, norm2_g, w_in, b_in, q_norm_g, k_norm_g, lambda_q1, lambda_k1, lambda_q2, lambda_k2, subln_g, conv_w, conv_b, filt_w1, filt_b1, filt_w2, filt_b2, filt_w3, filt_b3, filt_w4, filt_freq, hyena_bias, w_up_a, w_up_b, w_out, router_w, router_b, exp_w1, exp_b1, exp_w2, exp_b2):
    B, S, D = x.shape
    depth = ada_w.shape[0]
    assert depth == 1, "context-stream update between layers is not implemented"
    l = 0
    QC = DA_HEADS * 2 * DA_QK_DIM
    W = D // 2
    row = lambda a: a.reshape(1, -1)

    lambda_init = 0.8 - 0.6 * math.exp(-0.3 * l)
    lam = (jnp.exp(jnp.sum(lambda_q1[l] * lambda_k1[l])) - jnp.exp(jnp.sum(lambda_q2[l] * lambda_k2[l]))
           + lambda_init).reshape(1).astype(F32)

    R = -(-(B + 1) // 8) * 8
    cc = jnp.concatenate([c, c_ctx[None, :], jnp.zeros((R - B - 1, D), F32)], axis=0)
    mods = _ada(cc, ada_w[l], row(ada_b[l]))
    sh1, sc1, g1, sh2, sc2, g2 = [mods[:B, i * D:(i + 1) * D].reshape(B, 1, D) for i in range(6)]
    sh1c, sc1c = [jnp.broadcast_to(mods[B, i * D:(i + 1) * D].reshape(1, 1, D), (B, 1, D)) for i in range(2)]

    w_in_b = w_in[l].astype(BF16)
    qg = row(jnp.tile(q_norm_g[l], QC // DA_QK_DIM))
    kg = row(jnp.tile(k_norm_g[l], QC // DA_QK_DIM))
    grp = np.arange(QC) // DA_QK_DIM
    bd = jnp.asarray(grp[:, None] == grp[None, :], BF16)
    qscale = DA_QK_DIM ** -0.5 * math.log2(math.e)
    spec = (("q", 0, QC), ("k", QC, QC), ("transposed", 2 * QC, QC), ("plain", 3 * QC, 3 * W),
            ("sigmoid", 3 * QC + 3 * W, 2 * D))
    qtx, kx, vtx, hy, gates = _in_proj(x, sc1, sh1, row(norm1_g[l]), w_in_b, row(b_in[l]), qg, kg, bd,
                                       _rope_tables(S, QC), spec, tm=min(S, 512), qscale=qscale)
    spec_c = (("kc", 0, QC), ("transposed", QC, QC))
    kc, vtc = _in_proj(ctx, sc1c, sh1c, row(norm1_g[l]), w_in_b[:, QC:3 * QC], row(b_in[l][QC:3 * QC]), qg,
                       kg, bd, None, spec_c, tm=ctx.shape[1], qscale=qscale)

    oa = _attention(lam, qtx, kx, kc, vtx, vtc, row(subln_g[l] * (1.0 - lambda_init)), tq=min(S // 2, 256))
    ob = _hyena(hy, conv_w[l], conv_b[l], filt_w1[l], filt_b1[l], filt_w2[l], filt_b2[l], filt_w3[l],
                filt_b3[l], filt_w4[l], filt_freq[l], hyena_bias[l])
    x_new, h2, rec, counts, before = _merge(x, oa, ob, gates, g1, sc2, sh2, row(norm2_g[l]),
                                            w_up_a[l].astype(BF16), w_up_b[l].astype(BF16), w_out[l].astype(BF16),
                                            router_w[l], row(router_b[l]), tm=min(S, 512))

    groups = MOE_BATCH_GROUPS if B % MOE_BATCH_GROUPS == 0 else 1
    nb = B // groups
    bm = EXPERT_BLOCK
    edges = jnp.concatenate([before[:, :, 0], counts[None, :, 0]], axis=0)
    ygs, wtss = [], []
    for grp in range(groups):
        b0 = grp * nb
        wts, row_tok, pos, block_e, n_used = _route(rec[b0:b0 + nb], edges[b0 + nb] - edges[b0], edges[b0],
                                                    b0 * S, bm)
        xg = h2.reshape(B * S, D).at[row_tok].get(mode="promise_in_bounds")
        yb = _moe_experts(block_e, n_used, xg, exp_w1[l], exp_b1[l], exp_w2[l], exp_b2[l], bm)
        ygs.append(yb.at[pos.T].get(mode="promise_in_bounds").reshape(TOP_K, nb, S, D))
        wtss.append(wts.reshape(nb, S, TOP_K))
    return _combine(x_new, ygs, wtss, g2, tm=min(S, 512))
```

```python
import functools
import math

import jax
import jax.numpy as jnp
import numpy as np
from jax import lax
from jax.experimental import pallas as pl
from jax.experimental.pallas import tpu as pltpu

F32 = jnp.float32
BF16 = jnp.bfloat16
HIGHEST = lax.Precision.HIGHEST

GRID_W = 64
DA_HEADS = 4
DA_QK_DIM = 64
DA_V_DIM = 2 * DA_QK_DIM
ROPE_THETA = 10000.0
ROPE_FREQS = DA_QK_DIM // 4
HY_DECAY_TARGET = 1e-2
HY_FAST_DECAY = 0.3
HY_SLOW_DECAY = 1.5
TOP_K = 4
SWIGLU_LIMIT = 7.0
SWIGLU_ALPHA = 1.702
EPS = 1e-6

LANES = 128
V7X_VMEM_LIMIT_BYTES = 48 * 1024 * 1024
V7X_VMEM_LIMIT_MOE_BYTES = 56 * 1024 * 1024
FFT_N2 = 128
ANCHOR_LAG = 2
EXPERT_BLOCK = 512
MOE_BATCH_GROUPS = 1


def _cparams(*sem):
    return pltpu.CompilerParams(dimension_semantics=sem, vmem_limit_bytes=V7X_VMEM_LIMIT_BYTES)


def _sigmoid(x):
    return 1.0 / (1.0 + jnp.exp(-x))


def _rms(x, width):
    return x * lax.rsqrt(jnp.sum(x * x, axis=-1, keepdims=True) * (1.0 / width) + EPS)


def _ada_kernel(c_ref, w_ref, b_ref, o_ref):
    c = c_ref[...]
    s = c * _sigmoid(c)
    o_ref[...] = jnp.dot(s, w_ref[...], preferred_element_type=F32, precision=HIGHEST) + b_ref[...]


def _ada(cc, w, b):
    R, D = cc.shape
    N = w.shape[1]
    tn = D
    return pl.pallas_call(
        _ada_kernel,
        out_shape=jax.ShapeDtypeStruct((R, N), F32),
        grid=(N // tn,),
        in_specs=[pl.BlockSpec((R, D), lambda j: (0, 0)),
                  pl.BlockSpec((D, tn), lambda j: (0, j)),
                  pl.BlockSpec((1, tn), lambda j: (0, j))],
        out_specs=pl.BlockSpec((R, tn), lambda j: (0, j)),
        compiler_params=_cparams("arbitrary"),
        name="ada",
    )(cc, w, b)


_TRANSPOSED_KINDS = ("q", "transposed")


def _in_kernel(spec, use_rope, qscale, x_ref, sc_ref, sh_ref, g_ref, w_ref, b_ref, qg_ref, kg_ref,
               bd_ref, *rest):
    if use_rope:
        cos_ref, sin_ref, *outs = rest
    else:
        outs = rest
    x = x_ref[0]
    h = _rms(x, x.shape[-1]) * g_ref[...]
    h = (h * (1.0 + sc_ref[0]) + sh_ref[0]).astype(BF16)
    for (kind, c0, cw), o_ref in zip(spec, outs):
        y = jnp.dot(h, w_ref[:, c0:c0 + cw], preferred_element_type=F32) + b_ref[:, c0:c0 + cw]
        if kind in ("q", "k", "kc"):
            ssq = jnp.dot((y * y).astype(BF16), bd_ref[...], preferred_element_type=F32)
            gn = qg_ref if kind == "q" else kg_ref
            y = y * lax.rsqrt(ssq * (1.0 / DA_QK_DIM) + EPS) * gn[...]
            if kind != "kc":
                lane = lax.broadcasted_iota(jnp.int32, y.shape, 1)
                partner = jnp.where((lane % (2 * ROPE_FREQS)) < ROPE_FREQS,
                                    pltpu.roll(y, cw - ROPE_FREQS, 1), pltpu.roll(y, ROPE_FREQS, 1))
                y = y * cos_ref[...] + partner * sin_ref[...]
            if kind == "q":
                y = y * qscale
        elif kind == "sigmoid":
            y = _sigmoid(y)
        if kind in _TRANSPOSED_KINDS:
            y = y.T
        o_ref[0] = y.astype(o_ref.dtype)


def _in_proj(x, sc, sh, g, w, b, qg, kg, bd, rope, spec, tm, qscale):
    B, S, D = x.shape
    N = w.shape[1]
    use_rope = rope is not None
    const2 = lambda i, bb: (0, 0)
    in_specs = [pl.BlockSpec((1, tm, D), lambda i, bb: (bb, i, 0)),
                pl.BlockSpec((1, 1, D), lambda i, bb: (bb, 0, 0)),
                pl.BlockSpec((1, 1, D), lambda i, bb: (bb, 0, 0)),
                pl.BlockSpec((1, D), const2),
                pl.BlockSpec((D, N), const2, pipeline_mode=pl.Buffered(1)),
                pl.BlockSpec((1, N), const2),
                pl.BlockSpec(qg.shape, const2),
                pl.BlockSpec(kg.shape, const2),
                pl.BlockSpec(bd.shape, const2)]
    args = [x, sc, sh, g, w, b, qg, kg, bd]
    if use_rope:
        cw = rope[0].shape[1]
        in_specs += [pl.BlockSpec((tm, cw), lambda i, bb: (i, 0))] * 2
        args += list(rope)
    out_shape = [jax.ShapeDtypeStruct((B, cw_, S) if kind in _TRANSPOSED_KINDS else (B, S, cw_), BF16)
                 for (kind, _, cw_) in spec]
    out_specs = [pl.BlockSpec((1, cw_, tm), lambda i, bb: (bb, 0, i)) if kind in _TRANSPOSED_KINDS
                 else pl.BlockSpec((1, tm, cw_), lambda i, bb: (bb, i, 0)) for (kind, _, cw_) in spec]
    return pl.pallas_call(
        functools.partial(_in_kernel, spec, use_rope, qscale),
        out_shape=out_shape,
        grid=(S // tm, B),
        in_specs=in_specs,
        out_specs=out_specs,
        compiler_params=_cparams("arbitrary", "arbitrary"),
        name="in_proj_rope" if use_rope else "in_proj_ctx",
    )(*args)


def _rope_tables(S, width):
    t = np.arange(S)
    pos = np.stack([t // GRID_W, t % GRID_W], axis=-1).astype(np.float32)
    freqs = (np.float32(ROPE_THETA) ** (-np.arange(ROPE_FREQS, dtype=np.float32) / ROPE_FREQS)).astype(np.float32)
    lane = np.arange(width)
    d = lane % DA_QK_DIM
    axis = d // (2 * ROPE_FREQS)
    half = (d % (2 * ROPE_FREQS)) // ROPE_FREQS
    f = d % ROPE_FREQS
    ang = (pos[:, axis] * freqs[f][None, :]).astype(np.float32)
    cos = np.cos(ang.astype(np.float64)).astype(np.float32)
    sin = np.sin(ang.astype(np.float64)).astype(np.float32)
    sin = np.where(half[None, :] == 0, -sin, sin)
    return jnp.asarray(cos), jnp.asarray(sin)


def _attn_kernel(tq, kc, lam_ref, q_ref, qn_ref, kx_ref, kc_ref, vtx_ref, vtc_ref, g_ref, o_ref,
                 st0, st1, mx0, mx1):
    n_lat = kx_ref.shape[1] // kc
    n_score = n_out = n_lat + kc_ref.shape[1] // kc
    rows_a = kc
    dv = vtx_ref.shape[1]
    ones = jnp.ones((16, kc), BF16)
    lam = lam_ref[0]

    def key_piece(j):
        return kx_ref[0, j * kc:(j + 1) * kc, :] if j < n_lat else kc_ref[0, (j - n_lat) * kc:(j - n_lat + 1) * kc, :]

    def value_piece(j):
        return vtx_ref[0, :, j * kc:(j + 1) * kc] if j < n_lat else vtc_ref[0, :, (j - n_lat) * kc:(j - n_lat + 1) * kc]

    def stacked(qt):
        row = lax.broadcasted_iota(jnp.int32, qt.shape, 0)
        zero = jnp.zeros_like(qt)
        return jnp.concatenate([jnp.where(row < DA_QK_DIM, qt, zero), jnp.where(row >= DA_QK_DIM, qt, zero)],
                               axis=1)

    def score_piece(qs, st, mx, j):
        r0 = j * rows_a
        st[r0:r0 + rows_a, :] = jnp.dot(key_piece(j), qs, preferred_element_type=F32)
        part = jnp.max(st[r0:r0 + rows_a, :].reshape(rows_a // 8, 8, st.shape[-1]), axis=0)
        mx[...] = part if j == 0 else jnp.maximum(mx[...], part)
        return part

    def out_piece(st, m, acc, j):
        e = jnp.exp2(st[j * kc:(j + 1) * kc, :] - m).astype(BF16)
        vt1 = jnp.concatenate([value_piece(j), ones], axis=0)
        d = jnp.dot(vt1, e, preferred_element_type=F32)
        return d if acc is None else acc + d

    def finish(acc, c):
        ot = acc[:dv] * (1.0 / acc[dv:dv + 1])
        o = (ot[:, :tq] - lam * ot[:, tq:]).T
        o = _rms(o, o.shape[-1]) * g_ref[...]
        o_ref[0, c * tq:(c + 1) * tq, :] = o.astype(o_ref.dtype)

    def overlapped(st_cur, mx_cur, q_next, st_next, mx_next, c):
        m = jnp.max(mx_cur[...], axis=0, keepdims=True)
        qs = stacked(q_next)
        acc = None
        p = 0
        parts = []
        for j in range(n_out):
            while p < n_score and p * n_out <= j * n_score:
                parts.append(score_piece(qs, st_next, mx_next, p))
                p += 1
            m_j = m + 0.0 * parts[j - ANCHOR_LAG][0:1, :] if ANCHOR_LAG <= j < len(parts) + ANCHOR_LAG else m
            acc = out_piece(st_cur, m_j, acc, j)
        for p in range(p, n_score):
            score_piece(qs, st_next, mx_next, p)
        finish(acc, c)

    @pl.when(pl.program_id(2) == 0)
    def _():
        qs = stacked(q_ref[0, :, 0:tq])
        for j in range(n_score):
            score_piece(qs, st0, mx0, j)

    overlapped(st0, mx0, q_ref[0, :, tq:2 * tq], st1, mx1, 0)
    overlapped(st1, mx1, qn_ref[0], st0, mx0, 1)


def _attention(lam, qt, kx, kc_, vtx, vtc, g, tq):
    B, W, S = qt.shape
    C = kc_.shape[1]
    Sk = S + C
    H = W // LANES
    tb = 2 * tq
    n_steps = S // tb
    kc = 256 if (S % 256 == 0 and C % 256 == 0) else LANES
    return pl.pallas_call(
        functools.partial(_attn_kernel, tq, kc),
        out_shape=jax.ShapeDtypeStruct((B, S, W), BF16),
        grid=(B, H, n_steps),
        in_specs=[pl.BlockSpec(memory_space=pltpu.SMEM),
                  pl.BlockSpec((1, LANES, tb), lambda b, h, i: (b, h, i)),
                  pl.BlockSpec((1, LANES, tq), lambda b, h, i: (b, h, jnp.minimum(i + 1, n_steps - 1) * 2)),
                  pl.BlockSpec((1, S, LANES), lambda b, h, i: (b, 0, h)),
                  pl.BlockSpec((1, C, LANES), lambda b, h, i: (b, 0, h)),
                  pl.BlockSpec((1, LANES, S), lambda b, h, i: (b, h, 0)),
                  pl.BlockSpec((1, LANES, C), lambda b, h, i: (b, h, 0)),
                  pl.BlockSpec((1, LANES), lambda b, h, i: (0, 0))],
        out_specs=pl.BlockSpec((1, tb, LANES), lambda b, h, i: (b, i, h)),
        scratch_shapes=[pltpu.VMEM((Sk, 2 * tq), F32), pltpu.VMEM((Sk, 2 * tq), F32),
                        pltpu.VMEM((8, 2 * tq), F32), pltpu.VMEM((8, 2 * tq), F32)],
        compiler_params=_cparams("arbitrary", "arbitrary", "arbitrary"),
        name="diff_attn",
    )(lam, qt, qt, kx, kc_, vtx, vtc, g)


def _filter_kernel(z_ref, w1, b1, w2, b2, w3, b3, w4, fr, dec_ref, o_ref):
    dot = functools.partial(jnp.dot, preferred_element_type=F32, precision=HIGHEST)
    f = fr[...]
    h = jnp.sin(f * (dot(z_ref[...], w1[...]) + b1[...]))
    h = jnp.sin(f * (dot(h, w2[...]) + b2[...]))
    h = jnp.sin(f * (dot(h, w3[...]) + b3[...]))
    taps = dot(h, w4[...]) * dec_ref[...]
    tl, W = o_ref.shape[1:]
    o_ref[0] = taps[:, :W]
    pos = lax.broadcasted_iota(jnp.int32, (tl, W), 0) + pl.program_id(0) * tl
    o_ref[1] = jnp.where(pos == 0, 0.0, taps[:, W:])


def _implicit_filter(z, w1, b1, w2, b2, w3, b3, w4, fr, dec, tl):
    L = z.shape[0]
    N = w4.shape[1]
    full = lambda a: pl.BlockSpec(a.shape, lambda i: (0, 0))
    return pl.pallas_call(
        _filter_kernel,
        out_shape=jax.ShapeDtypeStruct((2, L, N // 2), F32),
        grid=(L // tl,),
        in_specs=[pl.BlockSpec((tl, z.shape[1]), lambda i: (i, 0)),
                  full(w1), full(b1), full(w2), full(b2), full(w3), full(b3), full(w4), full(fr),
                  pl.BlockSpec((tl, N), lambda i: (i, 0))],
        out_specs=pl.BlockSpec((2, tl, N // 2), lambda i: (0, i, 0)),
        compiler_params=_cparams("arbitrary"),
        name="hyena_filter",
    )(z, w1, b1, w2, b2, w3, b3, w4, fr, dec)


def _conv3(a, w_ref, b_ref):
    L = a.shape[0]
    row = lax.broadcasted_iota(jnp.int32, a.shape, 0)
    prev = jnp.where(row == 0, 0.0, pltpu.roll(a, 1, 0))
    nxt = jnp.where(row == L - 1, 0.0, pltpu.roll(a, L - 1, 0))
    return prev * w_ref[0:1, :] + a * w_ref[1:2, :] + nxt * w_ref[2:3, :] + b_ref[...]


def _gate_kernel(x1_ref, v_ref, w1_ref, wv_ref, b1_ref, bv_ref, u_ref):
    u = _conv3(x1_ref[0].astype(F32), w1_ref, b1_ref) * _conv3(v_ref[0].astype(F32), wv_ref, bv_ref)
    u_ref[0] = u.astype(u_ref.dtype)


def _hyena_gate(hy, conv_w, conv_b):
    B, L, W3 = hy.shape
    W = W3 // 3
    nw = W // LANES
    act = lambda off: pl.BlockSpec((1, L, LANES), lambda b, j: (b, 0, j + off))
    cw = lambda off: pl.BlockSpec((3, LANES), lambda b, j: (0, j + off))
    cb = lambda off: pl.BlockSpec((1, LANES), lambda b, j: (0, j + off))
    return pl.pallas_call(
        _gate_kernel,
        out_shape=jax.ShapeDtypeStruct((B, L, W), BF16),
        grid=(B, nw),
        in_specs=[act(nw), act(2 * nw), cw(nw), cw(2 * nw), cb(nw), cb(2 * nw)],
        out_specs=pl.BlockSpec((1, L, LANES), lambda b, j: (b, 0, j)),
        compiler_params=_cparams("arbitrary", "arbitrary"),
        name="hyena_gate",
    )(hy, hy, conv_w, conv_w, conv_b, conv_b)


def _dot_split(a, b):
    a_hi, b_hi = a.astype(BF16), b.astype(BF16)
    a_lo = (a - a_hi.astype(F32)).astype(BF16)
    b_lo = (b - b_hi.astype(F32)).astype(BF16)
    dot = functools.partial(jnp.dot, preferred_element_type=F32)
    return dot(a_hi, b_hi) + (dot(a_lo, b_hi) + dot(a_hi, b_lo))


def _outer_dft_kernel(split, f_ref, x_ref, o_ref):
    if split:
        o_ref[0] = _dot_split(f_ref[...], x_ref[0]).astype(o_ref.dtype)
    else:
        o_ref[0] = jnp.dot(f_ref[...], x_ref[0], preferred_element_type=F32).astype(o_ref.dtype)


def _outer_dft(f, x, out_dtype, tn, name):
    B, K, N = x.shape
    M = f.shape[0]
    return pl.pallas_call(
        functools.partial(_outer_dft_kernel, x.dtype == F32),
        out_shape=jax.ShapeDtypeStruct((B, M, N), out_dtype),
        grid=(B, N // tn),
        in_specs=[pl.BlockSpec((M, K), lambda b, j: (0, 0)),
                  pl.BlockSpec((1, K, tn), lambda b, j: (b, 0, j))],
        out_specs=pl.BlockSpec((1, M, tn), lambda b, j: (b, 0, j)),
        compiler_params=_cparams("arbitrary", "arbitrary"),
        name=name,
    )(f, x)


def _outer_fwd_twiddle_kernel(f_ref, x_ref, tr_ref, ti_ref, o_ref):
    a = jnp.dot(f_ref[...], x_ref[0], preferred_element_type=F32)
    n1 = a.shape[0] // 2
    re, im = _twiddle(a[:n1], a[n1:], tr_ref[...], ti_ref[...], False)
    o_ref[0, :n1] = re.astype(o_ref.dtype)
    o_ref[0, n1:] = im.astype(o_ref.dtype)


def _outer_inv_twiddle_kernel(f_ref, x_ref, tr_ref, ti_ref, o_ref):
    n1 = x_ref.shape[1] // 2
    re, im = _twiddle(x_ref[0, :n1].astype(F32), x_ref[0, n1:].astype(F32), tr_ref[...], ti_ref[...], True)
    o_ref[0] = jnp.dot(f_ref[...], jnp.concatenate([re, im], axis=0).astype(BF16), preferred_element_type=F32)


def _outer_twiddled(kern, f, x, tr, ti, out_dtype, tn, name):
    B, K, N = x.shape
    M = f.shape[0]
    tw = pl.BlockSpec((tr.shape[0], tn), lambda j, b: (0, j))
    return pl.pallas_call(
        kern,
        out_shape=jax.ShapeDtypeStruct((B, M, N), out_dtype),
        grid=(N // tn, B),
        in_specs=[pl.BlockSpec((M, K), lambda j, b: (0, 0)),
                  pl.BlockSpec((1, K, tn), lambda j, b: (b, 0, j)), tw, tw],
        out_specs=pl.BlockSpec((1, M, tn), lambda j, b: (b, 0, j)),
        compiler_params=_cparams("arbitrary", "arbitrary"),
        name=name,
    )(f, x, tr, ti)


def _twiddle(re, im, tr, ti, conj):
    if conj:
        return re * tr + im * ti, im * tr - re * ti
    return re * tr - im * ti, re * ti + im * tr


def _inner_spectrum_kernel(kb, a_ref, tr_ref, ti_ref, f_ref, o_ref):
    reps = a_ref.shape[-1] // LANES
    for j in range(kb):
        tr = jnp.concatenate([tr_ref[j]] * reps, axis=-1)
        ti = jnp.concatenate([ti_ref[j]] * reps, axis=-1)
        xs = []
        for s in range(2):
            re, im = _twiddle(a_ref[s, 0, j], a_ref[s, 1, j], tr, ti, False)
            xs.append(_dot_split(f_ref[...], jnp.concatenate([re, im], axis=0)))
        n2 = xs[0].shape[0] // 2
        o_ref[0, 0, j] = xs[0][:n2] + xs[1][:n2]
        o_ref[0, 1, j] = xs[0][n2:] - xs[1][n2:]


def _inner_conv_kernel(kb, a_ref, kf_ref, f_ref, fi_ref, o_ref):
    for j in range(kb):
        n2 = a_ref.shape[3]
        x = jnp.dot(f_ref[...], jnp.concatenate([a_ref[0, 0, j], a_ref[0, 1, j]], axis=0),
                    preferred_element_type=F32)
        xr, xi = x[:n2], x[n2:]
        kr, ki = kf_ref[0, 0, j], kf_ref[0, 1, j]
        yr = xr * kr - xi * ki
        yi = xr * ki + xi * kr
        y = jnp.dot(fi_ref[...], jnp.concatenate([yr, yi], axis=0).astype(BF16), preferred_element_type=F32)
        o_ref[0, 0, j] = y[:n2].astype(o_ref.dtype)
        o_ref[0, 1, j] = y[n2:].astype(o_ref.dtype)


def _inner_stage(a5, kf5, tr, ti, f2, f2i, kb):
    B, _, N1, N2, W = a5.shape
    blk = lambda: pl.BlockSpec((1, 2, kb, N2, W), lambda g, b: (b, 0, g, 0, 0))
    tw = pl.BlockSpec((kb, N2, LANES), lambda g, b: (g, 0, 0))
    mat = pl.BlockSpec((2 * N2, 2 * N2), lambda g, b: (0, 0))
    if kf5 is None:
        kern = functools.partial(_inner_spectrum_kernel, kb)
        pair = pl.BlockSpec((2, 2, kb, N2, W), lambda g, b: (0, 0, g, 0, 0))
        in_specs, args, out_dtype, name = [pair, tw, tw, mat], (a5, tr, ti, f2), F32, "hyena_filter_spectrum"
        B = 1
    else:
        kern = functools.partial(_inner_conv_kernel, kb)
        kf_spec = pl.BlockSpec((1, 2, kb, N2, W), lambda g, b: (0, 0, g, 0, 0))
        in_specs, args, out_dtype, name = ([blk(), kf_spec, mat, mat], (a5, kf5, f2, f2i), BF16,
                                           "hyena_inner_conv")
    return pl.pallas_call(
        kern,
        out_shape=jax.ShapeDtypeStruct((B,) + a5.shape[1:], out_dtype),
        grid=(N1 // kb, B),
        in_specs=in_specs,
        out_specs=blk(),
        compiler_params=_cparams("arbitrary", "arbitrary"),
        name=name,
    )(*args)


def _hyena_out_kernel(x0_ref, w0_ref, b0_ref, y_ref, u_ref, hb_ref, o_ref):
    x0 = _conv3(x0_ref[0].astype(F32), w0_ref, b0_ref)
    o_ref[0] = (x0 * (y_ref[0] + u_ref[0].astype(F32) * hb_ref[...])).astype(o_ref.dtype)


def _hyena_out(hy, conv_w, conv_b, y, u, hbias):
    B, L, W = u.shape
    nw = W // LANES
    blk = pl.BlockSpec((1, L, LANES), lambda b, j: (b, 0, j))
    vec = lambda r: pl.BlockSpec((r, LANES), lambda b, j: (0, j))
    return pl.pallas_call(
        _hyena_out_kernel,
        out_shape=jax.ShapeDtypeStruct((B, L, W), BF16),
        grid=(B, nw),
        in_specs=[blk, vec(3), vec(1), blk, blk, vec(1)],
        out_specs=blk,
        compiler_params=_cparams("arbitrary", "arbitrary"),
        name="hyena_out",
    )(hy, conv_w, conv_b, y, u, hbias)


def _dft_constants(L):
    N = 2 * L
    N2 = FFT_N2
    N1 = N // N2
    k1 = np.arange(N1, dtype=np.float64)
    th1 = 2.0 * np.pi * np.outer(k1, k1) / N1
    fwd = np.concatenate([np.cos(th1), -np.sin(th1)], axis=0)
    inv = np.concatenate([np.cos(th1), -np.sin(th1)], axis=1)[: N1 // 2] / N
    n2 = np.arange(N2, dtype=np.float64)
    tw = 2.0 * np.pi * np.outer(k1, n2) / N
    tr = np.repeat(np.cos(tw)[:, :, None], LANES, axis=2)
    ti = np.repeat(-np.sin(tw)[:, :, None], LANES, axis=2)
    th2 = 2.0 * np.pi * np.outer(n2, n2) / N2
    c2, s2 = np.cos(th2), np.sin(th2)
    f2 = np.block([[c2, s2], [-s2, c2]])
    f2i = np.block([[c2, -s2], [s2, c2]])
    f = lambda a: jnp.asarray(a, F32)
    return dict(N1=N1, N2=N2, fwd=f(fwd), inv=f(inv), tr=f(tr), ti=f(ti), f2=f(f2), f2i=f(f2i))


def _filter_features(L, emb_dim):
    bands = (emb_dim - 1) // 2
    t = np.linspace(0.0, 1.0, L, dtype=np.float32)[:, None]
    w = (np.float32(2.0 * math.pi) * np.arange(L, dtype=np.float32)[:, None] / np.float32(L)).astype(np.float32)
    f = np.linspace(1e-4, bands - 1, bands, dtype=np.float32)
    fw = (f * w).astype(np.float32)
    z = np.concatenate([t, np.cos(fw.astype(np.float64)).astype(np.float32),
                        -np.sin(fw.astype(np.float64)).astype(np.float32)], axis=-1)
    return jnp.asarray(z), t


def _decay_window(t, W):
    deltas = np.abs(np.linspace(math.log(HY_DECAY_TARGET) / HY_SLOW_DECAY,
                                math.log(HY_DECAY_TARGET) / HY_FAST_DECAY, W, dtype=np.float32))
    dec = np.exp((-t * deltas).astype(np.float32).astype(np.float64)).astype(np.float32)
    return jnp.asarray(np.concatenate([dec, dec], axis=1))


def _hyena(hy, conv_w, conv_b, fw1, fb1, fw2, fb2, fw3, fb3, fw4, ffreq, hbias):
    B, L, W3 = hy.shape
    W = W3 // 3
    C = _dft_constants(L)
    N1, N2 = C["N1"], C["N2"]
    row = lambda a: a.reshape(1, -1)
    z, t = _filter_features(L, fw1.shape[0])
    kpad = LANES - fw1.shape[0]
    z = jnp.pad(z, ((0, 0), (0, kpad)))
    taps = _implicit_filter(z, jnp.pad(fw1, ((0, kpad), (0, 0))), row(fb1), fw2, row(fb2), fw3, row(fb3), fw4,
                            row(ffreq),
                            _decay_window(t, W), tl=min(L, 512))
    lanes = N2 * W
    tn = min(lanes, 8192)
    kf = _outer_dft(C["fwd"][:, : N1 // 2], taps.reshape(2, N1 // 2, lanes), F32, tn, "hyena_filter_outer")
    kf5 = _inner_stage(kf.reshape(2, 2, N1, N2, W), None, C["tr"], C["ti"], C["f2"], None, kb=min(N1, 2))
    u = _hyena_gate(hy, conv_w, row(conv_b))
    tr_o = jnp.repeat(C["tr"][:, :, 0], W, axis=1)
    ti_o = jnp.repeat(C["ti"][:, :, 0], W, axis=1)
    a = _outer_twiddled(_outer_fwd_twiddle_kernel, C["fwd"][:, : N1 // 2].astype(BF16),
                        u.reshape(B, N1 // 2, lanes), tr_o, ti_o, BF16, tn, "hyena_outer_fwd")
    bb = _inner_stage(a.reshape(B, 2, N1, N2, W), kf5, C["tr"], C["ti"], C["f2"].astype(BF16),
                      C["f2i"].astype(BF16), kb=min(N1, 8))
    y = _outer_twiddled(_outer_inv_twiddle_kernel, C["inv"].astype(BF16), bb.reshape(B, 2 * N1, lanes),
                        tr_o, ti_o, F32, tn, "hyena_outer_inv")
    return _hyena_out(hy, conv_w, row(conv_b), y.reshape(B, L, W), u, row(hbias))


ROUTE_ROWS = 16


def _route_tile(h, rw_ref, rb_ref, tri_ref, carry_ref):
    tm = h.shape[0]
    E = rw_ref.shape[0] // 2
    nt = (((1,), (1,)), ((), ()))
    h_hi = h.astype(BF16)
    h_lo = (h - h_hi.astype(F32)).astype(BF16)
    a = lax.dot_general(rw_ref[...], h_hi, nt, preferred_element_type=F32)
    b = lax.dot_general(rw_ref[0:E, :], h_lo, nt, preferred_element_type=F32)
    reps = tm // LANES
    logits = a[:E] + a[E:] + b + jnp.concatenate([rb_ref[...]] * reps, axis=1)
    row = lax.broadcasted_iota(jnp.int32, (E, tm), 0)
    v = logits
    ids, vals = [], []
    for _ in range(TOP_K):
        m = jnp.max(v, axis=0, keepdims=True)
        idx = jnp.min(jnp.where(v == m, row, E), axis=0, keepdims=True)
        ids.append(idx)
        vals.append(m)
        v = jnp.where(row == idx, -jnp.inf, v)
    ex = [jnp.exp(vk - vals[0]) for vk in vals]
    den = ex[0]
    for t in ex[1:]:
        den = den + t
    hot = [row == idx for idx in ids]
    member = jnp.zeros((E, tm), F32)
    for hk in hot:
        member = jnp.where(hk, 1.0, member)
    member = member.astype(BF16)
    carry = carry_ref[...]
    before = (jnp.dot(member, tri_ref[...], preferred_element_type=F32)
              + jnp.concatenate([carry] * reps, axis=1))
    ranks = [jnp.sum(jnp.where(hk, before, 0.0), axis=0, keepdims=True) for hk in hot]
    carry_ref[...] = carry + jnp.dot(member, jnp.ones((tm, LANES), BF16), preferred_element_type=F32)
    out_row = lax.broadcasted_iota(jnp.int32, (ROUTE_ROWS, tm), 0)
    rec = jnp.zeros((ROUTE_ROWS, tm), F32)
    for k in range(TOP_K):
        rec = jnp.where(out_row == k, ids[k].astype(F32), rec)
        rec = jnp.where(out_row == TOP_K + k, ex[k] / den, rec)
        rec = jnp.where(out_row == 2 * TOP_K + k, ranks[k], rec)
    return rec


def _merge_kernel(x_ref, oa_ref, ob_ref, gt_ref, g1_ref, sc_ref, sh_ref, ng_ref, wa_ref, wb_ref, wo_ref,
                  rw_ref, rb_ref, tri_ref, xo_ref, h_ref, rec_ref, cnt_ref, before_ref, carry_ref):
    @pl.when((pl.program_id(0) == 0) & (pl.program_id(1) == 0))
    def _():
        carry_ref[...] = jnp.zeros_like(carry_ref)

    @pl.when(pl.program_id(1) == 0)
    def _():
        before_ref[0] = carry_ref[...]

    D = x_ref.shape[-1]
    gt = gt_ref[0]
    m = (gt[:, :D].astype(F32) * jnp.dot(oa_ref[0], wa_ref[...], preferred_element_type=F32)
         + gt[:, D:].astype(F32) * jnp.dot(ob_ref[0], wb_ref[...], preferred_element_type=F32))
    y = jnp.dot(m.astype(BF16), wo_ref[...], preferred_element_type=F32)
    xn = x_ref[0] + g1_ref[0] * y
    xo_ref[0] = xn
    h = _rms(xn, D) * ng_ref[...]
    h = h * (1.0 + sc_ref[0]) + sh_ref[0]
    h_ref[0] = h.astype(h_ref.dtype)
    rec_ref[0] = _route_tile(h, rw_ref, rb_ref, tri_ref, carry_ref)
    cnt_ref[...] = carry_ref[...]


def _merge(x, oa, ob, gates, g1, sc2, sh2, ng, wa, wb, wo, rw, rb, tm):
    B, S, D = x.shape
    E = rw.shape[1]
    rw_hi = rw.T.astype(BF16)
    rw2 = jnp.concatenate([rw_hi, (rw.T - rw_hi.astype(F32)).astype(BF16)], axis=0)
    rb2 = jnp.broadcast_to(rb.reshape(E, 1), (E, LANES))
    tri = jnp.asarray(np.triu(np.ones((tm, tm), np.float32), 1), BF16)
    act = lambda w: pl.BlockSpec((1, tm, w), lambda b, i: (b, i, 0))
    mod = pl.BlockSpec((1, 1, D), lambda b, i: (b, 0, 0))
    full = lambda a: pl.BlockSpec(a.shape, lambda b, i: (0, 0))
    return pl.pallas_call(
        _merge_kernel,
        out_shape=[jax.ShapeDtypeStruct((B, S, D), F32), jax.ShapeDtypeStruct((B, S, D), BF16),
                   jax.ShapeDtypeStruct((B, ROUTE_ROWS, S), F32), jax.ShapeDtypeStruct((E, LANES), F32),
                   jax.ShapeDtypeStruct((B, E, LANES), F32)],
        grid=(B, S // tm),
        in_specs=[act(D), act(oa.shape[-1]), act(ob.shape[-1]), act(2 * D), mod, mod, mod, full(ng),
                  full(wa), full(wb), full(wo), full(rw2), full(rb2), full(tri)],
        out_specs=[act(D), act(D), pl.BlockSpec((1, ROUTE_ROWS, tm), lambda b, i: (b, 0, i)),
                   pl.BlockSpec((E, LANES), lambda b, i: (0, 0)),
                   pl.BlockSpec((1, E, LANES), lambda b, i: (b, 0, 0))],
        scratch_shapes=[pltpu.VMEM((E, LANES), F32)],
        compiler_params=_cparams("arbitrary", "arbitrary"),
        name="merge_norm_router",
    )(x, oa, ob, gates, g1, sc2, sh2, ng, wa, wb, wo, rw2, rb2, tri)


def _moe_kernel(be_ref, nu_ref, x_ref, w1_ref, b1_ref, w2_ref, b2_ref, o_ref, w1b_ref, w2b_ref):
    i = pl.program_id(0)

    @pl.when(i < nu_ref[0])
    def _():
        @pl.when((i == 0) | (be_ref[i] != be_ref[jnp.maximum(i - 1, 0)]))
        def _():
            w1b_ref[...] = w1_ref[0].astype(BF16)
            w2b_ref[...] = w2_ref[0].astype(BF16)

        F = w2_ref.shape[1]
        gu = jnp.dot(x_ref[...], w1b_ref[...], preferred_element_type=F32) + b1_ref[0]
        gate = jnp.minimum(gu[:, :F], SWIGLU_LIMIT)
        up = jnp.clip(gu[:, F:], -SWIGLU_LIMIT, SWIGLU_LIMIT)
        act = (up + 1.0) * gate * _sigmoid(SWIGLU_ALPHA * gate)
        y = jnp.dot(act.astype(BF16), w2b_ref[...], preferred_element_type=F32) + b2_ref[0]
        o_ref[...] = y.astype(o_ref.dtype)


def _moe_experts(block_e, n_used, xg, w1, b1, w2, b2, bm):
    R, D = xg.shape
    E, _, F2 = w1.shape
    F = w2.shape[1]
    grid_spec = pltpu.PrefetchScalarGridSpec(
        num_scalar_prefetch=2,
        grid=(R // bm,),
        in_specs=[pl.BlockSpec((bm, D), lambda i, be, nu: (i, 0)),
                  pl.BlockSpec((1, D, F2), lambda i, be, nu: (be[i], 0, 0)),
                  pl.BlockSpec((1, 1, F2), lambda i, be, nu: (be[i], 0, 0)),
                  pl.BlockSpec((1, F, D), lambda i, be, nu: (be[i], 0, 0)),
                  pl.BlockSpec((1, 1, D), lambda i, be, nu: (be[i], 0, 0))],
        out_specs=pl.BlockSpec((bm, D), lambda i, be, nu: (i, 0)),
        scratch_shapes=[pltpu.VMEM((D, F2), BF16), pltpu.VMEM((F, D), BF16)],
    )
    return pl.pallas_call(
        _moe_kernel,
        out_shape=jax.ShapeDtypeStruct((R, D), BF16),
        grid_spec=grid_spec,
        compiler_params=pltpu.CompilerParams(dimension_semantics=("arbitrary",),
                                             vmem_limit_bytes=V7X_VMEM_LIMIT_MOE_BYTES),
        name="moe_experts",
    )(block_e, n_used, xg, w1, b1.reshape(E, 1, F2), w2, b2.reshape(E, 1, D))


def _combine_kernel(nb, x_ref, g_ref, *refs):
    o_ref = refs[-1]
    b = pl.program_id(0)
    for grp in range((len(refs) - 1) // 2):
        y_ref, w_ref = refs[2 * grp:2 * grp + 2]

        @pl.when((b >= grp * nb) & (b < (grp + 1) * nb))
        def _():
            w = w_ref[0]
            acc = w[:, 0:1] * y_ref[0, 0].astype(F32)
            for k in range(1, y_ref.shape[0]):
                acc = acc + w[:, k:k + 1] * y_ref[k, 0].astype(F32)
            o_ref[0] = x_ref[0] + g_ref[0] * acc


def _combine(x, ygs, wtss, g2, tm):
    B, S, D = x.shape
    K, nb = ygs[0].shape[:2]
    in_specs = [pl.BlockSpec((1, tm, D), lambda b, i: (b, i, 0)), pl.BlockSpec((1, 1, D), lambda b, i: (b, 0, 0))]
    args = [x, g2]

    def group_block(grp, b, i):
        mine = (b >= grp * nb) & (b < (grp + 1) * nb)
        return jnp.where(mine, b - grp * nb, 0), jnp.where(mine, i, 0)

    for grp in range(len(ygs)):
        at = functools.partial(group_block, grp)
        in_specs += [pl.BlockSpec((K, 1, tm, D), lambda b, i, at=at: (0, *at(b, i), 0)),
                     pl.BlockSpec((1, tm, K), lambda b, i, at=at: (*at(b, i), 0))]
        args += [ygs[grp], wtss[grp]]
    return pl.pallas_call(
        functools.partial(_combine_kernel, nb),
        out_shape=jax.ShapeDtypeStruct((B, S, D), F32),
        grid=(B, S // tm),
        in_specs=in_specs,
        out_specs=pl.BlockSpec((1, tm, D), lambda b, i: (b, i, 0)),
        compiler_params=_cparams("arbitrary", "arbitrary"),
        name="moe_combine",
    )(*args)


def _route(rec, counts, before, tok0, bm):
    B, _, S = rec.shape
    T = B * S
    E = counts.shape[0]
    I32 = jnp.int32
    experts = jnp.arange(E, dtype=I32)
    pick = lambda table, idx: jnp.sum(jnp.where(idx[..., None] == experts, table, 0), axis=-1)
    rec = jnp.swapaxes(rec, 1, 2).reshape(T, ROUTE_ROWS)
    top_i = rec[:, 0:TOP_K].astype(I32)
    wts = rec[:, TOP_K:2 * TOP_K]
    rank = rec[:, 2 * TOP_K:3 * TOP_K].astype(I32) - pick(before.astype(I32), top_i)
    counts = counts.astype(I32)
    A = T * TOP_K
    padded = (counts + bm - 1) // bm * bm
    pend = jnp.cumsum(padded)
    pstart = pend - padded
    start = jnp.cumsum(counts) - counts
    pos = pick(pstart, top_i) + rank
    keys = top_i.reshape(-1) * A + jnp.arange(A, dtype=I32)
    order = jnp.sort(keys) % A
    n_rows = A + E * bm
    n_blocks = n_rows // bm
    blk_start = jnp.arange(n_blocks, dtype=I32) * bm
    block_e = jnp.minimum(jnp.sum((pend[None, :] <= blk_start[:, None]).astype(I32), axis=1), E - 1)
    off = blk_start[:, None] + jnp.arange(bm, dtype=I32)[None, :] - pick(pstart, block_e)[:, None]
    valid = (off >= 0) & (off < pick(counts, block_e)[:, None])
    src = jnp.clip(pick(start, block_e)[:, None] + off, 0, A - 1)
    sorted_tok = order.at[src.reshape(-1)].get(mode="promise_in_bounds").reshape(n_blocks, bm) // TOP_K
    row_tok = jnp.where(valid, sorted_tok, (blk_start[:, None] + jnp.arange(bm, dtype=I32)[None, :]) % T)
    n_used = (pend[-1:] // bm).astype(I32)
    return wts, row_tok.reshape(-1) + tok0, pos, block_e.astype(I32), n_used


def kernel(x, c, ctx, c_ctx, ada_w, ada_b, norm1_g, norm2_g, w_in, b_in, q_norm_g, k_norm_g, lambda_q1, lambda_k1, lambda_q2, lambda_k2, subln_g, conv_w, conv_b, filt_w1, filt_b1, filt_w2, filt_b2, filt_w3, filt_b3, filt_w4, filt_freq, hyena_bias, w_up_a, w_up_b, w_out, router_w, router_b, exp_w1, exp_b1, exp_w2, exp_b2):
    B, S, D = x.shape
    depth = ada_w.shape[0]
    assert depth == 1, "context-stream update between layers is not implemented"
    l = 0
    QC = DA_HEADS * 2 * DA_QK_DIM
    W = D // 2
    row = lambda a: a.reshape(1, -1)

    lambda_init = 0.8 - 0.6 * math.exp(-0.3 * l)
    lam = (jnp.exp(jnp.sum(lambda_q1[l] * lambda_k1[l])) - jnp.exp(jnp.sum(lambda_q2[l] * lambda_k2[l]))
           + lambda_init).reshape(1).astype(F32)

    R = -(-(B + 1) // 8) * 8
    cc = jnp.concatenate([c, c_ctx[None, :], jnp.zeros((R - B - 1, D), F32)], axis=0)
    mods = _ada(cc, ada_w[l], row(ada_b[l]))
    sh1, sc1, g1, sh2, sc2, g2 = [mods[:B, i * D:(i + 1) * D].reshape(B, 1, D) for i in range(6)]
    sh1c, sc1c = [jnp.broadcast_to(mods[B, i * D:(i + 1) * D].reshape(1, 1, D), (B, 1, D)) for i in range(2)]

    w_in_b = w_in[l].astype(BF16)
    qg = row(jnp.tile(q_norm_g[l], QC // DA_QK_DIM))
    kg = row(jnp.tile(k_norm_g[l], QC // DA_QK_DIM))
    grp = np.arange(QC) // DA_QK_DIM
    bd = jnp.asarray(grp[:, None] == grp[None, :], BF16)
    qscale = DA_QK_DIM ** -0.5 * math.log2(math.e)
    spec = (("q", 0, QC), ("k", QC, QC), ("transposed", 2 * QC, QC), ("plain", 3 * QC, 3 * W),
            ("sigmoid", 3 * QC + 3 * W, 2 * D))
    qtx, kx, vtx, hy, gates = _in_proj(x, sc1, sh1, row(norm1_g[l]), w_in_b, row(b_in[l]), qg, kg, bd,
                                       _rope_tables(S, QC), spec, tm=min(S, 512), qscale=qscale)
    spec_c = (("kc", 0, QC), ("transposed", QC, QC))
    kc, vtc = _in_proj(ctx, sc1c, sh1c, row(norm1_g[l]), w_in_b[:, QC:3 * QC], row(b_in[l][QC:3 * QC]), qg,
                       kg, bd, None, spec_c, tm=ctx.shape[1], qscale=qscale)

    oa = _attention(lam, qtx, kx, kc, vtx, vtc, row(subln_g[l] * (1.0 - lambda_init)), tq=min(S // 2, 256))
    ob = _hyena(hy, conv_w[l], conv_b[l], filt_w1[l], filt_b1[l], filt_w2[l], filt_b2[l], filt_w3[l],
                filt_b3[l], filt_w4[l], filt_freq[l], hyena_bias[l])
    x_new, h2, rec, counts, before = _merge(x, oa, ob, gates, g1, sc2, sh2, row(norm2_g[l]),
                                            w_up_a[l].astype(BF16), w_up_b[l].astype(BF16), w_out[l].astype(BF16),
                                            router_w[l], row(router_b[l]), tm=min(S, 512))

    groups = MOE_BATCH_GROUPS if B % MOE_BATCH_GROUPS == 0 else 1
    nb = B // groups
    bm = EXPERT_BLOCK
    edges = jnp.concatenate([before[:, :, 0], counts[None, :, 0]], axis=0)
    ygs, wtss = [], []
    for grp in range(groups):
        b0 = grp * nb
        wts, row_tok, pos, block_e, n_used = _route(rec[b0:b0 + nb], edges[b0 + nb] - edges[b0], edges[b0],
                                                    b0 * S, bm)
        xg = h2.reshape(B * S, D).at[row_tok].get(mode="promise_in_bounds")
        yb = _moe_experts(block_e, n_used, xg, exp_w1[l], exp_b1[l], exp_w2[l], exp_b2[l], bm)
        ygs.append(yb.at[pos.T].get(mode="promise_in_bounds").reshape(TOP_K, nb, S, D))
        wtss.append(wts.reshape(nb, S, TOP_K))
    return _combine(x_new, ygs, wtss, g2, tm=min(S, 512))
```

```python
import functools
import math

import jax
import jax.numpy as jnp
import numpy as np
from jax import lax
from jax.experimental import pallas as pl
from jax.experimental.pallas import tpu as pltpu

F32 = jnp.float32
BF16 = jnp.bfloat16
HIGHEST = lax.Precision.HIGHEST

GRID_W = 64
DA_HEADS = 4
DA_QK_DIM = 64
DA_V_DIM = 2 * DA_QK_DIM
ROPE_THETA = 10000.0
ROPE_FREQS = DA_QK_DIM // 4
HY_DECAY_TARGET = 1e-2
HY_FAST_DECAY = 0.3
HY_SLOW_DECAY = 1.5
TOP_K = 4
SWIGLU_LIMIT = 7.0
SWIGLU_ALPHA = 1.702
EPS = 1e-6

LANES = 128
V7X_VMEM_LIMIT_BYTES = 48 * 1024 * 1024
V7X_VMEM_LIMIT_MOE_BYTES = 56 * 1024 * 1024
FFT_N2 = 128
ANCHOR_LAG = 2
EXPERT_BLOCK = 512
MOE_BATCH_GROUPS = 1


def _cparams(*sem):
    return pltpu.CompilerParams(dimension_semantics=sem, vmem_limit_bytes=V7X_VMEM_LIMIT_BYTES)


def _sigmoid(x):
    return 1.0 / (1.0 + jnp.exp(-x))


def _rms(x, width):
    return x * lax.rsqrt(jnp.sum(x * x, axis=-1, keepdims=True) * (1.0 / width) + EPS)


def _ada_kernel(c_ref, w_ref, b_ref, o_ref):
    c = c_ref[...]
    s = c * _sigmoid(c)
    o_ref[...] = jnp.dot(s, w_ref[...], preferred_element_type=F32, precision=HIGHEST) + b_ref[...]


def _ada(cc, w, b):
    R, D = cc.shape
    N = w.shape[1]
    tn = D
    return pl.pallas_call(
        _ada_kernel,
        out_shape=jax.ShapeDtypeStruct((R, N), F32),
        grid=(N // tn,),
        in_specs=[pl.BlockSpec((R, D), lambda j: (0, 0)),
                  pl.BlockSpec((D, tn), lambda j: (0, j)),
                  pl.BlockSpec((1, tn), lambda j: (0, j))],
        out_specs=pl.BlockSpec((R, tn), lambda j: (0, j)),
        compiler_params=_cparams("arbitrary"),
        name="ada",
    )(cc, w, b)


_TRANSPOSED_KINDS = ("q", "transposed")


def _in_kernel(spec, use_rope, qscale, x_ref, sc_ref, sh_ref, g_ref, w_ref, b_ref, qg_ref, kg_ref,
               bd_ref, *rest):
    if use_rope:
        cos_ref, sin_ref, *outs = rest
    else:
        outs = rest
    x = x_ref[0]
    h = _rms(x, x.shape[-1]) * g_ref[...]
    h = (h * (1.0 + sc_ref[0]) + sh_ref[0]).astype(BF16)
    for (kind, c0, cw), o_ref in zip(spec, outs):
        y = jnp.dot(h, w_ref[:, c0:c0 + cw], preferred_element_type=F32) + b_ref[:, c0:c0 + cw]
        if kind in ("q", "k", "kc"):
            ssq = jnp.dot((y * y).astype(BF16), bd_ref[...], preferred_element_type=F32)
            gn = qg_ref if kind == "q" else kg_ref
            y = y * lax.rsqrt(ssq * (1.0 / DA_QK_DIM) + EPS) * gn[...]
            if kind != "kc":
                lane = lax.broadcasted_iota(jnp.int32, y.shape, 1)
                partner = jnp.where((lane % (2 * ROPE_FREQS)) < ROPE_FREQS,
                                    pltpu.roll(y, cw - ROPE_FREQS, 1), pltpu.roll(y, ROPE_FREQS, 1))
                y = y * cos_ref[...] + partner * sin_ref[...]
            if kind == "q":
                y = y * qscale
        elif kind == "sigmoid":
            y = _sigmoid(y)
        if kind in _TRANSPOSED_KINDS:
            y = y.T
        o_ref[0] = y.astype(o_ref.dtype)


def _in_proj(x, sc, sh, g, w, b, qg, kg, bd, rope, spec, tm, qscale):
    B, S, D = x.shape
    N = w.shape[1]
    use_rope = rope is not None
    const2 = lambda i, bb: (0, 0)
    in_specs = [pl.BlockSpec((1, tm, D), lambda i, bb: (bb, i, 0)),
                pl.BlockSpec((1, 1, D), lambda i, bb: (bb, 0, 0)),
                pl.BlockSpec((1, 1, D), lambda i, bb: (bb, 0, 0)),
                pl.BlockSpec((1, D), const2),
                pl.BlockSpec((D, N), const2, pipeline_mode=pl.Buffered(1)),
                pl.BlockSpec((1, N), const2),
                pl.BlockSpec(qg.shape, const2),
                pl.BlockSpec(kg.shape, const2),
                pl.BlockSpec(bd.shape, const2)]
    args = [x, sc, sh, g, w, b, qg, kg, bd]
    if use_rope:
        cw = rope[0].shape[1]
        in_specs += [pl.BlockSpec((tm, cw), lambda i, bb: (i, 0))] * 2
        args += list(rope)
    out_shape = [jax.ShapeDtypeStruct((B, cw_, S) if kind in _TRANSPOSED_KINDS else (B, S, cw_), BF16)
                 for (kind, _, cw_) in spec]
    out_specs = [pl.BlockSpec((1, cw_, tm), lambda i, bb: (bb, 0, i)) if kind in _TRANSPOSED_KINDS
                 else pl.BlockSpec((1, tm, cw_), lambda i, bb: (bb, i, 0)) for (kind, _, cw_) in spec]
    return pl.pallas_call(
        functools.partial(_in_kernel, spec, use_rope, qscale),
        out_shape=out_shape,
        grid=(S // tm, B),
        in_specs=in_specs,
        out_specs=out_specs,
        compiler_params=_cparams("arbitrary", "arbitrary"),
        name="in_proj_rope" if use_rope else "in_proj_ctx",
    )(*args)


def _rope_tables(S, width):
    t = np.arange(S)
    pos = np.stack([t // GRID_W, t % GRID_W], axis=-1).astype(np.float32)
    freqs = (np.float32(ROPE_THETA) ** (-np.arange(ROPE_FREQS, dtype=np.float32) / ROPE_FREQS)).astype(np.float32)
    lane = np.arange(width)
    d = lane % DA_QK_DIM
    axis = d // (2 * ROPE_FREQS)
    half = (d % (2 * ROPE_FREQS)) // ROPE_FREQS
    f = d % ROPE_FREQS
    ang = (pos[:, axis] * freqs[f][None, :]).astype(np.float32)
    cos = np.cos(ang.astype(np.float64)).astype(np.float32)
    sin = np.sin(ang.astype(np.float64)).astype(np.float32)
    sin = np.where(half[None, :] == 0, -sin, sin)
    return jnp.asarray(cos), jnp.asarray(sin)


def _attn_kernel(tq, kc, lam_ref, q_ref, qn_ref, kx_ref, kc_ref, vtx_ref, vtc_ref, g_ref, o_ref,
                 st0, st1, mx0, mx1):
    n_lat = kx_ref.shape[1] // kc
    n_score = n_out = n_lat + kc_ref.shape[1] // kc
    rows_a = kc
    dv = vtx_ref.shape[1]
    ones = jnp.ones((16, kc), BF16)
    lam = lam_ref[0]

    def key_piece(j):
        return kx_ref[0, j * kc:(j + 1) * kc, :] if j < n_lat else kc_ref[0, (j - n_lat) * kc:(j - n_lat + 1) * kc, :]

    def value_piece(j):
        return vtx_ref[0, :, j * kc:(j + 1) * kc] if j < n_lat else vtc_ref[0, :, (j - n_lat) * kc:(j - n_lat + 1) * kc]

    def stacked(qt):
        row = lax.broadcasted_iota(jnp.int32, qt.shape, 0)
        zero = jnp.zeros_like(qt)
        return jnp.concatenate([jnp.where(row < DA_QK_DIM, qt, zero), jnp.where(row >= DA_QK_DIM, qt, zero)],
                               axis=1)

    def score_piece(qs, st, mx, j):
        r0 = j * rows_a
        st[r0:r0 + rows_a, :] = jnp.dot(key_piece(j), qs, preferred_element_type=F32)
        part = jnp.max(st[r0:r0 + rows_a, :].reshape(rows_a // 8, 8, st.shape[-1]), axis=0)
        mx[...] = part if j == 0 else jnp.maximum(mx[...], part)
        return part

    def out_piece(st, m, acc, j):
        e = jnp.exp2(st[j * kc:(j + 1) * kc, :] - m).astype(BF16)
        vt1 = jnp.concatenate([value_piece(j), ones], axis=0)
        d = jnp.dot(vt1, e, preferred_element_type=F32)
        return d if acc is None else acc + d

    def finish(acc, c):
        ot = acc[:dv] * (1.0 / acc[dv:dv + 1])
        o = (ot[:, :tq] - lam * ot[:, tq:]).T
        o = _rms(o, o.shape[-1]) * g_ref[...]
        o_ref[0, c * tq:(c + 1) * tq, :] = o.astype(o_ref.dtype)

    def overlapped(st_cur, mx_cur, q_next, st_next, mx_next, c):
        m = jnp.max(mx_cur[...], axis=0, keepdims=True)
        qs = stacked(q_next)
        acc = None
        p = 0
        parts = []
        for j in range(n_out):
            while p < n_score and p * n_out <= j * n_score:
                parts.append(score_piece(qs, st_next, mx_next, p))
                p += 1
            m_j = m + 0.0 * parts[j - ANCHOR_LAG][0:1, :] if ANCHOR_LAG <= j < len(parts) + ANCHOR_LAG else m
            acc = out_piece(st_cur, m_j, acc, j)
        for p in range(p, n_score):
            score_piece(qs, st_next, mx_next, p)
        finish(acc, c)

    @pl.when(pl.program_id(2) == 0)
    def _():
        qs = stacked(q_ref[0, :, 0:tq])
        for j in range(n_score):
            score_piece(qs, st0, mx0, j)

    overlapped(st0, mx0, q_ref[0, :, tq:2 * tq], st1, mx1, 0)
    overlapped(st1, mx1, qn_ref[0], st0, mx0, 1)


def _attention(lam, qt, kx, kc_, vtx, vtc, g, tq):
    B, W, S = qt.shape
    C = kc_.shape[1]
    Sk = S + C
    H = W // LANES
    tb = 2 * tq
    n_steps = S // tb
    kc = 256 if (S % 256 == 0 and C % 256 == 0) else LANES
    return pl.pallas_call(
        functools.partial(_attn_kernel, tq, kc),
        out_shape=jax.ShapeDtypeStruct((B, S, W), BF16),
        grid=(B, H, n_steps),
        in_specs=[pl.BlockSpec(memory_space=pltpu.SMEM),
                  pl.BlockSpec((1, LANES, tb), lambda b, h, i: (b, h, i)),
                  pl.BlockSpec((1, LANES, tq), lambda b, h, i: (b, h, jnp.minimum(i + 1, n_steps - 1) * 2)),
                  pl.BlockSpec((1, S, LANES), lambda b, h, i: (b, 0, h)),
                  pl.BlockSpec((1, C, LANES), lambda b, h, i: (b, 0, h)),
                  pl.BlockSpec((1, LANES, S), lambda b, h, i: (b, h, 0)),
                  pl.BlockSpec((1, LANES, C), lambda b, h, i: (b, h, 0)),
                  pl.BlockSpec((1, LANES), lambda b, h, i: (0, 0))],
        out_specs=pl.BlockSpec((1, tb, LANES), lambda b, h, i: (b, i, h)),
        scratch_shapes=[pltpu.VMEM((Sk, 2 * tq), F32), pltpu.VMEM((Sk, 2 * tq), F32),
                        pltpu.VMEM((8, 2 * tq), F32), pltpu.VMEM((8, 2 * tq), F32)],
        compiler_params=_cparams("arbitrary", "arbitrary", "arbitrary"),
        name="diff_attn",
    )(lam, qt, qt, kx, kc_, vtx, vtc, g)


def _filter_kernel(z_ref, w1, b1, w2, b2, w3, b3, w4, fr, dec_ref, o_ref):
    dot = functools.partial(jnp.dot, preferred_element_type=F32, precision=HIGHEST)
    f = fr[...]
    h = jnp.sin(f * (dot(z_ref[...], w1[...]) + b1[...]))
    h = jnp.sin(f * (dot(h, w2[...]) + b2[...]))
    h = jnp.sin(f * (dot(h, w3[...]) + b3[...]))
    taps = _dot_split(h, w4[...]) * dec_ref[...]
    tl, W = o_ref.shape[1:]
    o_ref[0] = taps[:, :W]
    pos = lax.broadcasted_iota(jnp.int32, (tl, W), 0) + pl.program_id(0) * tl
    o_ref[1] = jnp.where(pos == 0, 0.0, taps[:, W:])


def _implicit_filter(z, w1, b1, w2, b2, w3, b3, w4, fr, dec, tl):
    L = z.shape[0]
    N = w4.shape[1]
    full = lambda a: pl.BlockSpec(a.shape, lambda i: (0, 0))
    return pl.pallas_call(
        _filter_kernel,
        out_shape=jax.ShapeDtypeStruct((2, L, N // 2), F32),
        grid=(L // tl,),
        in_specs=[pl.BlockSpec((tl, z.shape[1]), lambda i: (i, 0)),
                  full(w1), full(b1), full(w2), full(b2), full(w3), full(b3), full(w4), full(fr),
                  pl.BlockSpec((tl, N), lambda i: (i, 0))],
        out_specs=pl.BlockSpec((2, tl, N // 2), lambda i: (0, i, 0)),
        compiler_params=_cparams("arbitrary"),
        name="hyena_filter",
    )(z, w1, b1, w2, b2, w3, b3, w4, fr, dec)


def _conv3(a, w_ref, b_ref):
    L = a.shape[0]
    row = lax.broadcasted_iota(jnp.int32, a.shape, 0)
    prev = jnp.where(row == 0, 0.0, pltpu.roll(a, 1, 0))
    nxt = jnp.where(row == L - 1, 0.0, pltpu.roll(a, L - 1, 0))
    return prev * w_ref[0:1, :] + a * w_ref[1:2, :] + nxt * w_ref[2:3, :] + b_ref[...]


def _gate_kernel(x1_ref, v_ref, w1_ref, wv_ref, b1_ref, bv_ref, u_ref):
    u = _conv3(x1_ref[0].astype(F32), w1_ref, b1_ref) * _conv3(v_ref[0].astype(F32), wv_ref, bv_ref)
    u_ref[0] = u.astype(u_ref.dtype)


def _hyena_gate(hy, conv_w, conv_b):
    B, L, W3 = hy.shape
    W = W3 // 3
    nw = W // LANES
    act = lambda off: pl.BlockSpec((1, L, LANES), lambda b, j: (b, 0, j + off))
    cw = lambda off: pl.BlockSpec((3, LANES), lambda b, j: (0, j + off))
    cb = lambda off: pl.BlockSpec((1, LANES), lambda b, j: (0, j + off))
    return pl.pallas_call(
        _gate_kernel,
        out_shape=jax.ShapeDtypeStruct((B, L, W), BF16),
        grid=(B, nw),
        in_specs=[act(nw), act(2 * nw), cw(nw), cw(2 * nw), cb(nw), cb(2 * nw)],
        out_specs=pl.BlockSpec((1, L, LANES), lambda b, j: (b, 0, j)),
        compiler_params=_cparams("arbitrary", "arbitrary"),
        name="hyena_gate",
    )(hy, hy, conv_w, conv_w, conv_b, conv_b)


def _dot_split(a, b):
    a_hi, b_hi = a.astype(BF16), b.astype(BF16)
    a_lo = (a - a_hi.astype(F32)).astype(BF16)
    b_lo = (b - b_hi.astype(F32)).astype(BF16)
    dot = functools.partial(jnp.dot, preferred_element_type=F32)
    return dot(a_hi, b_hi) + (dot(a_lo, b_hi) + dot(a_hi, b_lo))


def _outer_dft_kernel(split, f_ref, x_ref, o_ref):
    if split:
        o_ref[0] = _dot_split(f_ref[...], x_ref[0]).astype(o_ref.dtype)
    else:
        o_ref[0] = jnp.dot(f_ref[...], x_ref[0], preferred_element_type=F32).astype(o_ref.dtype)


def _outer_dft(f, x, out_dtype, tn, name):
    B, K, N = x.shape
    M = f.shape[0]
    return pl.pallas_call(
        functools.partial(_outer_dft_kernel, x.dtype == F32),
        out_shape=jax.ShapeDtypeStruct((B, M, N), out_dtype),
        grid=(B, N // tn),
        in_specs=[pl.BlockSpec((M, K), lambda b, j: (0, 0)),
                  pl.BlockSpec((1, K, tn), lambda b, j: (b, 0, j))],
        out_specs=pl.BlockSpec((1, M, tn), lambda b, j: (b, 0, j)),
        compiler_params=_cparams("arbitrary", "arbitrary"),
        name=name,
    )(f, x)


def _twiddle(re, im, tr, ti, conj):
    if conj:
        return re * tr + im * ti, im * tr - re * ti
    return re * tr - im * ti, re * ti + im * tr


def _inner_spectrum_kernel(kb, a_ref, tr_ref, ti_ref, f_ref, o_ref):
    reps = a_ref.shape[-1] // LANES
    for j in range(kb):
        tr = jnp.concatenate([tr_ref[j]] * reps, axis=-1)
        ti = jnp.concatenate([ti_ref[j]] * reps, axis=-1)
        xs = []
        for s in range(2):
            re, im = _twiddle(a_ref[s, 0, j], a_ref[s, 1, j], tr, ti, False)
            xs.append(_dot_split(f_ref[...], jnp.concatenate([re, im], axis=0)))
        n2 = xs[0].shape[0] // 2
        o_ref[0, 0, j] = xs[0][:n2] + xs[1][:n2]
        o_ref[0, 1, j] = xs[0][n2:] - xs[1][n2:]


def _inner_conv_kernel(kb, a_ref, kf_ref, tr_ref, ti_ref, f_ref, fi_ref, o_ref):
    reps = a_ref.shape[-1] // LANES
    for j in range(kb):
        tr = jnp.concatenate([tr_ref[j]] * reps, axis=-1)
        ti = jnp.concatenate([ti_ref[j]] * reps, axis=-1)
        re, im = _twiddle(a_ref[0, 0, j].astype(F32), a_ref[0, 1, j].astype(F32), tr, ti, False)
        n2 = re.shape[0]
        x = jnp.dot(f_ref[...], jnp.concatenate([re, im], axis=0).astype(BF16), preferred_element_type=F32)
        xr, xi = x[:n2], x[n2:]
        kr, ki = kf_ref[0, 0, j], kf_ref[0, 1, j]
        yr = xr * kr - xi * ki
        yi = xr * ki + xi * kr
        y = jnp.dot(fi_ref[...], jnp.concatenate([yr, yi], axis=0).astype(BF16), preferred_element_type=F32)
        re, im = _twiddle(y[:n2], y[n2:], tr, ti, True)
        o_ref[0, 0, j] = re.astype(o_ref.dtype)
        o_ref[0, 1, j] = im.astype(o_ref.dtype)


def _inner_stage(a5, kf5, tr, ti, f2, f2i, kb):
    B, _, N1, N2, W = a5.shape
    blk = lambda: pl.BlockSpec((1, 2, kb, N2, W), lambda g, b: (b, 0, g, 0, 0))
    tw = pl.BlockSpec((kb, N2, LANES), lambda g, b: (g, 0, 0))
    mat = pl.BlockSpec((2 * N2, 2 * N2), lambda g, b: (0, 0))
    if kf5 is None:
        kern = functools.partial(_inner_spectrum_kernel, kb)
        pair = pl.BlockSpec((2, 2, kb, N2, W), lambda g, b: (0, 0, g, 0, 0))
        in_specs, args, out_dtype, name = [pair, tw, tw, mat], (a5, tr, ti, f2), F32, "hyena_filter_spectrum"
        B = 1
    else:
        kern = functools.partial(_inner_conv_kernel, kb)
        kf_spec = pl.BlockSpec((1, 2, kb, N2, W), lambda g, b: (0, 0, g, 0, 0))
        in_specs, args, out_dtype, name = ([blk(), kf_spec, tw, tw, mat, mat], (a5, kf5, tr, ti, f2, f2i),
                                           BF16, "hyena_inner_conv")
    return pl.pallas_call(
        kern,
        out_shape=jax.ShapeDtypeStruct((B,) + a5.shape[1:], out_dtype),
        grid=(N1 // kb, B),
        in_specs=in_specs,
        out_specs=blk(),
        compiler_params=_cparams("arbitrary", "arbitrary"),
        name=name,
    )(*args)


def _hyena_out_kernel(x0_ref, w0_ref, b0_ref, y_ref, u_ref, hb_ref, o_ref):
    x0 = _conv3(x0_ref[0].astype(F32), w0_ref, b0_ref)
    o_ref[0] = (x0 * (y_ref[0] + u_ref[0].astype(F32) * hb_ref[...])).astype(o_ref.dtype)


def _hyena_out(hy, conv_w, conv_b, y, u, hbias):
    B, L, W = u.shape
    nw = W // LANES
    blk = pl.BlockSpec((1, L, LANES), lambda b, j: (b, 0, j))
    vec = lambda r: pl.BlockSpec((r, LANES), lambda b, j: (0, j))
    return pl.pallas_call(
        _hyena_out_kernel,
        out_shape=jax.ShapeDtypeStruct((B, L, W), BF16),
        grid=(B, nw),
        in_specs=[blk, vec(3), vec(1), blk, blk, vec(1)],
        out_specs=blk,
        compiler_params=_cparams("arbitrary", "arbitrary"),
        name="hyena_out",
    )(hy, conv_w, conv_b, y, u, hbias)


def _dft_constants(L):
    N = 2 * L
    N2 = FFT_N2
    N1 = N // N2
    k1 = np.arange(N1, dtype=np.float64)
    th1 = 2.0 * np.pi * np.outer(k1, k1) / N1
    fwd = np.concatenate([np.cos(th1), -np.sin(th1)], axis=0)
    inv = np.concatenate([np.cos(th1), -np.sin(th1)], axis=1)[: N1 // 2] / N
    n2 = np.arange(N2, dtype=np.float64)
    tw = 2.0 * np.pi * np.outer(k1, n2) / N
    tr = np.repeat(np.cos(tw)[:, :, None], LANES, axis=2)
    ti = np.repeat(-np.sin(tw)[:, :, None], LANES, axis=2)
    th2 = 2.0 * np.pi * np.outer(n2, n2) / N2
    c2, s2 = np.cos(th2), np.sin(th2)
    f2 = np.block([[c2, s2], [-s2, c2]])
    f2i = np.block([[c2, -s2], [s2, c2]])
    f = lambda a: jnp.asarray(a, F32)
    return dict(N1=N1, N2=N2, fwd=f(fwd), inv=f(inv), tr=f(tr), ti=f(ti), f2=f(f2), f2i=f(f2i))


def _filter_features(L, emb_dim):
    bands = (emb_dim - 1) // 2
    t = np.linspace(0.0, 1.0, L, dtype=np.float32)[:, None]
    w = (np.float32(2.0 * math.pi) * np.arange(L, dtype=np.float32)[:, None] / np.float32(L)).astype(np.float32)
    f = np.linspace(1e-4, bands - 1, bands, dtype=np.float32)
    fw = (f * w).astype(np.float32)
    z = np.concatenate([t, np.cos(fw.astype(np.float64)).astype(np.float32),
                        -np.sin(fw.astype(np.float64)).astype(np.float32)], axis=-1)
    return jnp.asarray(z), t


def _decay_window(t, W):
    deltas = np.abs(np.linspace(math.log(HY_DECAY_TARGET) / HY_SLOW_DECAY,
                                math.log(HY_DECAY_TARGET) / HY_FAST_DECAY, W, dtype=np.float32))
    dec = np.exp((-t * deltas).astype(np.float32).astype(np.float64)).astype(np.float32)
    return jnp.asarray(np.concatenate([dec, dec], axis=1))


def _hyena(hy, conv_w, conv_b, fw1, fb1, fw2, fb2, fw3, fb3, fw4, ffreq, hbias):
    B, L, W3 = hy.shape
    W = W3 // 3
    C = _dft_constants(L)
    N1, N2 = C["N1"], C["N2"]
    row = lambda a: a.reshape(1, -1)
    z, t = _filter_features(L, fw1.shape[0])
    kpad = LANES - fw1.shape[0]
    z = jnp.pad(z, ((0, 0), (0, kpad)))
    taps = _implicit_filter(z, jnp.pad(fw1, ((0, kpad), (0, 0))), row(fb1), fw2, row(fb2), fw3, row(fb3), fw4,
                            row(ffreq),
                            _decay_window(t, W), tl=min(L, 512))
    lanes = N2 * W
    tn = min(lanes, 8192)
    kf = _outer_dft(C["fwd"][:, : N1 // 2], taps.reshape(2, N1 // 2, lanes), F32, tn, "hyena_filter_outer")
    kf5 = _inner_stage(kf.reshape(2, 2, N1, N2, W), None, C["tr"], C["ti"], C["f2"], None, kb=min(N1, 2))
    u = _hyena_gate(hy, conv_w, row(conv_b))
    a = _outer_dft(C["fwd"][:, : N1 // 2].astype(BF16), u.reshape(B, N1 // 2, lanes), BF16, tn,
                   "hyena_outer_fwd")
    bb = _inner_stage(a.reshape(B, 2, N1, N2, W), kf5, C["tr"], C["ti"], C["f2"].astype(BF16),
                      C["f2i"].astype(BF16), kb=min(N1, 16))
    y = _outer_dft(C["inv"].astype(BF16), bb.reshape(B, 2 * N1, lanes), F32, tn, "hyena_outer_inv")
    return _hyena_out(hy, conv_w, row(conv_b), y.reshape(B, L, W), u, row(hbias))


ROUTE_ROWS = 16


def _route_tile(h, rw_ref, rb_ref, tri_ref, carry_ref):
    tm = h.shape[0]
    E = rw_ref.shape[0] // 2
    nt = (((1,), (1,)), ((), ()))
    h_hi = h.astype(BF16)
    h_lo = (h - h_hi.astype(F32)).astype(BF16)
    a = lax.dot_general(rw_ref[...], h_hi, nt, preferred_element_type=F32)
    b = lax.dot_general(rw_ref[0:E, :], h_lo, nt, preferred_element_type=F32)
    reps = tm // LANES
    logits = a[:E] + a[E:] + b + jnp.concatenate([rb_ref[...]] * reps, axis=1)
    row = lax.broadcasted_iota(jnp.int32, (E, tm), 0)
    v = logits
    ids, vals = [], []
    for _ in range(TOP_K):
        m = jnp.max(v, axis=0, keepdims=True)
        idx = jnp.min(jnp.where(v == m, row, E), axis=0, keepdims=True)
        ids.append(idx)
        vals.append(m)
        v = jnp.where(row == idx, -jnp.inf, v)
    ex = [jnp.exp(vk - vals[0]) for vk in vals]
    den = ex[0]
    for t in ex[1:]:
        den = den + t
    hot = [row == idx for idx in ids]
    member = jnp.zeros((E, tm), F32)
    for hk in hot:
        member = jnp.where(hk, 1.0, member)
    member = member.astype(BF16)
    carry = carry_ref[...]
    before = (jnp.dot(member, tri_ref[...], preferred_element_type=F32)
              + jnp.concatenate([carry] * reps, axis=1))
    ranks = [jnp.sum(jnp.where(hk, before, 0.0), axis=0, keepdims=True) for hk in hot]
    carry_ref[...] = carry + jnp.dot(member, jnp.ones((tm, LANES), BF16), preferred_element_type=F32)
    out_row = lax.broadcasted_iota(jnp.int32, (ROUTE_ROWS, tm), 0)
    rec = jnp.zeros((ROUTE_ROWS, tm), F32)
    for k in range(TOP_K):
        rec = jnp.where(out_row == k, ids[k].astype(F32), rec)
        rec = jnp.where(out_row == TOP_K + k, ex[k] / den, rec)
        rec = jnp.where(out_row == 2 * TOP_K + k, ranks[k], rec)
    return rec


def _merge_kernel(x_ref, oa_ref, ob_ref, gt_ref, g1_ref, sc_ref, sh_ref, ng_ref, wa_ref, wb_ref, wo_ref,
                  rw_ref, rb_ref, tri_ref, xo_ref, h_ref, rec_ref, cnt_ref, before_ref, carry_ref):
    @pl.when((pl.program_id(0) == 0) & (pl.program_id(1) == 0))
    def _():
        carry_ref[...] = jnp.zeros_like(carry_ref)

    @pl.when(pl.program_id(1) == 0)
    def _():
        before_ref[0] = carry_ref[...]

    D = x_ref.shape[-1]
    gt = gt_ref[0]
    m = (gt[:, :D].astype(F32) * jnp.dot(oa_ref[0], wa_ref[...], preferred_element_type=F32)
         + gt[:, D:].astype(F32) * jnp.dot(ob_ref[0], wb_ref[...], preferred_element_type=F32))
    y = jnp.dot(m.astype(BF16), wo_ref[...], preferred_element_type=F32)
    xn = x_ref[0] + g1_ref[0] * y
    xo_ref[0] = xn
    h = _rms(xn, D) * ng_ref[...]
    h = h * (1.0 + sc_ref[0]) + sh_ref[0]
    h_ref[0] = h.astype(h_ref.dtype)
    rec_ref[0] = _route_tile(h, rw_ref, rb_ref, tri_ref, carry_ref)
    cnt_ref[...] = carry_ref[...]


def _merge(x, oa, ob, gates, g1, sc2, sh2, ng, wa, wb, wo, rw, rb, tm):
    B, S, D = x.shape
    E = rw.shape[1]
    rw_hi = rw.T.astype(BF16)
    rw2 = jnp.concatenate([rw_hi, (rw.T - rw_hi.astype(F32)).astype(BF16)], axis=0)
    rb2 = jnp.broadcast_to(rb.reshape(E, 1), (E, LANES))
    tri = jnp.asarray(np.triu(np.ones((tm, tm), np.float32), 1), BF16)
    act = lambda w: pl.BlockSpec((1, tm, w), lambda b, i: (b, i, 0))
    mod = pl.BlockSpec((1, 1, D), lambda b, i: (b, 0, 0))
    full = lambda a: pl.BlockSpec(a.shape, lambda b, i: (0, 0))
    return pl.pallas_call(
        _merge_kernel,
        out_shape=[jax.ShapeDtypeStruct((B, S, D), F32), jax.ShapeDtypeStruct((B, S, D), BF16),
                   jax.ShapeDtypeStruct((B, ROUTE_ROWS, S), F32), jax.ShapeDtypeStruct((E, LANES), F32),
                   jax.ShapeDtypeStruct((B, E, LANES), F32)],
        grid=(B, S // tm),
        in_specs=[act(D), act(oa.shape[-1]), act(ob.shape[-1]), act(2 * D), mod, mod, mod, full(ng),
                  full(wa), full(wb), full(wo), full(rw2), full(rb2), full(tri)],
        out_specs=[act(D), act(D), pl.BlockSpec((1, ROUTE_ROWS, tm), lambda b, i: (b, 0, i)),
                   pl.BlockSpec((E, LANES), lambda b, i: (0, 0)),
                   pl.BlockSpec((1, E, LANES), lambda b, i: (b, 0, 0))],
        scratch_shapes=[pltpu.VMEM((E, LANES), F32)],
        compiler_params=_cparams("arbitrary", "arbitrary"),
        name="merge_norm_router",
    )(x, oa, ob, gates, g1, sc2, sh2, ng, wa, wb, wo, rw2, rb2, tri)


def _moe_kernel(be_ref, nu_ref, x_ref, w1_ref, b1_ref, w2_ref, b2_ref, o_ref, w1b_ref, w2b_ref):
    i = pl.program_id(0)

    @pl.when(i < nu_ref[0])
    def _():
        @pl.when((i == 0) | (be_ref[i] != be_ref[jnp.maximum(i - 1, 0)]))
        def _():
            w1b_ref[...] = w1_ref[0].astype(BF16)
            w2b_ref[...] = w2_ref[0].astype(BF16)

        F = w2_ref.shape[1]
        gu = jnp.dot(x_ref[...], w1b_ref[...], preferred_element_type=F32) + b1_ref[0]
        gate = jnp.minimum(gu[:, :F], SWIGLU_LIMIT)
        up = jnp.clip(gu[:, F:], -SWIGLU_LIMIT, SWIGLU_LIMIT)
        act = (up + 1.0) * gate * _sigmoid(SWIGLU_ALPHA * gate)
        y = jnp.dot(act.astype(BF16), w2b_ref[...], preferred_element_type=F32) + b2_ref[0]
        o_ref[...] = y.astype(o_ref.dtype)


def _moe_experts(block_e, n_used, xg, w1, b1, w2, b2, bm):
    R, D = xg.shape
    E, _, F2 = w1.shape
    F = w2.shape[1]
    grid_spec = pltpu.PrefetchScalarGridSpec(
        num_scalar_prefetch=2,
        grid=(R // bm,),
        in_specs=[pl.BlockSpec((bm, D), lambda i, be, nu: (i, 0)),
                  pl.BlockSpec((1, D, F2), lambda i, be, nu: (be[i], 0, 0)),
                  pl.BlockSpec((1, 1, F2), lambda i, be, nu: (be[i], 0, 0)),
                  pl.BlockSpec((1, F, D), lambda i, be, nu: (be[i], 0, 0)),
                  pl.BlockSpec((1, 1, D), lambda i, be, nu: (be[i], 0, 0))],
        out_specs=pl.BlockSpec((bm, D), lambda i, be, nu: (i, 0)),
        scratch_shapes=[pltpu.VMEM((D, F2), BF16), pltpu.VMEM((F, D), BF16)],
    )
    return pl.pallas_call(
        _moe_kernel,
        out_shape=jax.ShapeDtypeStruct((R, D), BF16),
        grid_spec=grid_spec,
        compiler_params=pltpu.CompilerParams(dimension_semantics=("arbitrary",),
                                             vmem_limit_bytes=V7X_VMEM_LIMIT_MOE_BYTES),
        name="moe_experts",
    )(block_e, n_used, xg, w1, b1.reshape(E, 1, F2), w2, b2.reshape(E, 1, D))


def _combine_kernel(nb, x_ref, g_ref, *refs):
    o_ref = refs[-1]
    b = pl.program_id(0)
    for grp in range((len(refs) - 1) // 2):
        y_ref, w_ref = refs[2 * grp:2 * grp + 2]

        @pl.when((b >= grp * nb) & (b < (grp + 1) * nb))
        def _():
            w = w_ref[0]
            acc = w[:, 0:1] * y_ref[0, 0].astype(F32)
            for k in range(1, y_ref.shape[0]):
                acc = acc + w[:, k:k + 1] * y_ref[k, 0].astype(F32)
            o_ref[0] = x_ref[0] + g_ref[0] * acc


def _combine(x, ygs, wtss, g2, tm):
    B, S, D = x.shape
    K, nb = ygs[0].shape[:2]
    in_specs = [pl.BlockSpec((1, tm, D), lambda b, i: (b, i, 0)), pl.BlockSpec((1, 1, D), lambda b, i: (b, 0, 0))]
    args = [x, g2]

    def group_block(grp, b, i):
        mine = (b >= grp * nb) & (b < (grp + 1) * nb)
        return jnp.where(mine, b - grp * nb, 0), jnp.where(mine, i, 0)

    for grp in range(len(ygs)):
        at = functools.partial(group_block, grp)
        in_specs += [pl.BlockSpec((K, 1, tm, D), lambda b, i, at=at: (0, *at(b, i), 0)),
                     pl.BlockSpec((1, tm, K), lambda b, i, at=at: (*at(b, i), 0))]
        args += [ygs[grp], wtss[grp]]
    return pl.pallas_call(
        functools.partial(_combine_kernel, nb),
        out_shape=jax.ShapeDtypeStruct((B, S, D), F32),
        grid=(B, S // tm),
        in_specs=in_specs,
        out_specs=pl.BlockSpec((1, tm, D), lambda b, i: (b, i, 0)),
        compiler_params=_cparams("arbitrary", "arbitrary"),
        name="moe_combine",
    )(*args)


def _route(rec, counts, before, tok0, bm):
    B, _, S = rec.shape
    T = B * S
    E = counts.shape[0]
    I32 = jnp.int32
    experts = jnp.arange(E, dtype=I32)
    pick = lambda table, idx: jnp.sum(jnp.where(idx[..., None] == experts, table, 0), axis=-1)
    rec = jnp.swapaxes(rec, 1, 2).reshape(T, ROUTE_ROWS)
    top_i = rec[:, 0:TOP_K].astype(I32)
    wts = rec[:, TOP_K:2 * TOP_K]
    rank = rec[:, 2 * TOP_K:3 * TOP_K].astype(I32) - pick(before.astype(I32), top_i)
    counts = counts.astype(I32)
    A = T * TOP_K
    padded = (counts + bm - 1) // bm * bm
    pend = jnp.cumsum(padded)
    pstart = pend - padded
    start = jnp.cumsum(counts) - counts
    pos = pick(pstart, top_i) + rank
    keys = top_i.reshape(-1) * A + jnp.arange(A, dtype=I32)
    order = jnp.sort(keys) % A
    n_rows = A + E * bm
    n_blocks = n_rows // bm
    blk_start = jnp.arange(n_blocks, dtype=I32) * bm
    block_e = jnp.minimum(jnp.sum((pend[None, :] <= blk_start[:, None]).astype(I32), axis=1), E - 1)
    off = blk_start[:, None] + jnp.arange(bm, dtype=I32)[None, :] - pick(pstart, block_e)[:, None]
    valid = (off >= 0) & (off < pick(counts, block_e)[:, None])
    src = jnp.clip(pick(start, block_e)[:, None] + off, 0, A - 1)
    sorted_tok = order.at[src.reshape(-1)].get(mode="promise_in_bounds").reshape(n_blocks, bm) // TOP_K
    row_tok = jnp.where(valid, sorted_tok, (blk_start[:, None] + jnp.arange(bm, dtype=I32)[None, :]) % T)
    n_used = (pend[-1:] // bm).astype(I32)
    return wts, row_tok.reshape(-1) + tok0, pos, block_e.astype(I32), n_used


def kernel(x, c, ctx, c_ctx, ada_w, ada_b, norm1_g, norm2_g, w_in, b_in, q_norm_g, k_norm_g, lambda_q1, lambda_k1, lambda_q2, lambda_k2, subln_g, conv_w, conv_b, filt_w1, filt_b1, filt_w2, filt_b2, filt_w3, filt_b3, filt_w4, filt_freq, hyena_bias, w_up_a, w_up_b, w_out, router_w, router_b, exp_w1, exp_b1, exp_w2, exp_b2):
    B, S, D = x.shape
    depth = ada_w.shape[0]
    assert depth == 1, "context-stream update between layers is not implemented"
    l = 0
    QC = DA_HEADS * 2 * DA_QK_DIM
    W = D // 2
    row = lambda a: a.reshape(1, -1)

    lambda_init = 0.8 - 0.6 * math.exp(-0.3 * l)
    lam = (jnp.exp(jnp.sum(lambda_q1[l] * lambda_k1[l])) - jnp.exp(jnp.sum(lambda_q2[l] * lambda_k2[l]))
           + lambda_init).reshape(1).astype(F32)

    R = -(-(B + 1) // 8) * 8
    cc = jnp.concatenate([c, c_ctx[None, :], jnp.zeros((R - B - 1, D), F32)], axis=0)
    mods = _ada(cc, ada_w[l], row(ada_b[l]))
    sh1, sc1, g1, sh2, sc2, g2 = [mods[:B, i * D:(i + 1) * D].reshape(B, 1, D) for i in range(6)]
    sh1c, sc1c = [jnp.broadcast_to(mods[B, i * D:(i + 1) * D].reshape(1, 1, D), (B, 1, D)) for i in range(2)]

    w_in_b = w_in[l].astype(BF16)
    qg = row(jnp.tile(q_norm_g[l], QC // DA_QK_DIM))
    kg = row(jnp.tile(k_norm_g[l], QC // DA_QK_DIM))
    grp = np.arange(QC) // DA_QK_DIM
    bd = jnp.asarray(grp[:, None] == grp[None, :], BF16)
    qscale = DA_QK_DIM ** -0.5 * math.log2(math.e)
    spec = (("q", 0, QC), ("k", QC, QC), ("transposed", 2 * QC, QC), ("plain", 3 * QC, 3 * W),
            ("sigmoid", 3 * QC + 3 * W, 2 * D))
    qtx, kx, vtx, hy, gates = _in_proj(x, sc1, sh1, row(norm1_g[l]), w_in_b, row(b_in[l]), qg, kg, bd,
                                       _rope_tables(S, QC), spec, tm=min(S, 512), qscale=qscale)
    spec_c = (("kc", 0, QC), ("transposed", QC, QC))
    kc, vtc = _in_proj(ctx, sc1c, sh1c, row(norm1_g[l]), w_in_b[:, QC:3 * QC], row(b_in[l][QC:3 * QC]), qg,
                       kg, bd, None, spec_c, tm=ctx.shape[1], qscale=qscale)

    oa = _attention(lam, qtx, kx, kc, vtx, vtc, row(subln_g[l] * (1.0 - lambda_init)), tq=min(S // 2, 256))
    ob = _hyena(hy, conv_w[l], conv_b[l], filt_w1[l], filt_b1[l], filt_w2[l], filt_b2[l], filt_w3[l],
                filt_b3[l], filt_w4[l], filt_freq[l], hyena_bias[l])
    x_new, h2, rec, counts, before = _merge(x, oa, ob, gates, g1, sc2, sh2, row(norm2_g[l]),
                                            w_up_a[l].astype(BF16), w_up_b[l].astype(BF16), w_out[l].astype(BF16),
                                            router_w[l], row(router_b[l]), tm=min(S, 512))

    groups = MOE_BATCH_GROUPS if B % MOE_BATCH_GROUPS == 0 else 1
    nb = B // groups
    bm = EXPERT_BLOCK
    edges = jnp.concatenate([before[:, :, 0], counts[None, :, 0]], axis=0)
    ygs, wtss = [], []
    for grp in range(groups):
        b0 = grp * nb
        wts, row_tok, pos, block_e, n_used = _route(rec[b0:b0 + nb], edges[b0 + nb] - edges[b0], edges[b0],
                                                    b0 * S, bm)
        xg = h2.reshape(B * S, D).at[row_tok].get(mode="promise_in_bounds")
        yb = _moe_experts(block_e, n_used, xg, exp_w1[l], exp_b1[l], exp_w2[l], exp_b2[l], bm)
        ygs.append(yb.at[pos.T].get(mode="promise_in_bounds").reshape(TOP_K, nb, S, D))
        wtss.append(wts.reshape(nb, S, TOP_K))
    return _combine(x_new, ygs, wtss, g2, tm=min(S, 512))
```

```python
import functools
import math

import jax
import jax.numpy as jnp
import numpy as np
from jax import lax
from jax.experimental import pallas as pl
from jax.experimental.pallas import tpu as pltpu

F32 = jnp.float32
BF16 = jnp.bfloat16
HIGHEST = lax.Precision.HIGHEST

GRID_W = 64
DA_HEADS = 4
DA_QK_DIM = 64
DA_V_DIM = 2 * DA_QK_DIM
ROPE_THETA = 10000.0
ROPE_FREQS = DA_QK_DIM // 4
HY_DECAY_TARGET = 1e-2
HY_FAST_DECAY = 0.3
HY_SLOW_DECAY = 1.5
TOP_K = 4
SWIGLU_LIMIT = 7.0
SWIGLU_ALPHA = 1.702
EPS = 1e-6

LANES = 128
V7X_VMEM_LIMIT_BYTES = 48 * 1024 * 1024
V7X_VMEM_LIMIT_MOE_BYTES = 56 * 1024 * 1024
FFT_N2 = 128
ANCHOR_LAG = 3
EXPERT_BLOCK = 512
MOE_BATCH_GROUPS = 1


def _cparams(*sem):
    return pltpu.CompilerParams(dimension_semantics=sem, vmem_limit_bytes=V7X_VMEM_LIMIT_BYTES)


def _sigmoid(x):
    return 1.0 / (1.0 + jnp.exp(-x))


def _rms(x, width):
    return x * lax.rsqrt(jnp.sum(x * x, axis=-1, keepdims=True) * (1.0 / width) + EPS)


def _ada_kernel(c_ref, w_ref, b_ref, o_ref):
    c = c_ref[...]
    s = c * _sigmoid(c)
    o_ref[...] = jnp.dot(s, w_ref[...], preferred_element_type=F32, precision=HIGHEST) + b_ref[...]


def _ada(cc, w, b):
    R, D = cc.shape
    N = w.shape[1]
    tn = D
    return pl.pallas_call(
        _ada_kernel,
        out_shape=jax.ShapeDtypeStruct((R, N), F32),
        grid=(N // tn,),
        in_specs=[pl.BlockSpec((R, D), lambda j: (0, 0)),
                  pl.BlockSpec((D, tn), lambda j: (0, j)),
                  pl.BlockSpec((1, tn), lambda j: (0, j))],
        out_specs=pl.BlockSpec((R, tn), lambda j: (0, j)),
        compiler_params=_cparams("arbitrary"),
        name="ada",
    )(cc, w, b)


_TRANSPOSED_KINDS = ("q", "transposed")


def _in_kernel(spec, use_rope, qscale, x_ref, sc_ref, sh_ref, g_ref, w_ref, b_ref, qg_ref, kg_ref,
               bd_ref, *rest):
    if use_rope:
        cos_ref, sin_ref, *outs = rest
    else:
        outs = rest
    x = x_ref[0]
    h = _rms(x, x.shape[-1]) * g_ref[...]
    h = (h * (1.0 + sc_ref[0]) + sh_ref[0]).astype(BF16)
    for (kind, c0, cw), o_ref in zip(spec, outs):
        y = jnp.dot(h, w_ref[:, c0:c0 + cw], preferred_element_type=F32) + b_ref[:, c0:c0 + cw]
        if kind in ("q", "k", "kc"):
            ssq = jnp.dot((y * y).astype(BF16), bd_ref[...], preferred_element_type=F32)
            gn = qg_ref if kind == "q" else kg_ref
            y = y * lax.rsqrt(ssq * (1.0 / DA_QK_DIM) + EPS) * gn[...]
            if kind != "kc":
                lane = lax.broadcasted_iota(jnp.int32, y.shape, 1)
                partner = jnp.where((lane % (2 * ROPE_FREQS)) < ROPE_FREQS,
                                    pltpu.roll(y, cw - ROPE_FREQS, 1), pltpu.roll(y, ROPE_FREQS, 1))
                y = y * cos_ref[...] + partner * sin_ref[...]
            if kind == "q":
                y = y * qscale
        elif kind == "sigmoid":
            y = _sigmoid(y)
        if kind in _TRANSPOSED_KINDS:
            y = y.T
        o_ref[0] = y.astype(o_ref.dtype)


def _in_proj(x, sc, sh, g, w, b, qg, kg, bd, rope, spec, tm, qscale):
    B, S, D = x.shape
    N = w.shape[1]
    use_rope = rope is not None
    const2 = lambda i, bb: (0, 0)
    in_specs = [pl.BlockSpec((1, tm, D), lambda i, bb: (bb, i, 0)),
                pl.BlockSpec((1, 1, D), lambda i, bb: (bb, 0, 0)),
                pl.BlockSpec((1, 1, D), lambda i, bb: (bb, 0, 0)),
                pl.BlockSpec((1, D), const2),
                pl.BlockSpec((D, N), const2, pipeline_mode=pl.Buffered(1)),
                pl.BlockSpec((1, N), const2),
                pl.BlockSpec(qg.shape, const2),
                pl.BlockSpec(kg.shape, const2),
                pl.BlockSpec(bd.shape, const2)]
    args = [x, sc, sh, g, w, b, qg, kg, bd]
    if use_rope:
        cw = rope[0].shape[1]
        in_specs += [pl.BlockSpec((tm, cw), lambda i, bb: (i, 0))] * 2
        args += list(rope)
    out_shape = [jax.ShapeDtypeStruct((B, cw_, S) if kind in _TRANSPOSED_KINDS else (B, S, cw_), BF16)
                 for (kind, _, cw_) in spec]
    out_specs = [pl.BlockSpec((1, cw_, tm), lambda i, bb: (bb, 0, i)) if kind in _TRANSPOSED_KINDS
                 else pl.BlockSpec((1, tm, cw_), lambda i, bb: (bb, i, 0)) for (kind, _, cw_) in spec]
    return pl.pallas_call(
        functools.partial(_in_kernel, spec, use_rope, qscale),
        out_shape=out_shape,
        grid=(S // tm, B),
        in_specs=in_specs,
        out_specs=out_specs,
        compiler_params=_cparams("arbitrary", "arbitrary"),
        name="in_proj_rope" if use_rope else "in_proj_ctx",
    )(*args)


def _rope_tables(S, width):
    t = np.arange(S)
    pos = np.stack([t // GRID_W, t % GRID_W], axis=-1).astype(np.float32)
    freqs = (np.float32(ROPE_THETA) ** (-np.arange(ROPE_FREQS, dtype=np.float32) / ROPE_FREQS)).astype(np.float32)
    lane = np.arange(width)
    d = lane % DA_QK_DIM
    axis = d // (2 * ROPE_FREQS)
    half = (d % (2 * ROPE_FREQS)) // ROPE_FREQS
    f = d % ROPE_FREQS
    ang = (pos[:, axis] * freqs[f][None, :]).astype(np.float32)
    cos = np.cos(ang.astype(np.float64)).astype(np.float32)
    sin = np.sin(ang.astype(np.float64)).astype(np.float32)
    sin = np.where(half[None, :] == 0, -sin, sin)
    return jnp.asarray(cos), jnp.asarray(sin)


def _attn_kernel(tq, kc, lam_ref, q_ref, qn_ref, kx_ref, kc_ref, vtx_ref, vtc_ref, g_ref, o_ref,
                 st0, st1, mx0, mx1):
    n_lat = kx_ref.shape[1] // kc
    n_score = n_out = n_lat + kc_ref.shape[1] // kc
    rows_a = kc
    dv = vtx_ref.shape[1]
    ones = jnp.ones((16, kc), BF16)
    lam = lam_ref[0]

    def key_piece(j):
        return kx_ref[0, j * kc:(j + 1) * kc, :] if j < n_lat else kc_ref[0, (j - n_lat) * kc:(j - n_lat + 1) * kc, :]

    def value_piece(j):
        return vtx_ref[0, :, j * kc:(j + 1) * kc] if j < n_lat else vtc_ref[0, :, (j - n_lat) * kc:(j - n_lat + 1) * kc]

    def stacked(qt):
        row = lax.broadcasted_iota(jnp.int32, qt.shape, 0)
        zero = jnp.zeros_like(qt)
        return jnp.concatenate([jnp.where(row < DA_QK_DIM, qt, zero), jnp.where(row >= DA_QK_DIM, qt, zero)],
                               axis=1)

    def score_piece(qs, st, mx, j):
        r0 = j * rows_a
        st[r0:r0 + rows_a, :] = jnp.dot(key_piece(j), qs, preferred_element_type=F32)
        part = jnp.max(st[r0:r0 + rows_a, :].reshape(rows_a // 8, 8, st.shape[-1]), axis=0)
        mx[...] = part if j == 0 else jnp.maximum(mx[...], part)
        return part

    def out_piece(st, m, acc, j):
        e = jnp.exp2(st[j * kc:(j + 1) * kc, :] - m).astype(BF16)
        vt1 = jnp.concatenate([value_piece(j), ones], axis=0)
        d = jnp.dot(vt1, e, preferred_element_type=F32)
        return d if acc is None else acc + d

    def finish(acc, c):
        ot = acc[:dv] * (1.0 / acc[dv:dv + 1])
        o = (ot[:, :tq] - lam * ot[:, tq:]).T
        o = _rms(o, o.shape[-1]) * g_ref[...]
        o_ref[0, c * tq:(c + 1) * tq, :] = o.astype(o_ref.dtype)

    def overlapped(st_cur, mx_cur, q_next, st_next, mx_next, c):
        m = jnp.max(mx_cur[...], axis=0, keepdims=True)
        qs = stacked(q_next)
        acc = None
        p = 0
        parts = []
        for j in range(n_out):
            while p < n_score and p * n_out <= j * n_score:
                parts.append(score_piece(qs, st_next, mx_next, p))
                p += 1
            m_j = m + 0.0 * parts[j - ANCHOR_LAG][0:1, :] if ANCHOR_LAG <= j < len(parts) + ANCHOR_LAG else m
            acc = out_piece(st_cur, m_j, acc, j)
        for p in range(p, n_score):
            score_piece(qs, st_next, mx_next, p)
        finish(acc, c)

    @pl.when(pl.program_id(2) == 0)
    def _():
        qs = stacked(q_ref[0, :, 0:tq])
        for j in range(n_score):
            score_piece(qs, st0, mx0, j)

    overlapped(st0, mx0, q_ref[0, :, tq:2 * tq], st1, mx1, 0)
    overlapped(st1, mx1, qn_ref[0], st0, mx0, 1)


def _attention(lam, qt, kx, kc_, vtx, vtc, g, tq):
    B, W, S = qt.shape
    C = kc_.shape[1]
    Sk = S + C
    H = W // LANES
    tb = 2 * tq
    n_steps = S // tb
    kc = 256 if (S % 256 == 0 and C % 256 == 0) else LANES
    return pl.pallas_call(
        functools.partial(_attn_kernel, tq, kc),
        out_shape=jax.ShapeDtypeStruct((B, S, W), BF16),
        grid=(B, H, n_steps),
        in_specs=[pl.BlockSpec(memory_space=pltpu.SMEM),
                  pl.BlockSpec((1, LANES, tb), lambda b, h, i: (b, h, i)),
                  pl.BlockSpec((1, LANES, tq), lambda b, h, i: (b, h, jnp.minimum(i + 1, n_steps - 1) * 2)),
                  pl.BlockSpec((1, S, LANES), lambda b, h, i: (b, 0, h)),
                  pl.BlockSpec((1, C, LANES), lambda b, h, i: (b, 0, h)),
                  pl.BlockSpec((1, LANES, S), lambda b, h, i: (b, h, 0)),
                  pl.BlockSpec((1, LANES, C), lambda b, h, i: (b, h, 0)),
                  pl.BlockSpec((1, LANES), lambda b, h, i: (0, 0))],
        out_specs=pl.BlockSpec((1, tb, LANES), lambda b, h, i: (b, i, h)),
        scratch_shapes=[pltpu.VMEM((Sk, 2 * tq), F32), pltpu.VMEM((Sk, 2 * tq), F32),
                        pltpu.VMEM((8, 2 * tq), F32), pltpu.VMEM((8, 2 * tq), F32)],
        compiler_params=_cparams("arbitrary", "arbitrary", "arbitrary"),
        name="diff_attn",
    )(lam, qt, qt, kx, kc_, vtx, vtc, g)


def _filter_kernel(z_ref, w1, b1, w2, b2, w3, b3, w4, fr, dec_ref, o_ref):
    dot = functools.partial(jnp.dot, preferred_element_type=F32, precision=HIGHEST)
    f = fr[...]
    h = jnp.sin(f * (dot(z_ref[...], w1[...]) + b1[...]))
    h = jnp.sin(f * (dot(h, w2[...]) + b2[...]))
    h = jnp.sin(f * (dot(h, w3[...]) + b3[...]))
    taps = _dot_split(h, w4[...]) * dec_ref[...]
    tl, W = o_ref.shape[1:]
    o_ref[0] = taps[:, :W]
    pos = lax.broadcasted_iota(jnp.int32, (tl, W), 0) + pl.program_id(0) * tl
    o_ref[1] = jnp.where(pos == 0, 0.0, taps[:, W:])


def _implicit_filter(z, w1, b1, w2, b2, w3, b3, w4, fr, dec, tl):
    L = z.shape[0]
    N = w4.shape[1]
    full = lambda a: pl.BlockSpec(a.shape, lambda i: (0, 0))
    return pl.pallas_call(
        _filter_kernel,
        out_shape=jax.ShapeDtypeStruct((2, L, N // 2), F32),
        grid=(L // tl,),
        in_specs=[pl.BlockSpec((tl, z.shape[1]), lambda i: (i, 0)),
                  full(w1), full(b1), full(w2), full(b2), full(w3), full(b3), full(w4), full(fr),
                  pl.BlockSpec((tl, N), lambda i: (i, 0))],
        out_specs=pl.BlockSpec((2, tl, N // 2), lambda i: (0, i, 0)),
        compiler_params=_cparams("arbitrary"),
        name="hyena_filter",
    )(z, w1, b1, w2, b2, w3, b3, w4, fr, dec)


def _conv3(a, w_ref, b_ref):
    L = a.shape[0]
    row = lax.broadcasted_iota(jnp.int32, a.shape, 0)
    prev = jnp.where(row == 0, 0.0, pltpu.roll(a, 1, 0))
    nxt = jnp.where(row == L - 1, 0.0, pltpu.roll(a, L - 1, 0))
    return prev * w_ref[0:1, :] + a * w_ref[1:2, :] + nxt * w_ref[2:3, :] + b_ref[...]


def _gate_kernel(x1_ref, v_ref, w1_ref, wv_ref, b1_ref, bv_ref, u_ref):
    u = _conv3(x1_ref[0].astype(F32), w1_ref, b1_ref) * _conv3(v_ref[0].astype(F32), wv_ref, bv_ref)
    u_ref[0] = u.astype(u_ref.dtype)


def _hyena_gate(hy, conv_w, conv_b):
    B, L, W3 = hy.shape
    W = W3 // 3
    nw = W // LANES
    act = lambda off: pl.BlockSpec((1, L, LANES), lambda b, j: (b, 0, j + off))
    cw = lambda off: pl.BlockSpec((3, LANES), lambda b, j: (0, j + off))
    cb = lambda off: pl.BlockSpec((1, LANES), lambda b, j: (0, j + off))
    return pl.pallas_call(
        _gate_kernel,
        out_shape=jax.ShapeDtypeStruct((B, L, W), BF16),
        grid=(B, nw),
        in_specs=[act(nw), act(2 * nw), cw(nw), cw(2 * nw), cb(nw), cb(2 * nw)],
        out_specs=pl.BlockSpec((1, L, LANES), lambda b, j: (b, 0, j)),
        compiler_params=_cparams("arbitrary", "arbitrary"),
        name="hyena_gate",
    )(hy, hy, conv_w, conv_w, conv_b, conv_b)


def _dot_split(a, b):
    a_hi, b_hi = a.astype(BF16), b.astype(BF16)
    a_lo = (a - a_hi.astype(F32)).astype(BF16)
    b_lo = (b - b_hi.astype(F32)).astype(BF16)
    dot = functools.partial(jnp.dot, preferred_element_type=F32)
    return dot(a_hi, b_hi) + (dot(a_lo, b_hi) + dot(a_hi, b_lo))


def _outer_dft_kernel(split, f_ref, x_ref, o_ref):
    if split:
        o_ref[0] = _dot_split(f_ref[...], x_ref[0]).astype(o_ref.dtype)
    else:
        o_ref[0] = jnp.dot(f_ref[...], x_ref[0], preferred_element_type=F32).astype(o_ref.dtype)


def _outer_dft(f, x, out_dtype, tn, name):
    B, K, N = x.shape
    M = f.shape[0]
    return pl.pallas_call(
        functools.partial(_outer_dft_kernel, x.dtype == F32),
        out_shape=jax.ShapeDtypeStruct((B, M, N), out_dtype),
        grid=(B, N // tn),
        in_specs=[pl.BlockSpec((M, K), lambda b, j: (0, 0)),
                  pl.BlockSpec((1, K, tn), lambda b, j: (b, 0, j))],
        out_specs=pl.BlockSpec((1, M, tn), lambda b, j: (b, 0, j)),
        compiler_params=_cparams("arbitrary", "arbitrary"),
        name=name,
    )(f, x)


def _twiddle(re, im, tr, ti, conj):
    if conj:
        return re * tr + im * ti, im * tr - re * ti
    return re * tr - im * ti, re * ti + im * tr


def _inner_spectrum_kernel(kb, a_ref, tr_ref, ti_ref, f_ref, o_ref):
    reps = a_ref.shape[-1] // LANES
    for j in range(kb):
        tr = jnp.concatenate([tr_ref[j]] * reps, axis=-1)
        ti = jnp.concatenate([ti_ref[j]] * reps, axis=-1)
        xs = []
        for s in range(2):
            re, im = _twiddle(a_ref[s, 0, j], a_ref[s, 1, j], tr, ti, False)
            xs.append(_dot_split(f_ref[...], jnp.concatenate([re, im], axis=0)))
        n2 = xs[0].shape[0] // 2
        o_ref[0, 0, j] = xs[0][:n2] + xs[1][:n2]
        o_ref[0, 1, j] = xs[0][n2:] - xs[1][n2:]


def _inner_conv_kernel(kb, a_ref, kf_ref, tr_ref, ti_ref, f_ref, fi_ref, o_ref):
    reps = a_ref.shape[-1] // LANES
    for j in range(kb):
        tr = jnp.concatenate([tr_ref[j]] * reps, axis=-1)
        ti = jnp.concatenate([ti_ref[j]] * reps, axis=-1)
        re, im = _twiddle(a_ref[0, 0, j].astype(F32), a_ref[0, 1, j].astype(F32), tr, ti, False)
        n2 = re.shape[0]
        x = jnp.dot(f_ref[...], jnp.concatenate([re, im], axis=0).astype(BF16), preferred_element_type=F32)
        xr, xi = x[:n2], x[n2:]
        kr, ki = kf_ref[0, 0, j], kf_ref[0, 1, j]
        yr = xr * kr - xi * ki
        yi = xr * ki + xi * kr
        y = jnp.dot(fi_ref[...], jnp.concatenate([yr, yi], axis=0).astype(BF16), preferred_element_type=F32)
        re, im = _twiddle(y[:n2], y[n2:], tr, ti, True)
        o_ref[0, 0, j] = re.astype(o_ref.dtype)
        o_ref[0, 1, j] = im.astype(o_ref.dtype)


def _inner_stage(a5, kf5, tr, ti, f2, f2i, kb):
    B, _, N1, N2, W = a5.shape
    blk = lambda: pl.BlockSpec((1, 2, kb, N2, W), lambda g, b: (b, 0, g, 0, 0))
    tw = pl.BlockSpec((kb, N2, LANES), lambda g, b: (g, 0, 0))
    mat = pl.BlockSpec((2 * N2, 2 * N2), lambda g, b: (0, 0))
    if kf5 is None:
        kern = functools.partial(_inner_spectrum_kernel, kb)
        pair = pl.BlockSpec((2, 2, kb, N2, W), lambda g, b: (0, 0, g, 0, 0))
        in_specs, args, out_dtype, name = [pair, tw, tw, mat], (a5, tr, ti, f2), F32, "hyena_filter_spectrum"
        B = 1
    else:
        kern = functools.partial(_inner_conv_kernel, kb)
        kf_spec = pl.BlockSpec((1, 2, kb, N2, W), lambda g, b: (0, 0, g, 0, 0))
        in_specs, args, out_dtype, name = ([blk(), kf_spec, tw, tw, mat, mat], (a5, kf5, tr, ti, f2, f2i),
                                           BF16, "hyena_inner_conv")
    return pl.pallas_call(
        kern,
        out_shape=jax.ShapeDtypeStruct((B,) + a5.shape[1:], out_dtype),
        grid=(N1 // kb, B),
        in_specs=in_specs,
        out_specs=blk(),
        compiler_params=_cparams("arbitrary", "arbitrary"),
        name=name,
    )(*args)


def _hyena_out_kernel(x0_ref, w0_ref, b0_ref, y_ref, u_ref, hb_ref, o_ref):
    x0 = _conv3(x0_ref[0].astype(F32), w0_ref, b0_ref)
    o_ref[0] = (x0 * (y_ref[0] + u_ref[0].astype(F32) * hb_ref[...])).astype(o_ref.dtype)


def _hyena_out(hy, conv_w, conv_b, y, u, hbias):
    B, L, W = u.shape
    nw = W // LANES
    blk = pl.BlockSpec((1, L, LANES), lambda b, j: (b, 0, j))
    vec = lambda r: pl.BlockSpec((r, LANES), lambda b, j: (0, j))
    return pl.pallas_call(
        _hyena_out_kernel,
        out_shape=jax.ShapeDtypeStruct((B, L, W), BF16),
        grid=(B, nw),
        in_specs=[blk, vec(3), vec(1), blk, blk, vec(1)],
        out_specs=blk,
        compiler_params=_cparams("arbitrary", "arbitrary"),
        name="hyena_out",
    )(hy, conv_w, conv_b, y, u, hbias)


def _dft_constants(L):
    N = 2 * L
    N2 = FFT_N2
    N1 = N // N2
    k1 = np.arange(N1, dtype=np.float64)
    th1 = 2.0 * np.pi * np.outer(k1, k1) / N1
    fwd = np.concatenate([np.cos(th1), -np.sin(th1)], axis=0)
    inv = np.concatenate([np.cos(th1), -np.sin(th1)], axis=1)[: N1 // 2] / N
    n2 = np.arange(N2, dtype=np.float64)
    tw = 2.0 * np.pi * np.outer(k1, n2) / N
    tr = np.repeat(np.cos(tw)[:, :, None], LANES, axis=2)
    ti = np.repeat(-np.sin(tw)[:, :, None], LANES, axis=2)
    th2 = 2.0 * np.pi * np.outer(n2, n2) / N2
    c2, s2 = np.cos(th2), np.sin(th2)
    f2 = np.block([[c2, s2], [-s2, c2]])
    f2i = np.block([[c2, -s2], [s2, c2]])
    f = lambda a: jnp.asarray(a, F32)
    return dict(N1=N1, N2=N2, fwd=f(fwd), inv=f(inv), tr=f(tr), ti=f(ti), f2=f(f2), f2i=f(f2i))


def _filter_features(L, emb_dim):
    bands = (emb_dim - 1) // 2
    t = np.linspace(0.0, 1.0, L, dtype=np.float32)[:, None]
    w = (np.float32(2.0 * math.pi) * np.arange(L, dtype=np.float32)[:, None] / np.float32(L)).astype(np.float32)
    f = np.linspace(1e-4, bands - 1, bands, dtype=np.float32)
    fw = (f * w).astype(np.float32)
    z = np.concatenate([t, np.cos(fw.astype(np.float64)).astype(np.float32),
                        -np.sin(fw.astype(np.float64)).astype(np.float32)], axis=-1)
    return jnp.asarray(z), t


def _decay_window(t, W):
    deltas = np.abs(np.linspace(math.log(HY_DECAY_TARGET) / HY_SLOW_DECAY,
                                math.log(HY_DECAY_TARGET) / HY_FAST_DECAY, W, dtype=np.float32))
    dec = np.exp((-t * deltas).astype(np.float32).astype(np.float64)).astype(np.float32)
    return jnp.asarray(np.concatenate([dec, dec], axis=1))


def _hyena(hy, conv_w, conv_b, fw1, fb1, fw2, fb2, fw3, fb3, fw4, ffreq, hbias):
    B, L, W3 = hy.shape
    W = W3 // 3
    C = _dft_constants(L)
    N1, N2 = C["N1"], C["N2"]
    row = lambda a: a.reshape(1, -1)
    z, t = _filter_features(L, fw1.shape[0])
    kpad = LANES - fw1.shape[0]
    z = jnp.pad(z, ((0, 0), (0, kpad)))
    taps = _implicit_filter(z, jnp.pad(fw1, ((0, kpad), (0, 0))), row(fb1), fw2, row(fb2), fw3, row(fb3), fw4,
                            row(ffreq),
                            _decay_window(t, W), tl=min(L, 512))
    lanes = N2 * W
    tn = min(lanes, 8192)
    kf = _outer_dft(C["fwd"][:, : N1 // 2], taps.reshape(2, N1 // 2, lanes), F32, tn, "hyena_filter_outer")
    kf5 = _inner_stage(kf.reshape(2, 2, N1, N2, W), None, C["tr"], C["ti"], C["f2"], None, kb=min(N1, 2))
    u = _hyena_gate(hy, conv_w, row(conv_b))
    a = _outer_dft(C["fwd"][:, : N1 // 2].astype(BF16), u.reshape(B, N1 // 2, lanes), BF16, tn,
                   "hyena_outer_fwd")
    bb = _inner_stage(a.reshape(B, 2, N1, N2, W), kf5, C["tr"], C["ti"], C["f2"].astype(BF16),
                      C["f2i"].astype(BF16), kb=min(N1, 16))
    y = _outer_dft(C["inv"].astype(BF16), bb.reshape(B, 2 * N1, lanes), F32, tn, "hyena_outer_inv")
    return _hyena_out(hy, conv_w, row(conv_b), y.reshape(B, L, W), u, row(hbias))


ROUTE_ROWS = 16


def _route_tile(h, rw_ref, rb_ref, tri_ref, carry_ref):
    tm = h.shape[0]
    E = rw_ref.shape[0] // 2
    nt = (((1,), (1,)), ((), ()))
    h_hi = h.astype(BF16)
    h_lo = (h - h_hi.astype(F32)).astype(BF16)
    a = lax.dot_general(rw_ref[...], h_hi, nt, preferred_element_type=F32)
    b = lax.dot_general(rw_ref[0:E, :], h_lo, nt, preferred_element_type=F32)
    reps = tm // LANES
    logits = a[:E] + a[E:] + b + jnp.concatenate([rb_ref[...]] * reps, axis=1)
    row = lax.broadcasted_iota(jnp.int32, (E, tm), 0)
    v = logits
    ids, vals = [], []
    for _ in range(TOP_K):
        m = jnp.max(v, axis=0, keepdims=True)
        idx = jnp.min(jnp.where(v == m, row, E), axis=0, keepdims=True)
        ids.append(idx)
        vals.append(m)
        v = jnp.where(row == idx, -jnp.inf, v)
    ex = [jnp.exp(vk - vals[0]) for vk in vals]
    den = ex[0]
    for t in ex[1:]:
        den = den + t
    hot = [row == idx for idx in ids]
    member = jnp.zeros((E, tm), F32)
    for hk in hot:
        member = jnp.where(hk, 1.0, member)
    member = member.astype(BF16)
    carry = carry_ref[...]
    before = (jnp.dot(member, tri_ref[...], preferred_element_type=F32)
              + jnp.concatenate([carry] * reps, axis=1))
    ranks = [jnp.sum(jnp.where(hk, before, 0.0), axis=0, keepdims=True) for hk in hot]
    carry_ref[...] = carry + jnp.dot(member, jnp.ones((tm, LANES), BF16), preferred_element_type=F32)
    out_row = lax.broadcasted_iota(jnp.int32, (ROUTE_ROWS, tm), 0)
    rec = jnp.zeros((ROUTE_ROWS, tm), F32)
    for k in range(TOP_K):
        rec = jnp.where(out_row == k, ids[k].astype(F32), rec)
        rec = jnp.where(out_row == TOP_K + k, ex[k] / den, rec)
        rec = jnp.where(out_row == 2 * TOP_K + k, ranks[k], rec)
    return rec


def _merge_kernel(x_ref, oa_ref, ob_ref, gt_ref, g1_ref, sc_ref, sh_ref, ng_ref, wa_ref, wb_ref, wo_ref,
                  rw_ref, rb_ref, tri_ref, xo_ref, h_ref, rec_ref, cnt_ref, before_ref, carry_ref):
    @pl.when((pl.program_id(0) == 0) & (pl.program_id(1) == 0))
    def _():
        carry_ref[...] = jnp.zeros_like(carry_ref)

    @pl.when(pl.program_id(1) == 0)
    def _():
        before_ref[0] = carry_ref[...]

    D = x_ref.shape[-1]
    gt = gt_ref[0]
    m = (gt[:, :D].astype(F32) * jnp.dot(oa_ref[0], wa_ref[...], preferred_element_type=F32)
         + gt[:, D:].astype(F32) * jnp.dot(ob_ref[0], wb_ref[...], preferred_element_type=F32))
    y = jnp.dot(m.astype(BF16), wo_ref[...], preferred_element_type=F32)
    xn = x_ref[0] + g1_ref[0] * y
    xo_ref[0] = xn
    h = _rms(xn, D) * ng_ref[...]
    h = h * (1.0 + sc_ref[0]) + sh_ref[0]
    h_ref[0] = h.astype(h_ref.dtype)
    rec_ref[0] = _route_tile(h, rw_ref, rb_ref, tri_ref, carry_ref)
    cnt_ref[...] = carry_ref[...]


def _merge(x, oa, ob, gates, g1, sc2, sh2, ng, wa, wb, wo, rw, rb, tm):
    B, S, D = x.shape
    E = rw.shape[1]
    rw_hi = rw.T.astype(BF16)
    rw2 = jnp.concatenate([rw_hi, (rw.T - rw_hi.astype(F32)).astype(BF16)], axis=0)
    rb2 = jnp.broadcast_to(rb.reshape(E, 1), (E, LANES))
    tri = jnp.asarray(np.triu(np.ones((tm, tm), np.float32), 1), BF16)
    act = lambda w: pl.BlockSpec((1, tm, w), lambda b, i: (b, i, 0))
    mod = pl.BlockSpec((1, 1, D), lambda b, i: (b, 0, 0))
    full = lambda a: pl.BlockSpec(a.shape, lambda b, i: (0, 0))
    return pl.pallas_call(
        _merge_kernel,
        out_shape=[jax.ShapeDtypeStruct((B, S, D), F32), jax.ShapeDtypeStruct((B, S, D), BF16),
                   jax.ShapeDtypeStruct((B, ROUTE_ROWS, S), F32), jax.ShapeDtypeStruct((E, LANES), F32),
                   jax.ShapeDtypeStruct((B, E, LANES), F32)],
        grid=(B, S // tm),
        in_specs=[act(D), act(oa.shape[-1]), act(ob.shape[-1]), act(2 * D), mod, mod, mod, full(ng),
                  full(wa), full(wb), full(wo), full(rw2), full(rb2), full(tri)],
        out_specs=[act(D), act(D), pl.BlockSpec((1, ROUTE_ROWS, tm), lambda b, i: (b, 0, i)),
                   pl.BlockSpec((E, LANES), lambda b, i: (0, 0)),
                   pl.BlockSpec((1, E, LANES), lambda b, i: (b, 0, 0))],
        scratch_shapes=[pltpu.VMEM((E, LANES), F32)],
        compiler_params=_cparams("arbitrary", "arbitrary"),
        name="merge_norm_router",
    )(x, oa, ob, gates, g1, sc2, sh2, ng, wa, wb, wo, rw2, rb2, tri)


def _moe_kernel(be_ref, nu_ref, x_ref, w1_ref, b1_ref, w2_ref, b2_ref, o_ref, w1b_ref, w2b_ref):
    i = pl.program_id(0)

    @pl.when(i < nu_ref[0])
    def _():
        @pl.when((i == 0) | (be_ref[i] != be_ref[jnp.maximum(i - 1, 0)]))
        def _():
            w1b_ref[...] = w1_ref[0].astype(BF16)
            w2b_ref[...] = w2_ref[0].astype(BF16)

        F = w2_ref.shape[1]
        gu = jnp.dot(x_ref[...], w1b_ref[...], preferred_element_type=F32) + b1_ref[0]
        gate = jnp.minimum(gu[:, :F], SWIGLU_LIMIT)
        up = jnp.clip(gu[:, F:], -SWIGLU_LIMIT, SWIGLU_LIMIT)
        act = (up + 1.0) * gate * _sigmoid(SWIGLU_ALPHA * gate)
        y = jnp.dot(act.astype(BF16), w2b_ref[...], preferred_element_type=F32) + b2_ref[0]
        o_ref[...] = y.astype(o_ref.dtype)


def _moe_experts(block_e, n_used, xg, w1, b1, w2, b2, bm):
    R, D = xg.shape
    E, _, F2 = w1.shape
    F = w2.shape[1]
    grid_spec = pltpu.PrefetchScalarGridSpec(
        num_scalar_prefetch=2,
        grid=(R // bm,),
        in_specs=[pl.BlockSpec((bm, D), lambda i, be, nu: (i, 0)),
                  pl.BlockSpec((1, D, F2), lambda i, be, nu: (be[i], 0, 0)),
                  pl.BlockSpec((1, 1, F2), lambda i, be, nu: (be[i], 0, 0)),
                  pl.BlockSpec((1, F, D), lambda i, be, nu: (be[i], 0, 0)),
                  pl.BlockSpec((1, 1, D), lambda i, be, nu: (be[i], 0, 0))],
        out_specs=pl.BlockSpec((bm, D), lambda i, be, nu: (i, 0)),
        scratch_shapes=[pltpu.VMEM((D, F2), BF16), pltpu.VMEM((F, D), BF16)],
    )
    return pl.pallas_call(
        _moe_kernel,
        out_shape=jax.ShapeDtypeStruct((R, D), BF16),
        grid_spec=grid_spec,
        compiler_params=pltpu.CompilerParams(dimension_semantics=("arbitrary",),
                                             vmem_limit_bytes=V7X_VMEM_LIMIT_MOE_BYTES),
        name="moe_experts",
    )(block_e, n_used, xg, w1, b1.reshape(E, 1, F2), w2, b2.reshape(E, 1, D))


def _combine_kernel(nb, x_ref, g_ref, *refs):
    o_ref = refs[-1]
    b = pl.program_id(0)
    for grp in range((len(refs) - 1) // 2):
        y_ref, w_ref = refs[2 * grp:2 * grp + 2]

        @pl.when((b >= grp * nb) & (b < (grp + 1) * nb))
        def _():
            w = w_ref[0]
            acc = w[:, 0:1] * y_ref[0, 0].astype(F32)
            for k in range(1, y_ref.shape[0]):
                acc = acc + w[:, k:k + 1] * y_ref[k, 0].astype(F32)
            o_ref[0] = x_ref[0] + g_ref[0] * acc


def _combine(x, ygs, wtss, g2, tm):
    B, S, D = x.shape
    K, nb = ygs[0].shape[:2]
    in_specs = [pl.BlockSpec((1, tm, D), lambda b, i: (b, i, 0)), pl.BlockSpec((1, 1, D), lambda b, i: (b, 0, 0))]
    args = [x, g2]

    def group_block(grp, b, i):
        mine = (b >= grp * nb) & (b < (grp + 1) * nb)
        return jnp.where(mine, b - grp * nb, 0), jnp.where(mine, i, 0)

    for grp in range(len(ygs)):
        at = functools.partial(group_block, grp)
        in_specs += [pl.BlockSpec((K, 1, tm, D), lambda b, i, at=at: (0, *at(b, i), 0)),
                     pl.BlockSpec((1, tm, K), lambda b, i, at=at: (*at(b, i), 0))]
        args += [ygs[grp], wtss[grp]]
    return pl.pallas_call(
        functools.partial(_combine_kernel, nb),
        out_shape=jax.ShapeDtypeStruct((B, S, D), F32),
        grid=(B, S // tm),
        in_specs=in_specs,
        out_specs=pl.BlockSpec((1, tm, D), lambda b, i: (b, i, 0)),
        compiler_params=_cparams("arbitrary", "arbitrary"),
        name="moe_combine",
    )(*args)


def _route(rec, counts, before, tok0, bm):
    B, _, S = rec.shape
    T = B * S
    E = counts.shape[0]
    I32 = jnp.int32
    experts = jnp.arange(E, dtype=I32)
    pick = lambda table, idx: jnp.sum(jnp.where(idx[..., None] == experts, table, 0), axis=-1)
    rec = jnp.swapaxes(rec, 1, 2).reshape(T, ROUTE_ROWS)
    top_i = rec[:, 0:TOP_K].astype(I32)
    wts = rec[:, TOP_K:2 * TOP_K]
    rank = rec[:, 2 * TOP_K:3 * TOP_K].astype(I32) - pick(before.astype(I32), top_i)
    counts = counts.astype(I32)
    A = T * TOP_K
    padded = (counts + bm - 1) // bm * bm
    pend = jnp.cumsum(padded)
    pstart = pend - padded
    start = jnp.cumsum(counts) - counts
    pos = pick(pstart, top_i) + rank
    keys = top_i.reshape(-1) * A + jnp.arange(A, dtype=I32)
    order = jnp.sort(keys) % A
    n_rows = A + E * bm
    n_blocks = n_rows // bm
    blk_start = jnp.arange(n_blocks, dtype=I32) * bm
    block_e = jnp.minimum(jnp.sum((pend[None, :] <= blk_start[:, None]).astype(I32), axis=1), E - 1)
    off = blk_start[:, None] + jnp.arange(bm, dtype=I32)[None, :] - pick(pstart, block_e)[:, None]
    valid = (off >= 0) & (off < pick(counts, block_e)[:, None])
    src = jnp.clip(pick(start, block_e)[:, None] + off, 0, A - 1)
    sorted_tok = order.at[src.reshape(-1)].get(mode="promise_in_bounds").reshape(n_blocks, bm) // TOP_K
    row_tok = jnp.where(valid, sorted_tok, (blk_start[:, None] + jnp.arange(bm, dtype=I32)[None, :]) % T)
    n_used = (pend[-1:] // bm).astype(I32)
    return wts, row_tok.reshape(-1) + tok0, pos, block_e.astype(I32), n_used


def kernel(x, c, ctx, c_ctx, ada_w, ada_b, norm1_g, norm2_g, w_in, b_in, q_norm_g, k_norm_g, lambda_q1, lambda_k1, lambda_q2, lambda_k2, subln_g, conv_w, conv_b, filt_w1, filt_b1, filt_w2, filt_b2, filt_w3, filt_b3, filt_w4, filt_freq, hyena_bias, w_up_a, w_up_b, w_out, router_w, router_b, exp_w1, exp_b1, exp_w2, exp_b2):
    B, S, D = x.shape
    depth = ada_w.shape[0]
    assert depth == 1, "context-stream update between layers is not implemented"
    l = 0
    QC = DA_HEADS * 2 * DA_QK_DIM
    W = D // 2
    row = lambda a: a.reshape(1, -1)

    lambda_init = 0.8 - 0.6 * math.exp(-0.3 * l)
    lam = (jnp.exp(jnp.sum(lambda_q1[l] * lambda_k1[l])) - jnp.exp(jnp.sum(lambda_q2[l] * lambda_k2[l]))
           + lambda_init).reshape(1).astype(F32)

    R = -(-(B + 1) // 8) * 8
    cc = jnp.concatenate([c, c_ctx[None, :], jnp.zeros((R - B - 1, D), F32)], axis=0)
    mods = _ada(cc, ada_w[l], row(ada_b[l]))
    sh1, sc1, g1, sh2, sc2, g2 = [mods[:B, i * D:(i + 1) * D].reshape(B, 1, D) for i in range(6)]
    sh1c, sc1c = [jnp.broadcast_to(mods[B, i * D:(i + 1) * D].reshape(1, 1, D), (B, 1, D)) for i in range(2)]

    w_in_b = w_in[l].astype(BF16)
    qg = row(jnp.tile(q_norm_g[l], QC // DA_QK_DIM))
    kg = row(jnp.tile(k_norm_g[l], QC // DA_QK_DIM))
    grp = np.arange(QC) // DA_QK_DIM
    bd = jnp.asarray(grp[:, None] == grp[None, :], BF16)
    qscale = DA_QK_DIM ** -0.5 * math.log2(math.e)
    spec = (("q", 0, QC), ("k", QC, QC), ("transposed", 2 * QC, QC), ("plain", 3 * QC, 3 * W),
            ("sigmoid", 3 * QC + 3 * W, 2 * D))
    qtx, kx, vtx, hy, gates = _in_proj(x, sc1, sh1, row(norm1_g[l]), w_in_b, row(b_in[l]), qg, kg, bd,
                                       _rope_tables(S, QC), spec, tm=min(S, 512), qscale=qscale)
    spec_c = (("kc", 0, QC), ("transposed", QC, QC))
    kc, vtc = _in_proj(ctx, sc1c, sh1c, row(norm1_g[l]), w_in_b[:, QC:3 * QC], row(b_in[l][QC:3 * QC]), qg,
                       kg, bd, None, spec_c, tm=ctx.shape[1], qscale=qscale)

    oa = _attention(lam, qtx, kx, kc, vtx, vtc, row(subln_g[l] * (1.0 - lambda_init)), tq=min(S // 2, 256))
    ob = _hyena(hy, conv_w[l], conv_b[l], filt_w1[l], filt_b1[l], filt_w2[l], filt_b2[l], filt_w3[l],
                filt_b3[l], filt_w4[l], filt_freq[l], hyena_bias[l])
    x_new, h2, rec, counts, before = _merge(x, oa, ob, gates, g1, sc2, sh2, row(norm2_g[l]),
                                            w_up_a[l].astype(BF16), w_up_b[l].astype(BF16), w_out[l].astype(BF16),
                                            router_w[l], row(router_b[l]), tm=min(S, 512))

    groups = MOE_BATCH_GROUPS if B % MOE_BATCH_GROUPS == 0 else 1
    nb = B // groups
    bm = EXPERT_BLOCK
    edges = jnp.concatenate([before[:, :, 0], counts[None, :, 0]], axis=0)
    ygs, wtss = [], []
    for grp in range(groups):
        b0 = grp * nb
        wts, row_tok, pos, block_e, n_used = _route(rec[b0:b0 + nb], edges[b0 + nb] - edges[b0], edges[b0],
                                                    b0 * S, bm)
        xg = h2.reshape(B * S, D).at[row_tok].get(mode="promise_in_bounds")
        yb = _moe_experts(block_e, n_used, xg, exp_w1[l], exp_b1[l], exp_w2[l], exp_b2[l], bm)
        ygs.append(yb.at[pos.T].get(mode="promise_in_bounds").reshape(TOP_K, nb, S, D))
        wtss.append(wts.reshape(nb, S, TOP_K))
    return _combine(x_new, ygs, wtss, g2, tm=min(S, 512))
```

```python
import functools
import math

import jax
import jax.numpy as jnp
import numpy as np
from jax import lax
from jax.experimental import pallas as pl
from jax.experimental.pallas import tpu as pltpu

F32 = jnp.float32
BF16 = jnp.bfloat16
HIGHEST = lax.Precision.HIGHEST

GRID_W = 64
DA_HEADS = 4
DA_QK_DIM = 64
DA_V_DIM = 2 * DA_QK_DIM
ROPE_THETA = 10000.0
ROPE_FREQS = DA_QK_DIM // 4
HY_DECAY_TARGET = 1e-2
HY_FAST_DECAY = 0.3
HY_SLOW_DECAY = 1.5
TOP_K = 4
SWIGLU_LIMIT = 7.0
SWIGLU_ALPHA = 1.702
EPS = 1e-6

LANES = 128
V7X_VMEM_LIMIT_BYTES = 48 * 1024 * 1024
V7X_VMEM_LIMIT_MOE_BYTES = 56 * 1024 * 1024
FFT_N2 = 128
ANCHOR_LAG = 2
EXPERT_BLOCK = 512
MOE_BATCH_GROUPS = 1


def _cparams(*sem):
    return pltpu.CompilerParams(dimension_semantics=sem, vmem_limit_bytes=V7X_VMEM_LIMIT_BYTES)


def _sigmoid(x):
    return 1.0 / (1.0 + jnp.exp(-x))


def _rms(x, width):
    return x * lax.rsqrt(jnp.sum(x * x, axis=-1, keepdims=True) * (1.0 / width) + EPS)


def _ada_kernel(c_ref, w_ref, b_ref, o_ref):
    c = c_ref[...]
    s = c * _sigmoid(c)
    o_ref[...] = jnp.dot(s, w_ref[...], preferred_element_type=F32, precision=HIGHEST) + b_ref[...]


def _ada(cc, w, b):
    R, D = cc.shape
    N = w.shape[1]
    tn = D
    return pl.pallas_call(
        _ada_kernel,
        out_shape=jax.ShapeDtypeStruct((R, N), F32),
        grid=(N // tn,),
        in_specs=[pl.BlockSpec((R, D), lambda j: (0, 0)),
                  pl.BlockSpec((D, tn), lambda j: (0, j)),
                  pl.BlockSpec((1, tn), lambda j: (0, j))],
        out_specs=pl.BlockSpec((R, tn), lambda j: (0, j)),
        compiler_params=_cparams("arbitrary"),
        name="ada",
    )(cc, w, b)


_TRANSPOSED_KINDS = ("q", "transposed")


def _in_kernel(spec, use_rope, qscale, x_ref, sc_ref, sh_ref, g_ref, w_ref, b_ref, qg_ref, kg_ref,
               bd_ref, *rest):
    if use_rope:
        cos_ref, sin_ref, *outs = rest
    else:
        outs = rest
    x = x_ref[0]
    h = _rms(x, x.shape[-1]) * g_ref[...]
    h = (h * (1.0 + sc_ref[0]) + sh_ref[0]).astype(BF16)
    for (kind, c0, cw), o_ref in zip(spec, outs):
        y = jnp.dot(h, w_ref[:, c0:c0 + cw], preferred_element_type=F32) + b_ref[:, c0:c0 + cw]
        if kind in ("q", "k", "kc"):
            ssq = jnp.dot((y * y).astype(BF16), bd_ref[...], preferred_element_type=F32)
            gn = qg_ref if kind == "q" else kg_ref
            y = y * lax.rsqrt(ssq * (1.0 / DA_QK_DIM) + EPS) * gn[...]
            if kind != "kc":
                lane = lax.broadcasted_iota(jnp.int32, y.shape, 1)
                partner = jnp.where((lane % (2 * ROPE_FREQS)) < ROPE_FREQS,
                                    pltpu.roll(y, cw - ROPE_FREQS, 1), pltpu.roll(y, ROPE_FREQS, 1))
                y = y * cos_ref[...] + partner * sin_ref[...]
            if kind == "q":
                y = y * qscale
        elif kind == "sigmoid":
            y = _sigmoid(y)
        if kind in _TRANSPOSED_KINDS:
            y = y.T
        o_ref[0] = y.astype(o_ref.dtype)


def _in_proj(x, sc, sh, g, w, b, qg, kg, bd, rope, spec, tm, qscale):
    B, S, D = x.shape
    N = w.shape[1]
    use_rope = rope is not None
    const2 = lambda i, bb: (0, 0)
    in_specs = [pl.BlockSpec((1, tm, D), lambda i, bb: (bb, i, 0)),
                pl.BlockSpec((1, 1, D), lambda i, bb: (bb, 0, 0)),
                pl.BlockSpec((1, 1, D), lambda i, bb: (bb, 0, 0)),
                pl.BlockSpec((1, D), const2),
                pl.BlockSpec((D, N), const2, pipeline_mode=pl.Buffered(1)),
                pl.BlockSpec((1, N), const2),
                pl.BlockSpec(qg.shape, const2),
                pl.BlockSpec(kg.shape, const2),
                pl.BlockSpec(bd.shape, const2)]
    args = [x, sc, sh, g, w, b, qg, kg, bd]
    if use_rope:
        cw = rope[0].shape[1]
        in_specs += [pl.BlockSpec((tm, cw), lambda i, bb: (i, 0))] * 2
        args += list(rope)
    out_shape = [jax.ShapeDtypeStruct((B, cw_, S) if kind in _TRANSPOSED_KINDS else (B, S, cw_), BF16)
                 for (kind, _, cw_) in spec]
    out_specs = [pl.BlockSpec((1, cw_, tm), lambda i, bb: (bb, 0, i)) if kind in _TRANSPOSED_KINDS
                 else pl.BlockSpec((1, tm, cw_), lambda i, bb: (bb, i, 0)) for (kind, _, cw_) in spec]
    return pl.pallas_call(
        functools.partial(_in_kernel, spec, use_rope, qscale),
        out_shape=out_shape,
        grid=(S // tm, B),
        in_specs=in_specs,
        out_specs=out_specs,
        compiler_params=_cparams("arbitrary", "arbitrary"),
        name="in_proj_rope" if use_rope else "in_proj_ctx",
    )(*args)


def _rope_tables(S, width):
    t = np.arange(S)
    pos = np.stack([t // GRID_W, t % GRID_W], axis=-1).astype(np.float32)
    freqs = (np.float32(ROPE_THETA) ** (-np.arange(ROPE_FREQS, dtype=np.float32) / ROPE_FREQS)).astype(np.float32)
    lane = np.arange(width)
    d = lane % DA_QK_DIM
    axis = d // (2 * ROPE_FREQS)
    half = (d % (2 * ROPE_FREQS)) // ROPE_FREQS
    f = d % ROPE_FREQS
    ang = (pos[:, axis] * freqs[f][None, :]).astype(np.float32)
    cos = np.cos(ang.astype(np.float64)).astype(np.float32)
    sin = np.sin(ang.astype(np.float64)).astype(np.float32)
    sin = np.where(half[None, :] == 0, -sin, sin)
    return jnp.asarray(cos), jnp.asarray(sin)


def _attn_kernel(tq, kc, lam_ref, q_ref, qn_ref, kx_ref, kc_ref, vtx_ref, vtc_ref, g_ref, o_ref,
                 st0, st1, mx0, mx1):
    n_lat = kx_ref.shape[1] // kc
    n_score = n_out = n_lat + kc_ref.shape[1] // kc
    rows_a = kc
    dv = vtx_ref.shape[1]
    ones = jnp.ones((16, kc), BF16)
    lam = lam_ref[0]

    def key_piece(j):
        return kx_ref[0, j * kc:(j + 1) * kc, :] if j < n_lat else kc_ref[0, (j - n_lat) * kc:(j - n_lat + 1) * kc, :]

    def value_piece(j):
        return vtx_ref[0, :, j * kc:(j + 1) * kc] if j < n_lat else vtc_ref[0, :, (j - n_lat) * kc:(j - n_lat + 1) * kc]

    def stacked(qt):
        row = lax.broadcasted_iota(jnp.int32, qt.shape, 0)
        zero = jnp.zeros_like(qt)
        return jnp.concatenate([jnp.where(row < DA_QK_DIM, qt, zero), jnp.where(row >= DA_QK_DIM, qt, zero)],
                               axis=1)

    def score_piece(qs, st, mx, j):
        r0 = j * rows_a
        st[r0:r0 + rows_a, :] = jnp.dot(key_piece(j), qs, preferred_element_type=F32)
        part = jnp.max(st[r0:r0 + rows_a, :].reshape(rows_a // 8, 8, st.shape[-1]), axis=0)
        mx[...] = part if j == 0 else jnp.maximum(mx[...], part)
        return part

    def out_piece(st, m, acc, j):
        e = jnp.exp2(st[j * kc:(j + 1) * kc, :] - m).astype(BF16)
        vt1 = jnp.concatenate([value_piece(j), ones], axis=0)
        d = jnp.dot(vt1, e, preferred_element_type=F32)
        return d if acc is None else acc + d

    def finish(acc, c):
        ot = acc[:dv] * (1.0 / acc[dv:dv + 1])
        o = (ot[:, :tq] - lam * ot[:, tq:]).T
        o = _rms(o, o.shape[-1]) * g_ref[...]
        o_ref[0, c * tq:(c + 1) * tq, :] = o.astype(o_ref.dtype)

    def overlapped(st_cur, mx_cur, q_next, st_next, mx_next, c):
        m = jnp.max(mx_cur[...], axis=0, keepdims=True)
        qs = stacked(q_next)
        acc = None
        p = 0
        parts = []
        for j in range(n_out):
            while p < n_score and p * n_out <= j * n_score:
                parts.append(score_piece(qs, st_next, mx_next, p))
                p += 1
            m_j = m + 0.0 * parts[j - ANCHOR_LAG][0:1, :] if ANCHOR_LAG <= j < len(parts) + ANCHOR_LAG else m
            acc = out_piece(st_cur, m_j, acc, j)
        for p in range(p, n_score):
            score_piece(qs, st_next, mx_next, p)
        finish(acc, c)

    @pl.when(pl.program_id(2) == 0)
    def _():
        qs = stacked(q_ref[0, :, 0:tq])
        for j in range(n_score):
            score_piece(qs, st0, mx0, j)

    overlapped(st0, mx0, q_ref[0, :, tq:2 * tq], st1, mx1, 0)
    overlapped(st1, mx1, qn_ref[0], st0, mx0, 1)


def _attention(lam, qt, kx, kc_, vtx, vtc, g, tq):
    B, W, S = qt.shape
    C = kc_.shape[1]
    Sk = S + C
    H = W // LANES
    tb = 2 * tq
    n_steps = S // tb
    kc = 256 if (S % 256 == 0 and C % 256 == 0) else LANES
    return pl.pallas_call(
        functools.partial(_attn_kernel, tq, kc),
        out_shape=jax.ShapeDtypeStruct((B, S, W), BF16),
        grid=(B, H, n_steps),
        in_specs=[pl.BlockSpec(memory_space=pltpu.SMEM),
                  pl.BlockSpec((1, LANES, tb), lambda b, h, i: (b, h, i)),
                  pl.BlockSpec((1, LANES, tq), lambda b, h, i: (b, h, jnp.minimum(i + 1, n_steps - 1) * 2)),
                  pl.BlockSpec((1, S, LANES), lambda b, h, i: (b, 0, h)),
                  pl.BlockSpec((1, C, LANES), lambda b, h, i: (b, 0, h)),
                  pl.BlockSpec((1, LANES, S), lambda b, h, i: (b, h, 0)),
                  pl.BlockSpec((1, LANES, C), lambda b, h, i: (b, h, 0)),
                  pl.BlockSpec((1, LANES), lambda b, h, i: (0, 0))],
        out_specs=pl.BlockSpec((1, tb, LANES), lambda b, h, i: (b, i, h)),
        scratch_shapes=[pltpu.VMEM((Sk, 2 * tq), F32), pltpu.VMEM((Sk, 2 * tq), F32),
                        pltpu.VMEM((8, 2 * tq), F32), pltpu.VMEM((8, 2 * tq), F32)],
        compiler_params=_cparams("arbitrary", "arbitrary", "arbitrary"),
        name="diff_attn",
    )(lam, qt, qt, kx, kc_, vtx, vtc, g)


def _filter_kernel(z_ref, w1, b1, w2, b2, w3, b3, w4, fr, dec_ref, o_ref):
    dot = functools.partial(jnp.dot, preferred_element_type=F32, precision=HIGHEST)
    f = fr[...]
    h = jnp.sin(f * (dot(z_ref[...], w1[...]) + b1[...]))
    h = jnp.sin(f * (dot(h, w2[...]) + b2[...]))
    h = jnp.sin(f * (dot(h, w3[...]) + b3[...]))
    taps = _dot_split(h, w4[...]) * dec_ref[...]
    tl, W = o_ref.shape[1:]
    o_ref[0] = taps[:, :W]
    pos = lax.broadcasted_iota(jnp.int32, (tl, W), 0) + pl.program_id(0) * tl
    o_ref[1] = jnp.where(pos == 0, 0.0, taps[:, W:])


def _implicit_filter(z, w1, b1, w2, b2, w3, b3, w4, fr, dec, tl):
    L = z.shape[0]
    N = w4.shape[1]
    full = lambda a: pl.BlockSpec(a.shape, lambda i: (0, 0))
    return pl.pallas_call(
        _filter_kernel,
        out_shape=jax.ShapeDtypeStruct((2, L, N // 2), F32),
        grid=(L // tl,),
        in_specs=[pl.BlockSpec((tl, z.shape[1]), lambda i: (i, 0)),
                  full(w1), full(b1), full(w2), full(b2), full(w3), full(b3), full(w4), full(fr),
                  pl.BlockSpec((tl, N), lambda i: (i, 0))],
        out_specs=pl.BlockSpec((2, tl, N // 2), lambda i: (0, i, 0)),
        compiler_params=_cparams("arbitrary"),
        name="hyena_filter",
    )(z, w1, b1, w2, b2, w3, b3, w4, fr, dec)


def _conv3(a, w_ref, b_ref):
    L = a.shape[0]
    row = lax.broadcasted_iota(jnp.int32, a.shape, 0)
    prev = jnp.where(row == 0, 0.0, pltpu.roll(a, 1, 0))
    nxt = jnp.where(row == L - 1, 0.0, pltpu.roll(a, L - 1, 0))
    return prev * w_ref[0:1, :] + a * w_ref[1:2, :] + nxt * w_ref[2:3, :] + b_ref[...]


def _gate_kernel(x1_ref, v_ref, w1_ref, wv_ref, b1_ref, bv_ref, u_ref):
    u = _conv3(x1_ref[0].astype(F32), w1_ref, b1_ref) * _conv3(v_ref[0].astype(F32), wv_ref, bv_ref)
    u_ref[0] = u.astype(u_ref.dtype)


def _hyena_gate(hy, conv_w, conv_b):
    B, L, W3 = hy.shape
    W = W3 // 3
    nw = W // LANES
    act = lambda off: pl.BlockSpec((1, L, LANES), lambda b, j: (b, 0, j + off))
    cw = lambda off: pl.BlockSpec((3, LANES), lambda b, j: (0, j + off))
    cb = lambda off: pl.BlockSpec((1, LANES), lambda b, j: (0, j + off))
    return pl.pallas_call(
        _gate_kernel,
        out_shape=jax.ShapeDtypeStruct((B, L, W), BF16),
        grid=(B, nw),
        in_specs=[act(nw), act(2 * nw), cw(nw), cw(2 * nw), cb(nw), cb(2 * nw)],
        out_specs=pl.BlockSpec((1, L, LANES), lambda b, j: (b, 0, j)),
        compiler_params=_cparams("arbitrary", "arbitrary"),
        name="hyena_gate",
    )(hy, hy, conv_w, conv_w, conv_b, conv_b)


def _dot_split(a, b):
    a_hi, b_hi = a.astype(BF16), b.astype(BF16)
    a_lo = (a - a_hi.astype(F32)).astype(BF16)
    b_lo = (b - b_hi.astype(F32)).astype(BF16)
    dot = functools.partial(jnp.dot, preferred_element_type=F32)
    return dot(a_hi, b_hi) + (dot(a_lo, b_hi) + dot(a_hi, b_lo))


def _outer_dft_kernel(split, f_ref, x_ref, o_ref):
    if split:
        o_ref[0] = _dot_split(f_ref[...], x_ref[0]).astype(o_ref.dtype)
    else:
        o_ref[0] = jnp.dot(f_ref[...], x_ref[0], preferred_element_type=F32).astype(o_ref.dtype)


def _outer_dft(f, x, out_dtype, tn, name):
    B, K, N = x.shape
    M = f.shape[0]
    return pl.pallas_call(
        functools.partial(_outer_dft_kernel, x.dtype == F32),
        out_shape=jax.ShapeDtypeStruct((B, M, N), out_dtype),
        grid=(B, N // tn),
        in_specs=[pl.BlockSpec((M, K), lambda b, j: (0, 0)),
                  pl.BlockSpec((1, K, tn), lambda b, j: (b, 0, j))],
        out_specs=pl.BlockSpec((1, M, tn), lambda b, j: (b, 0, j)),
        compiler_params=_cparams("arbitrary", "arbitrary"),
        name=name,
    )(f, x)


def _twiddle(re, im, tr, ti, conj):
    if conj:
        return re * tr + im * ti, im * tr - re * ti
    return re * tr - im * ti, re * ti + im * tr


def _inner_spectrum_kernel(kb, a_ref, tr_ref, ti_ref, f_ref, o_ref):
    reps = a_ref.shape[-1] // LANES
    for j in range(kb):
        tr = jnp.concatenate([tr_ref[j]] * reps, axis=-1)
        ti = jnp.concatenate([ti_ref[j]] * reps, axis=-1)
        xs = []
        for s in range(2):
            re, im = _twiddle(a_ref[s, 0, j], a_ref[s, 1, j], tr, ti, False)
            xs.append(_dot_split(f_ref[...], jnp.concatenate([re, im], axis=0)))
        n2 = xs[0].shape[0] // 2
        o_ref[0, 0, j] = xs[0][:n2] + xs[1][:n2]
        o_ref[0, 1, j] = xs[0][n2:] - xs[1][n2:]


def _inner_conv_kernel(kb, a_ref, kf_ref, tr_ref, ti_ref, f_ref, fi_ref, o_ref):
    reps = a_ref.shape[-1] // LANES
    for j in range(kb):
        tr = jnp.concatenate([tr_ref[j]] * reps, axis=-1)
        ti = jnp.concatenate([ti_ref[j]] * reps, axis=-1)
        re, im = _twiddle(a_ref[0, 0, j].astype(F32), a_ref[0, 1, j].astype(F32), tr, ti, False)
        n2 = re.shape[0]
        x = jnp.dot(f_ref[...], jnp.concatenate([re, im], axis=0).astype(BF16), preferred_element_type=F32)
        xr, xi = x[:n2], x[n2:]
        kr, ki = kf_ref[0, 0, j], kf_ref[0, 1, j]
        yr = xr * kr - xi * ki
        yi = xr * ki + xi * kr
        y = jnp.dot(fi_ref[...], jnp.concatenate([yr, yi], axis=0).astype(BF16), preferred_element_type=F32)
        re, im = _twiddle(y[:n2], y[n2:], tr, ti, True)
        o_ref[0, 0, j] = re.astype(o_ref.dtype)
        o_ref[0, 1, j] = im.astype(o_ref.dtype)


def _inner_stage(a5, kf5, tr, ti, f2, f2i, kb):
    B, _, N1, N2, W = a5.shape
    blk = lambda: pl.BlockSpec((1, 2, kb, N2, W), lambda g, b: (b, 0, g, 0, 0))
    tw = pl.BlockSpec((kb, N2, LANES), lambda g, b: (g, 0, 0))
    mat = pl.BlockSpec((2 * N2, 2 * N2), lambda g, b: (0, 0))
    if kf5 is None:
        kern = functools.partial(_inner_spectrum_kernel, kb)
        pair = pl.BlockSpec((2, 2, kb, N2, W), lambda g, b: (0, 0, g, 0, 0))
        in_specs, args, out_dtype, name = [pair, tw, tw, mat], (a5, tr, ti, f2), F32, "hyena_filter_spectrum"
        B = 1
    else:
        kern = functools.partial(_inner_conv_kernel, kb)
        kf_spec = pl.BlockSpec((1, 2, kb, N2, W), lambda g, b: (0, 0, g, 0, 0))
        in_specs, args, out_dtype, name = ([blk(), kf_spec, tw, tw, mat, mat], (a5, kf5, tr, ti, f2, f2i),
                                           BF16, "hyena_inner_conv")
    return pl.pallas_call(
        kern,
        out_shape=jax.ShapeDtypeStruct((B,) + a5.shape[1:], out_dtype),
        grid=(N1 // kb, B),
        in_specs=in_specs,
        out_specs=blk(),
        compiler_params=_cparams("arbitrary", "arbitrary"),
        name=name,
    )(*args)


def _hyena_out_kernel(x0_ref, w0_ref, b0_ref, y_ref, u_ref, hb_ref, o_ref):
    x0 = _conv3(x0_ref[0].astype(F32), w0_ref, b0_ref)
    o_ref[0] = (x0 * (y_ref[0] + u_ref[0].astype(F32) * hb_ref[...])).astype(o_ref.dtype)


def _hyena_out(hy, conv_w, conv_b, y, u, hbias):
    B, L, W = u.shape
    nw = W // LANES
    blk = pl.BlockSpec((1, L, LANES), lambda b, j: (b, 0, j))
    vec = lambda r: pl.BlockSpec((r, LANES), lambda b, j: (0, j))
    return pl.pallas_call(
        _hyena_out_kernel,
        out_shape=jax.ShapeDtypeStruct((B, L, W), BF16),
        grid=(B, nw),
        in_specs=[blk, vec(3), vec(1), blk, blk, vec(1)],
        out_specs=blk,
        compiler_params=_cparams("arbitrary", "arbitrary"),
        name="hyena_out",
    )(hy, conv_w, conv_b, y, u, hbias)


def _dft_constants(L):
    N = 2 * L
    N2 = FFT_N2
    N1 = N // N2
    k1 = np.arange(N1, dtype=np.float64)
    th1 = 2.0 * np.pi * np.outer(k1, k1) / N1
    fwd = np.concatenate([np.cos(th1), -np.sin(th1)], axis=0)
    inv = np.concatenate([np.cos(th1), -np.sin(th1)], axis=1)[: N1 // 2] / N
    n2 = np.arange(N2, dtype=np.float64)
    tw = 2.0 * np.pi * np.outer(k1, n2) / N
    tr = np.repeat(np.cos(tw)[:, :, None], LANES, axis=2)
    ti = np.repeat(-np.sin(tw)[:, :, None], LANES, axis=2)
    th2 = 2.0 * np.pi * np.outer(n2, n2) / N2
    c2, s2 = np.cos(th2), np.sin(th2)
    f2 = np.block([[c2, s2], [-s2, c2]])
    f2i = np.block([[c2, -s2], [s2, c2]])
    f = lambda a: jnp.asarray(a, F32)
    return dict(N1=N1, N2=N2, fwd=f(fwd), inv=f(inv), tr=f(tr), ti=f(ti), f2=f(f2), f2i=f(f2i))


def _filter_features(L, emb_dim):
    bands = (emb_dim - 1) // 2
    t = np.linspace(0.0, 1.0, L, dtype=np.float32)[:, None]
    w = (np.float32(2.0 * math.pi) * np.arange(L, dtype=np.float32)[:, None] / np.float32(L)).astype(np.float32)
    f = np.linspace(1e-4, bands - 1, bands, dtype=np.float32)
    fw = (f * w).astype(np.float32)
    z = np.concatenate([t, np.cos(fw.astype(np.float64)).astype(np.float32),
                        -np.sin(fw.astype(np.float64)).astype(np.float32)], axis=-1)
    return jnp.asarray(z), t


def _decay_window(t, W):
    deltas = np.abs(np.linspace(math.log(HY_DECAY_TARGET) / HY_SLOW_DECAY,
                                math.log(HY_DECAY_TARGET) / HY_FAST_DECAY, W, dtype=np.float32))
    dec = np.exp((-t * deltas).astype(np.float32).astype(np.float64)).astype(np.float32)
    return jnp.asarray(np.concatenate([dec, dec], axis=1))


def _hyena(hy, conv_w, conv_b, fw1, fb1, fw2, fb2, fw3, fb3, fw4, ffreq, hbias):
    B, L, W3 = hy.shape
    W = W3 // 3
    C = _dft_constants(L)
    N1, N2 = C["N1"], C["N2"]
    row = lambda a: a.reshape(1, -1)
    z, t = _filter_features(L, fw1.shape[0])
    kpad = LANES - fw1.shape[0]
    z = jnp.pad(z, ((0, 0), (0, kpad)))
    taps = _implicit_filter(z, jnp.pad(fw1, ((0, kpad), (0, 0))), row(fb1), fw2, row(fb2), fw3, row(fb3), fw4,
                            row(ffreq),
                            _decay_window(t, W), tl=min(L, 512))
    lanes = N2 * W
    tn = min(lanes, 8192)
    kf = _outer_dft(C["fwd"][:, : N1 // 2], taps.reshape(2, N1 // 2, lanes), F32, tn, "hyena_filter_outer")
    kf5 = _inner_stage(kf.reshape(2, 2, N1, N2, W), None, C["tr"], C["ti"], C["f2"], None, kb=min(N1, 2))
    u = _hyena_gate(hy, conv_w, row(conv_b))
    a = _outer_dft(C["fwd"][:, : N1 // 2].astype(BF16), u.reshape(B, N1 // 2, lanes), BF16, tn,
                   "hyena_outer_fwd")
    bb = _inner_stage(a.reshape(B, 2, N1, N2, W), kf5, C["tr"], C["ti"], C["f2"].astype(BF16),
                      C["f2i"].astype(BF16), kb=min(N1, 16))
    y = _outer_dft(C["inv"].astype(BF16), bb.reshape(B, 2 * N1, lanes), F32, tn, "hyena_outer_inv")
    return _hyena_out(hy, conv_w, row(conv_b), y.reshape(B, L, W), u, row(hbias))


ROUTE_ROWS = 16


def _route_tile(h, rw_ref, rb_ref, tri_ref, carry_ref):
    tm = h.shape[0]
    E = rw_ref.shape[0] // 2
    nt = (((1,), (1,)), ((), ()))
    h_hi = h.astype(BF16)
    h_lo = (h - h_hi.astype(F32)).astype(BF16)
    a = lax.dot_general(rw_ref[...], h_hi, nt, preferred_element_type=F32)
    b = lax.dot_general(rw_ref[0:E, :], h_lo, nt, preferred_element_type=F32)
    reps = tm // LANES
    logits = a[:E] + a[E:] + b + jnp.concatenate([rb_ref[...]] * reps, axis=1)
    row = lax.broadcasted_iota(jnp.int32, (E, tm), 0)
    v = logits
    ids, vals = [], []
    for _ in range(TOP_K):
        m = jnp.max(v, axis=0, keepdims=True)
        idx = jnp.min(jnp.where(v == m, row, E), axis=0, keepdims=True)
        ids.append(idx)
        vals.append(m)
        v = jnp.where(row == idx, -jnp.inf, v)
    ex = [jnp.exp(vk - vals[0]) for vk in vals]
    den = ex[0]
    for t in ex[1:]:
        den = den + t
    hot = [row == idx for idx in ids]
    member = jnp.zeros((E, tm), F32)
    for hk in hot:
        member = jnp.where(hk, 1.0, member)
    member = member.astype(BF16)
    carry = carry_ref[...]
    before = (jnp.dot(member, tri_ref[...], preferred_element_type=F32)
              + jnp.concatenate([carry] * reps, axis=1))
    ranks = [jnp.sum(jnp.where(hk, before, 0.0), axis=0, keepdims=True) for hk in hot]
    carry_ref[...] = carry + jnp.dot(member, jnp.ones((tm, LANES), BF16), preferred_element_type=F32)
    out_row = lax.broadcasted_iota(jnp.int32, (ROUTE_ROWS, tm), 0)
    rec = jnp.zeros((ROUTE_ROWS, tm), F32)
    for k in range(TOP_K):
        rec = jnp.where(out_row == k, ids[k].astype(F32), rec)
        rec = jnp.where(out_row == TOP_K + k, ex[k] / den, rec)
        rec = jnp.where(out_row == 2 * TOP_K + k, ranks[k], rec)
    return rec


def _merge_kernel(x_ref, oa_ref, ob_ref, gt_ref, g1_ref, sc_ref, sh_ref, ng_ref, wa_ref, wb_ref, wo_ref,
                  rw_ref, rb_ref, tri_ref, xo_ref, h_ref, rec_ref, cnt_ref, before_ref, carry_ref):
    @pl.when((pl.program_id(0) == 0) & (pl.program_id(1) == 0))
    def _():
        carry_ref[...] = jnp.zeros_like(carry_ref)

    @pl.when(pl.program_id(1) == 0)
    def _():
        before_ref[0] = carry_ref[...]

    D = x_ref.shape[-1]
    gt = gt_ref[0]
    m = (gt[:, :D].astype(F32) * jnp.dot(oa_ref[0], wa_ref[...], preferred_element_type=F32)
         + gt[:, D:].astype(F32) * jnp.dot(ob_ref[0], wb_ref[...], preferred_element_type=F32))
    y = jnp.dot(m.astype(BF16), wo_ref[...], preferred_element_type=F32)
    xn = x_ref[0] + g1_ref[0] * y
    xo_ref[0] = xn
    h = _rms(xn, D) * ng_ref[...]
    h = h * (1.0 + sc_ref[0]) + sh_ref[0]
    h_ref[0] = h.astype(h_ref.dtype)
    rec_ref[0] = _route_tile(h, rw_ref, rb_ref, tri_ref, carry_ref)
    cnt_ref[...] = carry_ref[...]


def _merge(x, oa, ob, gates, g1, sc2, sh2, ng, wa, wb, wo, rw, rb, tm):
    B, S, D = x.shape
    E = rw.shape[1]
    rw_hi = rw.T.astype(BF16)
    rw2 = jnp.concatenate([rw_hi, (rw.T - rw_hi.astype(F32)).astype(BF16)], axis=0)
    rb2 = jnp.broadcast_to(rb.reshape(E, 1), (E, LANES))
    tri = jnp.asarray(np.triu(np.ones((tm, tm), np.float32), 1), BF16)
    act = lambda w: pl.BlockSpec((1, tm, w), lambda b, i: (b, i, 0))
    mod = pl.BlockSpec((1, 1, D), lambda b, i: (b, 0, 0))
    full = lambda a: pl.BlockSpec(a.shape, lambda b, i: (0, 0))
    return pl.pallas_call(
        _merge_kernel,
        out_shape=[jax.ShapeDtypeStruct((B, S, D), F32), jax.ShapeDtypeStruct((B, S, D), BF16),
                   jax.ShapeDtypeStruct((B, ROUTE_ROWS, S), F32), jax.ShapeDtypeStruct((E, LANES), F32),
                   jax.ShapeDtypeStruct((B, E, LANES), F32)],
        grid=(B, S // tm),
        in_specs=[act(D), act(oa.shape[-1]), act(ob.shape[-1]), act(2 * D), mod, mod, mod, full(ng),
                  full(wa), full(wb), full(wo), full(rw2), full(rb2), full(tri)],
        out_specs=[act(D), act(D), pl.BlockSpec((1, ROUTE_ROWS, tm), lambda b, i: (b, 0, i)),
                   pl.BlockSpec((E, LANES), lambda b, i: (0, 0)),
                   pl.BlockSpec((1, E, LANES), lambda b, i: (b, 0, 0))],
        scratch_shapes=[pltpu.VMEM((E, LANES), F32)],
        compiler_params=_cparams("arbitrary", "arbitrary"),
        name="merge_norm_router",
    )(x, oa, ob, gates, g1, sc2, sh2, ng, wa, wb, wo, rw2, rb2, tri)


def _moe_kernel(be_ref, nu_ref, x_ref, w1_ref, b1_ref, w2_ref, b2_ref, o_ref, w1b_ref, w2b_ref):
    i = pl.program_id(0)

    @pl.when(i < nu_ref[0])
    def _():
        @pl.when((i == 0) | (be_ref[i] != be_ref[jnp.maximum(i - 1, 0)]))
        def _():
            w1b_ref[...] = w1_ref[0].astype(BF16)
            w2b_ref[...] = w2_ref[0].astype(BF16)

        F = w2_ref.shape[1]
        gu = jnp.dot(x_ref[...], w1b_ref[...], preferred_element_type=F32) + b1_ref[0]
        gate = jnp.minimum(gu[:, :F], SWIGLU_LIMIT)
        up = jnp.clip(gu[:, F:], -SWIGLU_LIMIT, SWIGLU_LIMIT)
        act = (up + 1.0) * gate * _sigmoid(SWIGLU_ALPHA * gate)
        y = jnp.dot(act.astype(BF16), w2b_ref[...], preferred_element_type=F32) + b2_ref[0]
        o_ref[...] = y.astype(o_ref.dtype)


def _moe_experts(block_e, n_used, xg, w1, b1, w2, b2, bm):
    R, D = xg.shape
    E, _, F2 = w1.shape
    F = w2.shape[1]
    grid_spec = pltpu.PrefetchScalarGridSpec(
        num_scalar_prefetch=2,
        grid=(R // bm,),
        in_specs=[pl.BlockSpec((bm, D), lambda i, be, nu: (i, 0)),
                  pl.BlockSpec((1, D, F2), lambda i, be, nu: (be[i], 0, 0)),
                  pl.BlockSpec((1, 1, F2), lambda i, be, nu: (be[i], 0, 0)),
                  pl.BlockSpec((1, F, D), lambda i, be, nu: (be[i], 0, 0)),
                  pl.BlockSpec((1, 1, D), lambda i, be, nu: (be[i], 0, 0))],
        out_specs=pl.BlockSpec((bm, D), lambda i, be, nu: (i, 0)),
        scratch_shapes=[pltpu.VMEM((D, F2), BF16), pltpu.VMEM((F, D), BF16)],
    )
    return pl.pallas_call(
        _moe_kernel,
        out_shape=jax.ShapeDtypeStruct((R, D), BF16),
        grid_spec=grid_spec,
        compiler_params=pltpu.CompilerParams(dimension_semantics=("arbitrary",),
                                             vmem_limit_bytes=V7X_VMEM_LIMIT_MOE_BYTES),
        name="moe_experts",
    )(block_e, n_used, xg, w1, b1.reshape(E, 1, F2), w2, b2.reshape(E, 1, D))


def _combine_kernel(nb, x_ref, g_ref, *refs):
    o_ref = refs[-1]
    b = pl.program_id(0)
    for grp in range((len(refs) - 1) // 2):
        y_ref, w_ref = refs[2 * grp:2 * grp + 2]

        @pl.when((b >= grp * nb) & (b < (grp + 1) * nb))
        def _():
            w = w_ref[0]
            acc = w[:, 0:1] * y_ref[0, 0].astype(F32)
            for k in range(1, y_ref.shape[0]):
                acc = acc + w[:, k:k + 1] * y_ref[k, 0].astype(F32)
            o_ref[0] = x_ref[0] + g_ref[0] * acc


def _combine(x, ygs, wtss, g2, tm):
    B, S, D = x.shape
    K, nb = ygs[0].shape[:2]
    in_specs = [pl.BlockSpec((1, tm, D), lambda b, i: (b, i, 0)), pl.BlockSpec((1, 1, D), lambda b, i: (b, 0, 0))]
    args = [x, g2]

    def group_block(grp, b, i):
        mine = (b >= grp * nb) & (b < (grp + 1) * nb)
        return jnp.where(mine, b - grp * nb, 0), jnp.where(mine, i, 0)

    for grp in range(len(ygs)):
        at = functools.partial(group_block, grp)
        in_specs += [pl.BlockSpec((K, 1, tm, D), lambda b, i, at=at: (0, *at(b, i), 0)),
                     pl.BlockSpec((1, tm, K), lambda b, i, at=at: (*at(b, i), 0))]
        args += [ygs[grp], wtss[grp]]
    return pl.pallas_call(
        functools.partial(_combine_kernel, nb),
        out_shape=jax.ShapeDtypeStruct((B, S, D), F32),
        grid=(B, S // tm),
        in_specs=in_specs,
        out_specs=pl.BlockSpec((1, tm, D), lambda b, i: (b, i, 0)),
        compiler_params=_cparams("arbitrary", "arbitrary"),
        name="moe_combine",
    )(*args)


def _route(rec, counts, before, tok0, bm):
    B, _, S = rec.shape
    T = B * S
    E = counts.shape[0]
    I32 = jnp.int32
    experts = jnp.arange(E, dtype=I32)
    pick = lambda table, idx: jnp.sum(jnp.where(idx[..., None] == experts, table, 0), axis=-1)
    rec = jnp.swapaxes(rec, 1, 2).reshape(T, ROUTE_ROWS)
    top_i = rec[:, 0:TOP_K].astype(I32)
    wts = rec[:, TOP_K:2 * TOP_K]
    rank = rec[:, 2 * TOP_K:3 * TOP_K].astype(I32) - pick(before.astype(I32), top_i)
    counts = counts.astype(I32)
    A = T * TOP_K
    padded = (counts + bm - 1) // bm * bm
    pend = jnp.cumsum(padded)
    pstart = pend - padded
    start = jnp.cumsum(counts) - counts
    pos = pick(pstart, top_i) + rank
    keys = top_i.reshape(-1) * A + jnp.arange(A, dtype=I32)
    order = jnp.sort(keys) % A
    n_rows = A + E * bm
    n_blocks = n_rows // bm
    blk_start = jnp.arange(n_blocks, dtype=I32) * bm
    block_e = jnp.minimum(jnp.sum((pend[None, :] <= blk_start[:, None]).astype(I32), axis=1), E - 1)
    off = blk_start[:, None] + jnp.arange(bm, dtype=I32)[None, :] - pick(pstart, block_e)[:, None]
    valid = (off >= 0) & (off < pick(counts, block_e)[:, None])
    src = jnp.clip(pick(start, block_e)[:, None] + off, 0, A - 1)
    sorted_tok = order.at[src.reshape(-1)].get(mode="promise_in_bounds").reshape(n_blocks, bm) // TOP_K
    row_tok = jnp.where(valid, sorted_tok, (blk_start[:, None] + jnp.arange(bm, dtype=I32)[None, :]) % T)
    n_used = (pend[-1:] // bm).astype(I32)
    return wts, row_tok.reshape(-1) + tok0, pos, block_e.astype(I32), n_used


def kernel(x, c, ctx, c_ctx, ada_w, ada_b, norm1_g, norm2_g, w_in, b_in, q_norm_g, k_norm_g, lambda_q1, lambda_k1, lambda_q2, lambda_k2, subln_g, conv_w, conv_b, filt_w1, filt_b1, filt_w2, filt_b2, filt_w3, filt_b3, filt_w4, filt_freq, hyena_bias, w_up_a, w_up_b, w_out, router_w, router_b, exp_w1, exp_b1, exp_w2, exp_b2):
    B, S, D = x.shape
    depth = ada_w.shape[0]
    assert depth == 1, "context-stream update between layers is not implemented"
    l = 0
    QC = DA_HEADS * 2 * DA_QK_DIM
    W = D // 2
    row = lambda a: a.reshape(1, -1)

    lambda_init = 0.8 - 0.6 * math.exp(-0.3 * l)
    lam = (jnp.exp(jnp.sum(lambda_q1[l] * lambda_k1[l])) - jnp.exp(jnp.sum(lambda_q2[l] * lambda_k2[l]))
           + lambda_init).reshape(1).astype(F32)

    R = -(-(B + 1) // 8) * 8
    cc = jnp.concatenate([c, c_ctx[None, :], jnp.zeros((R - B - 1, D), F32)], axis=0)
    mods = _ada(cc, ada_w[l], row(ada_b[l]))
    sh1, sc1, g1, sh2, sc2, g2 = [mods[:B, i * D:(i + 1) * D].reshape(B, 1, D) for i in range(6)]
    sh1c, sc1c = [jnp.broadcast_to(mods[B, i * D:(i + 1) * D].reshape(1, 1, D), (B, 1, D)) for i in range(2)]

    w_in_b = w_in[l].astype(BF16)
    qg = row(jnp.tile(q_norm_g[l], QC // DA_QK_DIM))
    kg = row(jnp.tile(k_norm_g[l], QC // DA_QK_DIM))
    grp = np.arange(QC) // DA_QK_DIM
    bd = jnp.asarray(grp[:, None] == grp[None, :], BF16)
    qscale = DA_QK_DIM ** -0.5 * math.log2(math.e)
    spec = (("q", 0, QC), ("k", QC, QC), ("transposed", 2 * QC, QC), ("plain", 3 * QC, 3 * W),
            ("sigmoid", 3 * QC + 3 * W, 2 * D))
    qtx, kx, vtx, hy, gates = _in_proj(x, sc1, sh1, row(norm1_g[l]), w_in_b, row(b_in[l]), qg, kg, bd,
                                       _rope_tables(S, QC), spec, tm=min(S, 512), qscale=qscale)
    spec_c = (("kc", 0, QC), ("transposed", QC, QC))
    kc, vtc = _in_proj(ctx, sc1c, sh1c, row(norm1_g[l]), w_in_b[:, QC:3 * QC], row(b_in[l][QC:3 * QC]), qg,
                       kg, bd, None, spec_c, tm=ctx.shape[1], qscale=qscale)

    oa = _attention(lam, qtx, kx, kc, vtx, vtc, row(subln_g[l] * (1.0 - lambda_init)), tq=min(S // 2, 256))
    ob = _hyena(hy, conv_w[l], conv_b[l], filt_w1[l], filt_b1[l], filt_w2[l], filt_b2[l], filt_w3[l],
                filt_b3[l], filt_w4[l], filt_freq[l], hyena_bias[l])
    x_new, h2, rec, counts, before = _merge(x, oa, ob, gates, g1, sc2, sh2, row(norm2_g[l]),
                                            w_up_a[l].astype(BF16), w_up_b[l].astype(BF16), w_out[l].astype(BF16),
                                            router_w[l], row(router_b[l]), tm=min(S, 512))

    groups = MOE_BATCH_GROUPS if B % MOE_BATCH_GROUPS == 0 else 1
    nb = B // groups
    bm = EXPERT_BLOCK
    edges = jnp.concatenate([before[:, :, 0], counts[None, :, 0]], axis=0)
    ygs, wtss = [], []
    for grp in range(groups):
        b0 = grp * nb
        wts, row_tok, pos, block_e, n_used = _route(rec[b0:b0 + nb], edges[b0 + nb] - edges[b0], edges[b0],
                                                    b0 * S, bm)
        xg = h2.reshape(B * S, D).at[row_tok].get(mode="promise_in_bounds")
        yb = _moe_experts(block_e, n_used, xg, exp_w1[l], exp_b1[l], exp_w2[l], exp_b2[l], bm)
        ygs.append(yb.at[pos.T].get(mode="promise_in_bounds").reshape(TOP_K, nb, S, D))
        wtss.append(wts.reshape(nb, S, TOP_K))
    return _combine(x_new, ygs, wtss, g2, tm=min(S, 512))
```
